```python
import math
import jax
import jax.numpy as jnp
from jax import lax
import numpy as np

D_MODEL = 1024
BATCH = 2
SEQ = 8192
DEPTH = 2
DEC_BATCH = 8
DEC_SEQ = 32
PAST_LEN = 2048

CHUNK = 64
Q_BLOCK = 128
N_EVEN = (DEPTH + 1) // 2
N_ODD = DEPTH // 2
EPS = 1e-6
SSD_D_INNER = D_MODEL
SSD_HEADDIM = 64
SSD_HEADS = SSD_D_INNER // SSD_HEADDIM
SSD_GROUPS = 4
SSD_STATE = 128
SSD_CONV = 4
SSD_XBC = SSD_D_INNER + 2 * SSD_GROUPS * SSD_STATE
MLA_HEADS = 16
MLA_NOPE = 64
MLA_ROPE = 32
MLA_V = 64
MLA_Q_RANK = 384
MLA_KV_RANK = 256
ROPE_THETA = 10000.0
GDN_HEADS = 8
GDN_DK = 128
GDN_DV = 128
GDN_CONV = 4
GDN_QKV = GDN_HEADS * (2 * GDN_DK + GDN_DV)
SC_WIDTH = D_MODEL
SC_CONV = 3
N_EXPERTS = 32
TOP_K = 4
D_FF = D_MODEL
SWIGLU_LIMIT = 7.0
SWIGLU_ALPHA = 1.702
MOE_BLOCK = 128
IN_EVEN = SSD_D_INNER + SSD_XBC + SSD_HEADS + MLA_Q_RANK + MLA_KV_RANK + MLA_ROPE
OUT_EVEN = SSD_D_INNER + MLA_HEADS * MLA_V
IN_ODD = GDN_QKV + GDN_HEADS * GDN_DV + 2 * GDN_HEADS + 3 * SC_WIDTH
OUT_ODD = GDN_HEADS * GDN_DV + SC_WIDTH
STATE_NAMES = ('mla_ckv', 'mla_krope', 'ssd', 'ssd_conv', 'gdn', 'gdn_conv', 'sconv')

kernel_name = 'hybrid_ssd_mla_gdn_shortconv_moe_stream_step'


def rmsnorm(x, g):
    xf = x.astype(jnp.float32)
    y = xf * lax.rsqrt(jnp.mean(xf * xf, axis=-1, keepdims=True) + EPS)
    return (y * g.astype(jnp.float32)).astype(x.dtype)


def l2norm(x):
    xf = x.astype(jnp.float32)
    return xf * lax.rsqrt(jnp.sum(xf * xf, axis=-1, keepdims=True) + EPS)


def split_cols(t, sizes):
    out, off = [], 0
    for s in sizes:
        out.append(t[..., off:off + s])
        off += s
    return out


def causal_conv(u, buf, w):
    width, L = w.shape[0], u.shape[1]
    full = jnp.concatenate([buf.astype(u.dtype), u], axis=1)
    out = full[:, 0:L] * w[0]
    for k in range(1, width):
        out = out + full[:, k:k + L] * w[k]
    return out, full[:, L:]


def rope(x, pos):
    half = MLA_ROPE // 2
    inv = ROPE_THETA ** (-jnp.arange(half, dtype=jnp.float32) / half)
    ang = pos.astype(jnp.float32)[:, None] * inv[None, :]
    cos, sin = jnp.cos(ang), jnp.sin(ang)
    if x.ndim == 4:
        cos, sin = cos[:, None], sin[:, None]
    xf = x.astype(jnp.float32)
    x1, x2 = xf[..., :half], xf[..., half:]
    return jnp.concatenate([x1 * cos - x2 * sin, x2 * cos + x1 * sin], axis=-1).astype(x.dtype)


def ssd_scan(x, a, bm, cm, s0):
    Bsz, L, H, P = x.shape
    G, N = bm.shape[2], bm.shape[3]
    R = H // G
    T = min(CHUNK, L)
    nc = L // T
    x = x.reshape(Bsz, nc, T, G, R, P)
    a = a.reshape(Bsz, nc, T, G, R)
    bm = bm.reshape(Bsz, nc, T, G, N)
    cm = cm.reshape(Bsz, nc, T, G, N)
    acum = jnp.cumsum(a, axis=2)
    causal = jnp.tril(jnp.ones((T, T), bool))
    seg = acum[:, :, :, None] - acum[:, :, None, :]
    lmat = jnp.exp(jnp.where(causal[None, None, :, :, None, None], seg, -jnp.inf))
    cb = jnp.einsum('bclgn,bcsgn->bclsg', cm, bm)
    y_diag = jnp.einsum('bclsgr,bcsgrp->bclgrp', cb[..., None] * lmat, x)
    decay_end = jnp.exp(acum[:, :, -1:] - acum)
    chunk_states = jnp.einsum('bclgn,bclgrp->bcgrpn', bm, x * decay_end[..., None])
    chunk_decay = jnp.exp(acum[:, :, -1])

    def step(s, inp):
        cs, cd = inp
        return s * cd[..., None, None] + cs, s

    s_last, s_in = lax.scan(step, s0.reshape(Bsz, G, R, P, N),
                            (jnp.moveaxis(chunk_states, 1, 0), jnp.moveaxis(chunk_decay, 1, 0)))
    s_in = jnp.moveaxis(s_in, 0, 1)
    y_off = jnp.einsum('bclgn,bcgrpn->bclgrp', cm, s_in) * jnp.exp(acum)[..., None]
    return (y_diag + y_off).reshape(Bsz, L, H, P), s_last.reshape(Bsz, H, P, N)


def gdn_scan(q, k, v, g, beta, s0):
    Bsz, L, H, K = q.shape
    V = v.shape[-1]
    T = min(CHUNK, L)
    nc = L // T

    def chunked(t):
        return jnp.moveaxis(t.reshape(Bsz, nc, T, H, *t.shape[3:]), 3, 2)

    q, k, v, g, beta = (chunked(t) for t in (q, k, v, g, beta))
    gc = jnp.cumsum(g, axis=-1)
    diff = gc[..., :, None] - gc[..., None, :]
    incl = jnp.tril(jnp.ones((T, T), bool))
    strict = jnp.tril(jnp.ones((T, T), bool), -1)
    decay_incl = jnp.exp(jnp.where(incl, diff, -jnp.inf))
    kk = jnp.einsum('bchlk,bchsk->bchls', k, k)
    a_mat = jnp.where(strict, beta[..., :, None] * kk * decay_incl, 0.0)
    lhs = a_mat + jnp.eye(T, dtype=a_mat.dtype)
    rhs = jnp.concatenate([v * beta[..., None], k * (beta * jnp.exp(gc))[..., None]], axis=-1)
    sol = lax.linalg.triangular_solve(lhs, rhs, left_side=True, lower=True, unit_diagonal=True)
    u, w = sol[..., :V], sol[..., V:]
    qk = jnp.einsum('bchlk,bchsk->bchls', q, k) * decay_incl
    q_dec = q * jnp.exp(gc)[..., None]
    k_dec = k * jnp.exp(gc[..., -1:] - gc)[..., None]
    last = jnp.exp(gc[..., -1])

    def step(s, inp):
        u_c, w_c, qk_c, qd_c, kd_c, last_c = inp
        v_new = u_c - jnp.einsum('bhlk,bhkv->bhlv', w_c, s)
        o = jnp.einsum('bhlk,bhkv->bhlv', qd_c, s) + jnp.einsum('bhls,bhsv->bhlv', qk_c, v_new)
        s = s * last_c[..., None, None] + jnp.einsum('bhlk,bhlv->bhkv', kd_c, v_new)
        return s, o

    xs = tuple(jnp.moveaxis(t, 1, 0) for t in (u, w, qk, q_dec, k_dec, last))
    s_last, o = lax.scan(step, s0, xs)
    o = jnp.moveaxis(jnp.moveaxis(o, 0, 1), 2, 3).reshape(Bsz, L, H, V)
    return o, s_last


def chunk_causal_attention(q_nope, q_pe, k_nope, k_pe, v, pos_q, pos_k):
    scale = (MLA_NOPE + MLA_ROPE) ** -0.5
    k_chunk = pos_k // CHUNK

    def attend(args):
        qn, qp, pq = args
        s = jnp.einsum('bqhd,bkhd->bhqk', qn, k_nope) + jnp.einsum('bqhd,bkd->bhqk', qp, k_pe)
        s = s.astype(jnp.float32) * scale
        visible = k_chunk[None, :] <= (pq // CHUNK)[:, None]
        p = jax.nn.softmax(jnp.where(visible, s, -jnp.inf), axis=-1).astype(v.dtype)
        return jnp.einsum('bhqk,bkhd->bqhd', p, v)

    Bsz, L = q_nope.shape[0], q_nope.shape[1]
    if L > Q_BLOCK and L % Q_BLOCK == 0:
        nb = L // Q_BLOCK

        def blocks(t):
            return jnp.moveaxis(t.reshape(Bsz, nb, Q_BLOCK, *t.shape[2:]), 1, 0)

        o = lax.map(attend, (blocks(q_nope), blocks(q_pe), pos_q.reshape(nb, Q_BLOCK)))
        return jnp.moveaxis(o, 0, 1).reshape(Bsz, L, *o.shape[3:])
    return attend((q_nope, q_pe, pos_q))


def mla_attention(lat_q, lat_kv, pos_q, pos_past, ckv_past, kpe_past, q_norm, w_q_up, kv_norm, w_kv_up):
    Bsz, L, _ = lat_q.shape
    q = (rmsnorm(lat_q, q_norm) @ w_q_up).reshape(Bsz, L, MLA_HEADS, MLA_NOPE + MLA_ROPE)
    q_nope = q[..., :MLA_NOPE]
    q_pe = rope(q[..., MLA_NOPE:], pos_q)
    ckv_new = rmsnorm(lat_kv[..., :MLA_KV_RANK], kv_norm)
    kpe_new = rope(lat_kv[..., MLA_KV_RANK:], pos_q)
    ckv = jnp.concatenate([ckv_past.astype(ckv_new.dtype), ckv_new], axis=1)
    kpe = jnp.concatenate([kpe_past.astype(kpe_new.dtype), kpe_new], axis=1)
    pos_k = jnp.concatenate([pos_past, pos_q])
    kv = jnp.einsum('bsr,rhd->bshd', ckv, w_kv_up.reshape(MLA_KV_RANK, MLA_HEADS, MLA_NOPE + MLA_V))
    o = chunk_causal_attention(q_nope, q_pe, kv[..., :MLA_NOPE], kpe, kv[..., MLA_NOPE:], pos_q, pos_k)
    return o.reshape(Bsz, L, MLA_HEADS * MLA_V), ckv_new, kpe_new


def even_mixer(h, pos_q, pos_past, ckv_past, kpe_past, ssd_s0, ssd_buf, w_in, w_out, conv_w, conv_b,
               dt_bias, a_log, d_skip, ssd_norm, q_norm, w_q_up, kv_norm, w_kv_up):
    f32 = jnp.float32
    Bsz, L, _ = h.shape
    z, xbc, dt_raw, lat_q, lat_kv = split_cols(
        h @ w_in, (SSD_D_INNER, SSD_XBC, SSD_HEADS, MLA_Q_RANK, MLA_KV_RANK + MLA_ROPE))
    xbc_c, ssd_buf_new = causal_conv(xbc, ssd_buf, conv_w)
    xbc_c = jax.nn.silu(xbc_c + conv_b).astype(f32)
    xs, bm, cm = split_cols(xbc_c, (SSD_D_INNER, SSD_GROUPS * SSD_STATE, SSD_GROUPS * SSD_STATE))
    xs = xs.reshape(Bsz, L, SSD_HEADS, SSD_HEADDIM)
    bm = bm.reshape(Bsz, L, SSD_GROUPS, SSD_STATE)
    cm = cm.reshape(Bsz, L, SSD_GROUPS, SSD_STATE)
    dt = jax.nn.softplus(dt_raw.astype(f32) + dt_bias.astype(f32))
    a = -jnp.exp(a_log.astype(f32))
    y, ssd_s_new = ssd_scan(xs * dt[..., None], dt * a, bm, cm, ssd_s0.astype(f32))
    y = (y + xs * d_skip.astype(f32)[:, None]).reshape(Bsz, L, SSD_D_INNER)
    y = y * jax.nn.silu(z.astype(f32))
    y = rmsnorm(y.reshape(Bsz, L, SSD_GROUPS, SSD_D_INNER // SSD_GROUPS),
                ssd_norm.reshape(SSD_GROUPS, -1)).reshape(Bsz, L, SSD_D_INNER).astype(h.dtype)
    o, ckv_new, kpe_new = mla_attention(lat_q, lat_kv, pos_q, pos_past, ckv_past, kpe_past,
                                        q_norm, w_q_up, kv_norm, w_kv_up)
    mix = jnp.concatenate([y, o.astype(h.dtype)], axis=-1) @ w_out
    return mix, ckv_new, kpe_new, ssd_s_new.astype(ssd_s0.dtype), ssd_buf_new


def odd_mixer(h, gdn_s0, gdn_buf, sc_buf, w_in, w_out, conv_w, dt_bias, a_log, gdn_norm, sconv_w):
    f32 = jnp.float32
    Bsz, L, _ = h.shape
    qkv, gate, b_raw, a_raw, sc_b, sc_c, sc_v = split_cols(
        h @ w_in, (GDN_QKV, GDN_HEADS * GDN_DV, GDN_HEADS, GDN_HEADS, SC_WIDTH, SC_WIDTH, SC_WIDTH))
    qkv_c, gdn_buf_new = causal_conv(qkv, gdn_buf, conv_w)
    qkv_c = jax.nn.silu(qkv_c).astype(f32)
    q, k, v = split_cols(qkv_c, (GDN_HEADS * GDN_DK, GDN_HEADS * GDN_DK, GDN_HEADS * GDN_DV))
    q = l2norm(q.reshape(Bsz, L, GDN_HEADS, GDN_DK)) * (GDN_DK ** -0.5)
    k = l2norm(k.reshape(Bsz, L, GDN_HEADS, GDN_DK))
    v = v.reshape(Bsz, L, GDN_HEADS, GDN_DV)
    beta = jax.nn.sigmoid(b_raw.astype(f32))
    g = -jnp.exp(a_log.astype(f32)) * jax.nn.softplus(a_raw.astype(f32) + dt_bias.astype(f32))
    o, gdn_s_new = gdn_scan(q, k, v, g, beta, gdn_s0.astype(f32))
    o = rmsnorm(o, gdn_norm) * jax.nn.silu(gate.astype(f32).reshape(Bsz, L, GDN_HEADS, GDN_DV))
    o = o.reshape(Bsz, L, GDN_HEADS * GDN_DV).astype(h.dtype)
    sc_conv, sc_buf_new = causal_conv(sc_c * sc_v, sc_buf, sconv_w)
    sc_out = sc_b * sc_conv
    mix = jnp.concatenate([o, sc_out], axis=-1) @ w_out
    return mix, gdn_s_new.astype(gdn_s0.dtype), gdn_buf_new, sc_buf_new


def moe_ffn(h, w_router, b_router, w_gu, b_gu, w_down, b_down):
    Bsz, L, D = h.shape
    hf = h.reshape(-1, D)
    n_tok = hf.shape[0]
    n_as = n_tok * TOP_K
    logits = hf.astype(jnp.float32) @ w_router.astype(jnp.float32) + b_router.astype(jnp.float32)
    top_v, top_e = lax.top_k(logits, TOP_K)
    gates = jax.nn.softmax(top_v, axis=-1)
    flat_e = top_e.reshape(-1)
    flat_t = jnp.repeat(jnp.arange(n_tok, dtype=jnp.int32), TOP_K)
    order = jnp.argsort(flat_e)
    se, st, sg = flat_e[order], flat_t[order], gates.reshape(-1)[order]
    counts = jnp.bincount(flat_e, length=N_EXPERTS)
    padded = (counts + MOE_BLOCK - 1) // MOE_BLOCK * MOE_BLOCK
    start = jnp.cumsum(counts) - counts
    pend = jnp.cumsum(padded)
    pstart = pend - padded
    dest = pstart[se] + jnp.arange(n_as, dtype=jnp.int32) - start[se]
    n_rows = (-(-n_as // MOE_BLOCK) + N_EXPERTS) * MOE_BLOCK
    n_blocks = n_rows // MOE_BLOCK
    rows_t = jnp.zeros((n_rows,), jnp.int32).at[dest].set(st)
    rows_g = jnp.zeros((n_rows,), jnp.float32).at[dest].set(sg)
    blk_e = jnp.minimum(jnp.searchsorted(pend, jnp.arange(n_blocks, dtype=pend.dtype) * MOE_BLOCK, side='right'),
                        N_EXPERTS - 1)
    xin = hf[rows_t].reshape(n_blocks, MOE_BLOCK, D)

    def expert_block(args):
        xb, e = args
        gu = xb @ w_gu[e] + b_gu[e]
        gate = jnp.minimum(gu[:, :D_FF], SWIGLU_LIMIT)
        up = jnp.clip(gu[:, D_FF:], -SWIGLU_LIMIT, SWIGLU_LIMIT)
        act = (up + 1.0) * gate * jax.nn.sigmoid(SWIGLU_ALPHA * gate)
        return act @ w_down[e] + b_down[e]

    out = lax.map(expert_block, (xin, blk_e)).reshape(n_rows, D)
    out = out * rows_g[:, None].astype(out.dtype)
    y = jnp.zeros_like(hf).at[rows_t].add(out.astype(hf.dtype))
    return y.reshape(Bsz, L, D)


def run_trunk(x, c, pos_q, pos_past, mla_ckv, mla_kpe, ssd_s, ssd_buf, gdn_s, gdn_buf, sc_buf, W):
    new = {name: [] for name in STATE_NAMES}
    Bsz = x.shape[0]
    cs = jax.nn.silu(c)
    for i in range(DEPTH):
        mod = (cs @ W['w_ada'][i] + W['b_ada'][i]).reshape(Bsz, 6, 1, D_MODEL)
        shift_m, scale_m, gate_m, shift_f, scale_f, gate_f = (mod[:, m] for m in range(6))
        h = rmsnorm(x, W['norm_mix'][i]) * (1.0 + scale_m) + shift_m
        j = i // 2
        if i % 2 == 0:
            mix, ckv_n, kpe_n, s_n, buf_n = even_mixer(
                h, pos_q, pos_past, mla_ckv[j], mla_kpe[j], ssd_s[j], ssd_buf[j],
                W['ev_w_in'][j], W['ev_w_out'][j], W['ssd_conv_w'][j], W['ssd_conv_b'][j],
                W['ssd_dt_bias'][j], W['ssd_a_log'][j], W['ssd_d'][j], W['ssd_norm'][j],
                W['mla_q_norm'][j], W['mla_w_q_up'][j], W['mla_kv_norm'][j], W['mla_w_kv_up'][j])
            new['mla_ckv'].append(ckv_n)
            new['mla_krope'].append(kpe_n)
            new['ssd'].append(s_n)
            new['ssd_conv'].append(buf_n)
        else:
            mix, s_n, buf_n, scb_n = odd_mixer(
                h, gdn_s[j], gdn_buf[j], sc_buf[j], W['od_w_in'][j], W['od_w_out'][j],
                W['gdn_conv_w'][j], W['gdn_dt_bias'][j], W['gdn_a_log'][j], W['gdn_norm'][j], W['sconv_w'][j])
            new['gdn'].append(s_n)
            new['gdn_conv'].append(buf_n)
            new['sconv'].append(scb_n)
        x = x + gate_m * mix
        h = rmsnorm(x, W['norm_ffn'][i]) * (1.0 + scale_f) + shift_f
        x = x + gate_f * moe_ffn(h, W['w_router'][i], W['b_router'][i], W['w_gu'][i], W['b_gu'][i],
                                 W['w_down'][i], W['b_down'][i])
    y = rmsnorm(x, W['norm_final'])
    return y, {name: jnp.stack(v) for name, v in new.items()}


def setup_inputs(seed: int = 0) -> dict:
    key = jax.random.key(seed)
    ks = iter(jax.random.split(key, 64))
    f32 = jnp.float32

    def nrm(shape, scale):
        return jax.random.normal(next(ks), shape, f32) * scale

    def gain(shape):
        return 1.0 + nrm(shape, 0.02)

    def dt_bias(shape):
        dt = jnp.exp(jax.random.uniform(next(ks), shape, f32, math.log(1e-3), math.log(1e-1)))
        return dt + jnp.log(-jnp.expm1(-dt))

    def a_log(shape):
        return jnp.log(jax.random.uniform(next(ks), shape, f32, 1.0, 16.0))

    return {
        'x_prompt': nrm((BATCH, SEQ, D_MODEL), 1.0),
        'x_sample': nrm((DEC_BATCH, DEC_SEQ, D_MODEL), 1.0),
        'c_prompt': nrm((BATCH, D_MODEL), 1.0),
        'c_sample': nrm((DEC_BATCH, D_MODEL), 1.0),
        'cache_mla_ckv': nrm((N_EVEN, DEC_BATCH, PAST_LEN, MLA_KV_RANK), 1.0),
        'cache_mla_krope': nrm((N_EVEN, DEC_BATCH, PAST_LEN, MLA_ROPE), 1.0),
        'state_ssd': nrm((N_EVEN, DEC_BATCH, SSD_HEADS, SSD_HEADDIM, SSD_STATE), 0.1),
        'state_ssd_conv': nrm((N_EVEN, DEC_BATCH, SSD_CONV - 1, SSD_XBC), 1.0),
        'state_gdn': nrm((N_ODD, DEC_BATCH, GDN_HEADS, GDN_DK, GDN_DV), 0.1),
        'state_gdn_conv': nrm((N_ODD, DEC_BATCH, GDN_CONV - 1, GDN_QKV), 1.0),
        'state_sconv': nrm((N_ODD, DEC_BATCH, SC_CONV - 1, SC_WIDTH), 1.0),
        'norm_mix': gain((DEPTH, D_MODEL)),
        'norm_ffn': gain((DEPTH, D_MODEL)),
        'w_ada': nrm((DEPTH, D_MODEL, 6 * D_MODEL), 0.5 * D_MODEL ** -0.5),
        'b_ada': nrm((DEPTH, 6 * D_MODEL), 0.02),
        'w_router': nrm((DEPTH, D_MODEL, N_EXPERTS), D_MODEL ** -0.5),
        'b_router': nrm((DEPTH, N_EXPERTS), 0.01),
        'w_gu': nrm((DEPTH, N_EXPERTS, D_MODEL, 2 * D_FF), D_MODEL ** -0.5),
        'b_gu': nrm((DEPTH, N_EXPERTS, 2 * D_FF), 0.01),
        'w_down': nrm((DEPTH, N_EXPERTS, D_FF, D_MODEL), D_FF ** -0.5),
        'b_down': nrm((DEPTH, N_EXPERTS, D_MODEL), 0.01),
        'norm_final': gain((D_MODEL,)),
        'ev_w_in': nrm((N_EVEN, D_MODEL, IN_EVEN), D_MODEL ** -0.5),
        'ev_w_out': nrm((N_EVEN, OUT_EVEN, D_MODEL), OUT_EVEN ** -0.5),
        'ssd_conv_w': nrm((N_EVEN, SSD_CONV, SSD_XBC), SSD_CONV ** -0.5),
        'ssd_conv_b': nrm((N_EVEN, SSD_XBC), 0.02),
        'ssd_dt_bias': dt_bias((N_EVEN, SSD_HEADS)),
        'ssd_a_log': a_log((N_EVEN, SSD_HEADS)),
        'ssd_d': 1.0 + nrm((N_EVEN, SSD_HEADS), 0.1),
        'ssd_norm': gain((N_EVEN, SSD_D_INNER)),
        'mla_q_norm': gain((N_EVEN, MLA_Q_RANK)),
        'mla_w_q_up': nrm((N_EVEN, MLA_Q_RANK, MLA_HEADS * (MLA_NOPE + MLA_ROPE)), MLA_Q_RANK ** -0.5),
        'mla_kv_norm': gain((N_EVEN, MLA_KV_RANK)),
        'mla_w_kv_up': nrm((N_EVEN, MLA_KV_RANK, MLA_HEADS * (MLA_NOPE + MLA_V)), MLA_KV_RANK ** -0.5),
        'od_w_in': nrm((N_ODD, D_MODEL, IN_ODD), D_MODEL ** -0.5),
        'od_w_out': nrm((N_ODD, OUT_ODD, D_MODEL), OUT_ODD ** -0.5),
        'gdn_conv_w': nrm((N_ODD, GDN_CONV, GDN_QKV), GDN_CONV ** -0.5),
        'gdn_dt_bias': dt_bias((N_ODD, GDN_HEADS)),
        'gdn_a_log': a_log((N_ODD, GDN_HEADS)),
        'gdn_norm': gain((N_ODD, GDN_DV)),
        'sconv_w': nrm((N_ODD, SC_CONV, SC_WIDTH), SC_CONV ** -0.5),
    }


def reference(x_prompt, x_sample, c_prompt, c_sample,
              cache_mla_ckv, cache_mla_krope, state_ssd, state_ssd_conv, state_gdn, state_gdn_conv, state_sconv,
              norm_mix, norm_ffn, w_ada, b_ada, w_router, b_router, w_gu, b_gu, w_down, b_down, norm_final,
              ev_w_in, ev_w_out, ssd_conv_w, ssd_conv_b, ssd_dt_bias, ssd_a_log, ssd_d, ssd_norm,
              mla_q_norm, mla_w_q_up, mla_kv_norm, mla_w_kv_up,
              od_w_in, od_w_out, gdn_conv_w, gdn_dt_bias, gdn_a_log, gdn_norm, sconv_w):
    W = dict(norm_mix=norm_mix, norm_ffn=norm_ffn, w_ada=w_ada, b_ada=b_ada, w_router=w_router,
             b_router=b_router, w_gu=w_gu, b_gu=b_gu, w_down=w_down, b_down=b_down, norm_final=norm_final,
             ev_w_in=ev_w_in, ev_w_out=ev_w_out, ssd_conv_w=ssd_conv_w, ssd_conv_b=ssd_conv_b,
             ssd_dt_bias=ssd_dt_bias, ssd_a_log=ssd_a_log, ssd_d=ssd_d, ssd_norm=ssd_norm,
             mla_q_norm=mla_q_norm, mla_w_q_up=mla_w_q_up, mla_kv_norm=mla_kv_norm, mla_w_kv_up=mla_w_kv_up,
             od_w_in=od_w_in, od_w_out=od_w_out, gdn_conv_w=gdn_conv_w, gdn_dt_bias=gdn_dt_bias,
             gdn_a_log=gdn_a_log, gdn_norm=gdn_norm, sconv_w=sconv_w)
    bp, dtp = x_prompt.shape[0], x_prompt.dtype
    y_prompt, sp = run_trunk(
        x_prompt, c_prompt, jnp.arange(x_prompt.shape[1], dtype=jnp.int32), jnp.arange(0, dtype=jnp.int32),
        jnp.zeros((N_EVEN, bp, 0, MLA_KV_RANK), dtp), jnp.zeros((N_EVEN, bp, 0, MLA_ROPE), dtp),
        jnp.zeros((N_EVEN, bp, SSD_HEADS, SSD_HEADDIM, SSD_STATE), dtp),
        jnp.zeros((N_EVEN, bp, SSD_CONV - 1, SSD_XBC), dtp),
        jnp.zeros((N_ODD, bp, GDN_HEADS, GDN_DK, GDN_DV), dtp),
        jnp.zeros((N_ODD, bp, GDN_CONV - 1, GDN_QKV), dtp),
        jnp.zeros((N_ODD, bp, SC_CONV - 1, SC_WIDTH), dtp), W)
    past = cache_mla_ckv.shape[2]
    y_sample, ss = run_trunk(
        x_sample, c_sample, past + jnp.arange(x_sample.shape[1], dtype=jnp.int32),
        jnp.arange(past, dtype=jnp.int32), cache_mla_ckv, cache_mla_krope, state_ssd, state_ssd_conv,
        state_gdn, state_gdn_conv, state_sconv, W)
    return (y_prompt, y_sample,
            sp['mla_ckv'], ss['mla_ckv'], sp['mla_krope'], ss['mla_krope'],
            sp['ssd'], ss['ssd'], sp['ssd_conv'], ss['ssd_conv'],
            sp['gdn'], ss['gdn'], sp['gdn_conv'], ss['gdn_conv'],
            sp['sconv'], ss['sconv'])
```

```python
import functools
import math

import jax
import jax.numpy as jnp
from jax import lax
from jax.experimental import pallas as pl
from jax.experimental.pallas import tpu as pltpu

F32 = jnp.float32
BF16 = jnp.bfloat16
I32 = jnp.int32
HI = lax.Precision.HIGHEST

D_MODEL = 1024
DEPTH = 2
CHUNK = 64
CHUNK_SHIFT = 6
EPS = 1e-6
SSD_D_INNER = D_MODEL
SSD_HEADDIM = 64
SSD_HEADS = SSD_D_INNER // SSD_HEADDIM
SSD_GROUPS = 4
SSD_STATE = 128
SSD_CONV = 4
SSD_XBC = SSD_D_INNER + 2 * SSD_GROUPS * SSD_STATE
MLA_HEADS = 16
MLA_NOPE = 64
MLA_ROPE = 32
MLA_V = 64
MLA_Q_RANK = 384
MLA_KV_RANK = 256
ROPE_THETA = 10000.0
GDN_HEADS = 8
GDN_DK = 128
GDN_DV = 128
GDN_CONV = 4
GDN_QKV = GDN_HEADS * (2 * GDN_DK + GDN_DV)
SC_WIDTH = D_MODEL
SC_CONV = 3
N_EXPERTS = 32
TOP_K = 4
D_FF = D_MODEL
SWIGLU_LIMIT = 7.0
SWIGLU_ALPHA = 1.702

LANES = 128
SUBLANES = 8
VMEM_LIMIT = 56 * 1024 * 1024

NEG_BIG = -1e30
MOE_BM = 256


def _cparams(sem):
    return pltpu.CompilerParams(dimension_semantics=sem, vmem_limit_bytes=VMEM_LIMIT)


def _sigmoid(x):
    return 1.0 / (1.0 + jnp.exp(-x))


def _silu(x):
    return x * _sigmoid(x)


def _softplus(x):
    return jnp.maximum(x, 0.0) + jnp.log1p(jnp.exp(-jnp.abs(x)))


def _dot(a, b, hi=False):
    if hi:
        return jnp.dot(a.astype(F32), b.astype(F32), precision=HI, preferred_element_type=F32)
    return jnp.dot(a.astype(BF16), b.astype(BF16), preferred_element_type=F32)


def _dot_nt(a, b, hi=False):
    dims = (((1,), (1,)), ((), ()))
    if hi:
        return lax.dot_general(a.astype(F32), b.astype(F32), dims, precision=HI, preferred_element_type=F32)
    return lax.dot_general(a.astype(BF16), b.astype(BF16), dims, preferred_element_type=F32)


def _act_dtype(hi):
    return F32 if hi else BF16


def _dot_hi(a, b):
    return jnp.dot(a, b, precision=HI, preferred_element_type=F32)


def _dot_nt_hi(a, b):
    return lax.dot_general(a, b, (((1,), (1,)), ((), ())), precision=HI, preferred_element_type=F32)


def _rope_rot(p, cos_t, sin_a, sin_b):
    return p * cos_t + pltpu.roll(p, LANES - MLA_ROPE // 2, 1) * sin_a + pltpu.roll(p, MLA_ROPE // 2, 1) * sin_b


def _ada_kernel(c_ref, w_ref, b_ref, o_ref):
    c = c_ref[...]
    o_ref[0] = _dot_hi(_silu(c), w_ref[0]) + b_ref[0]


def _ada_mod(c_all, w_ada, b_ada):
    nb = c_all.shape[0]
    return pl.pallas_call(
        _ada_kernel,
        grid=(DEPTH, 6),
        in_specs=[pl.BlockSpec((nb, D_MODEL), lambda i, j: (0, 0)),
                  pl.BlockSpec((1, D_MODEL, D_MODEL), lambda i, j: (i, 0, j)),
                  pl.BlockSpec((1, 1, D_MODEL), lambda i, j: (i, 0, j))],
        out_specs=pl.BlockSpec((1, nb, D_MODEL), lambda i, j: (i, 0, j)),
        out_shape=jax.ShapeDtypeStruct((DEPTH, nb, 6 * D_MODEL), F32),
        compiler_params=_cparams(("parallel", "parallel")),
        name="ada_mod",
    )(c_all, w_ada, b_ada.reshape(DEPTH, 1, 6 * D_MODEL))


def _in_kernel(x_ref, mod_ref, g_ref, w_ref, *out_refs, segs, shift_row, scale_row, hi):
    x = x_ref[0]
    h = x * lax.rsqrt(jnp.mean(x * x, axis=-1, keepdims=True) + EPS) * g_ref[...]
    h = h * (1.0 + mod_ref[0, scale_row:scale_row + 1, :]) + mod_ref[0, shift_row:shift_row + 1, :]
    r = _dot(h, w_ref[...], hi)
    for (off, width), o_ref in zip(segs, out_refs):
        o_ref[0] = r[:, off:off + width].astype(o_ref.dtype)


def _in_proj(x, mod, g, w, segs, dtypes, shift_row, scale_row, tm, hi):
    B, L, _ = x.shape
    n_p = w.shape[1]
    tm = min(tm, L)
    return pl.pallas_call(
        functools.partial(_in_kernel, segs=segs, shift_row=shift_row, scale_row=scale_row, hi=hi),
        grid=(B, L // tm),
        in_specs=[pl.BlockSpec((1, tm, D_MODEL), lambda b, i: (b, i, 0)),
                  pl.BlockSpec((1, 6, D_MODEL), lambda b, i: (b, 0, 0)),
                  pl.BlockSpec((1, D_MODEL), lambda b, i: (0, 0)),
                  pl.BlockSpec((D_MODEL, n_p), lambda b, i: (0, 0))],
        out_specs=[pl.BlockSpec((1, tm, wd), lambda b, i: (b, i, 0)) for _, wd in segs],
        out_shape=[jax.ShapeDtypeStruct((B, L, wd), dt) for (_, wd), dt in zip(segs, dtypes)],
        compiler_params=_cparams(("parallel", "parallel")),
        name="in_proj",
    )(x, mod, g, w)


def _ssd_kernel(z_ref, xbc_ref, dtr_ref, cst_ref, s0_ref, cw_ref, cb_ref, dtb_ref, alog_ref, dsk_ref,
                nrm_ref, e_ref, y_ref, sout_ref, cout_ref, state_sc, cbuf, *, T, n_valid, hi):
    c = pl.program_id(1)
    nc = pl.num_programs(1)
    halo = SSD_CONV - 1
    base = SUBLANES - halo

    @pl.when(c == 0)
    def _():
        state_sc[...] = s0_ref[0]
        cbuf[base:SUBLANES, :] = cst_ref[0]

    cbuf[SUBLANES:SUBLANES + T, :] = xbc_ref[0]
    conv = cb_ref[...] + cw_ref[0:1, :] * cbuf[base:base + T, :]
    for k in range(1, SSD_CONV):
        conv = conv + cw_ref[k:k + 1, :] * cbuf[base + k:base + k + T, :]
    xc = _silu(conv)

    c_last = (n_valid - 1) // T
    nv_last = n_valid - c_last * T

    @pl.when(c == c_last)
    def _():
        cout_ref[0] = cbuf[base + nv_last:base + nv_last + halo, :]

    cbuf[base:SUBLANES, :] = cbuf[base + T:SUBLANES + T, :]

    xs = xc[:, :SSD_D_INNER]
    gn = SSD_GROUPS * SSD_STATE
    bm = xc[:, SSD_D_INNER:SSD_D_INNER + gn]
    cm = xc[:, SSD_D_INNER + gn:]

    tok = lax.broadcasted_iota(I32, (T, 1), 0) + c * T
    dt = jnp.where(tok < n_valid, _softplus(dtr_ref[0] + dtb_ref[...]), 0.0)
    a = dt * (-jnp.exp(alog_ref[...]))
    ri = lax.broadcasted_iota(I32, (T, T), 0)
    ci = lax.broadcasted_iota(I32, (T, T), 1)
    causal = ci <= ri
    tril = jnp.where(causal, 1.0, 0.0).astype(F32)
    acum = _dot_hi(tril, a)
    eye = jnp.where(lax.broadcasted_iota(I32, (LANES, LANES), 0) == lax.broadcasted_iota(I32, (LANES, LANES), 1),
                    1.0, 0.0).astype(F32)
    acum_t = _dot_nt_hi(eye, acum)
    a_last = acum[T - 1:T, :]
    e = e_ref[...]
    xdt = xs * _dot_hi(dt, e)
    eacum_x = jnp.exp(_dot_hi(acum, e))
    xdend = xdt * jnp.exp(_dot_hi(a_last - acum, e))

    r = SSD_HEADS // SSD_GROUPS
    gw = r * SSD_HEADDIM
    y_groups = []
    for g in range(SSD_GROUPS):
        bg = bm[:, g * SSD_STATE:(g + 1) * SSD_STATE]
        cg = cm[:, g * SSD_STATE:(g + 1) * SSD_STATE]
        cb_mat = _dot_nt(cg, bg, hi)
        ys = []
        for j in range(r):
            h = g * r + j
            seg = acum[:, h:h + 1] - acum_t[h:h + 1, :]
            lm = jnp.where(causal, jnp.exp(jnp.minimum(seg, 0.0)), 0.0)
            ys.append(_dot(cb_mat * lm, xdt[:, h * SSD_HEADDIM:(h + 1) * SSD_HEADDIM], hi))
        y_diag = jnp.concatenate(ys, axis=-1)
        s_g = state_sc[g * gw:(g + 1) * gw, :]
        y_off = _dot_nt(cg, s_g, hi) * eacum_x[:, g * gw:(g + 1) * gw]
        y_groups.append(y_diag + y_off)
        cs = _dot(jnp.transpose(xdend[:, g * gw:(g + 1) * gw]), bg, hi)
        dec = jnp.concatenate(
            [jnp.broadcast_to(jnp.exp(acum_t[g * r + j:g * r + j + 1, T - 1:T]), (SSD_HEADDIM, SSD_STATE))
             for j in range(r)], axis=0)
        state_sc[g * gw:(g + 1) * gw, :] = s_g * dec + cs
    y = jnp.concatenate(y_groups, axis=-1)
    y = y + xs * dsk_ref[...]
    y = y * _silu(z_ref[0])
    outs = []
    for g in range(SSD_GROUPS):
        yg = y[:, g * gw:(g + 1) * gw]
        outs.append(yg * lax.rsqrt(jnp.mean(yg * yg, axis=-1, keepdims=True) + EPS))
    y_ref[0] = (jnp.concatenate(outs, axis=-1) * nrm_ref[...]).astype(y_ref.dtype)

    @pl.when(c == nc - 1)
    def _():
        sout_ref[0] = state_sc[...]


def _ssd(z, xbc, dtr, cst, s0, cw, cb, dtb, alog, dsk, nrm, e, T, n_valid, hi):
    B, Lp, _ = z.shape
    hp = SSD_HEADS * SSD_HEADDIM
    row = lambda b, c: (0, 0)
    return pl.pallas_call(
        functools.partial(_ssd_kernel, T=T, n_valid=n_valid, hi=hi),
        grid=(B, Lp // T),
        in_specs=[pl.BlockSpec((1, T, SSD_D_INNER), lambda b, c: (b, c, 0)),
                  pl.BlockSpec((1, T, SSD_XBC), lambda b, c: (b, c, 0)),
                  pl.BlockSpec((1, T, LANES), lambda b, c: (b, c, 0)),
                  pl.BlockSpec((1, SSD_CONV - 1, SSD_XBC), lambda b, c: (b, 0, 0)),
                  pl.BlockSpec((1, hp, SSD_STATE), lambda b, c: (b, 0, 0)),
                  pl.BlockSpec((SSD_CONV, SSD_XBC), row),
                  pl.BlockSpec((1, SSD_XBC), row),
                  pl.BlockSpec((1, LANES), row),
                  pl.BlockSpec((1, LANES), row),
                  pl.BlockSpec((1, SSD_D_INNER), row),
                  pl.BlockSpec((1, SSD_D_INNER), row),
                  pl.BlockSpec((LANES, SSD_D_INNER), row)],
        out_specs=[pl.BlockSpec((1, T, SSD_D_INNER), lambda b, c: (b, c, 0)),
                   pl.BlockSpec((1, hp, SSD_STATE), lambda b, c: (b, 0, 0)),
                   pl.BlockSpec((1, SSD_CONV - 1, SSD_XBC), lambda b, c: (b, 0, 0))],
        out_shape=[jax.ShapeDtypeStruct((B, Lp, SSD_D_INNER), _act_dtype(hi)),
                   jax.ShapeDtypeStruct((B, hp, SSD_STATE), F32),
                   jax.ShapeDtypeStruct((B, SSD_CONV - 1, SSD_XBC), F32)],
        scratch_shapes=[pltpu.VMEM((hp, SSD_STATE), F32),
                        pltpu.VMEM((T + SUBLANES, SSD_XBC), F32)],
        compiler_params=_cparams(("parallel", "arbitrary")),
        name="ssd_scan",
    )(z, xbc, dtr, cst, s0, cw, cb, dtb, alog, dsk, nrm, e)


def _latkv_kernel(lat_ref, g_ref, cos_ref, sa_ref, sb_ref, ckv_ref, kpe_ref):
    lat = lat_ref[0]
    cr = lat[:, :MLA_KV_RANK]
    ckv_ref[0] = cr * lax.rsqrt(jnp.mean(cr * cr, axis=-1, keepdims=True) + EPS) * g_ref[...]
    kpe_ref[0] = _rope_rot(lat[:, MLA_KV_RANK:], cos_ref[...], sa_ref[...], sb_ref[...])


def _latkv_post(latkv, g, cos_t, sin_a, sin_b, tm):
    B, L, wp = latkv.shape
    tm = min(tm, L)
    tab = pl.BlockSpec((tm, LANES), lambda b, i: (i, 0))
    return pl.pallas_call(
        _latkv_kernel,
        grid=(B, L // tm),
        in_specs=[pl.BlockSpec((1, tm, wp), lambda b, i: (b, i, 0)),
                  pl.BlockSpec((1, MLA_KV_RANK), lambda b, i: (0, 0)), tab, tab, tab],
        out_specs=[pl.BlockSpec((1, tm, MLA_KV_RANK), lambda b, i: (b, i, 0)),
                   pl.BlockSpec((1, tm, LANES), lambda b, i: (b, i, 0))],
        out_shape=[jax.ShapeDtypeStruct((B, L, MLA_KV_RANK), F32),
                   jax.ShapeDtypeStruct((B, L, LANES), F32)],
        compiler_params=_cparams(("parallel", "parallel")),
        name="mla_latent_kv",
    )(latkv, g, cos_t, sin_a, sin_b)


def _q_kernel(lat_ref, g_ref, w_ref, cos_ref, sa_ref, sb_ref, q_ref, *, scale, hi):
    lat = lat_ref[0]
    n = lat * lax.rsqrt(jnp.mean(lat * lat, axis=-1, keepdims=True) + EPS) * g_ref[...]
    q = _dot(n, w_ref[0], hi)
    a = q[:, :LANES] * scale
    p = _rope_rot(q[:, LANES:], cos_ref[...], sa_ref[...], sb_ref[...]) * scale
    q_ref[0, 0] = jnp.concatenate([a, p], axis=-1).astype(q_ref.dtype)


def _q_proj(latq, g, wq, cos_t, sin_a, sin_b, tm, hi):
    B, L, _ = latq.shape
    tm = min(tm, L)
    tab = pl.BlockSpec((tm, LANES), lambda b, i, h: (i, 0))
    scale = (MLA_NOPE + MLA_ROPE) ** -0.5
    return pl.pallas_call(
        functools.partial(_q_kernel, scale=scale, hi=hi),
        grid=(B, L // tm, MLA_HEADS),
        in_specs=[pl.BlockSpec((1, tm, MLA_Q_RANK), lambda b, i, h: (b, i, 0)),
                  pl.BlockSpec((1, MLA_Q_RANK), lambda b, i, h: (0, 0)),
                  pl.BlockSpec((1, MLA_Q_RANK, 2 * LANES), lambda b, i, h: (h, 0, 0)), tab, tab, tab],
        out_specs=pl.BlockSpec((1, 1, tm, 2 * LANES), lambda b, i, h: (b, h, i, 0)),
        out_shape=jax.ShapeDtypeStruct((B, MLA_HEADS, L, 2 * LANES), _act_dtype(hi)),
        compiler_params=_cparams(("parallel", "parallel", "parallel")),
        name="mla_q_proj",
    )(latq, g, wq, cos_t, sin_a, sin_b)


def _kvup_kernel(ckv_ref, kpe_ref, w_ref, kv_ref, *, hi):
    kv = _dot(ckv_ref[0], w_ref[0], hi)
    kv_ref[0, 0] = jnp.concatenate([kv, kpe_ref[0]], axis=-1).astype(kv_ref.dtype)


def _kv_up(ckv, kpe, wkv, tm, hi):
    B, Lk, _ = ckv.shape
    tm = min(tm, Lk)
    return pl.pallas_call(
        functools.partial(_kvup_kernel, hi=hi),
        grid=(B, Lk // tm, MLA_HEADS),
        in_specs=[pl.BlockSpec((1, tm, MLA_KV_RANK), lambda b, i, h: (b, i, 0)),
                  pl.BlockSpec((1, tm, LANES), lambda b, i, h: (b, i, 0)),
                  pl.BlockSpec((1, MLA_KV_RANK, LANES), lambda b, i, h: (h, 0, 0))],
        out_specs=pl.BlockSpec((1, 1, tm, 2 * LANES), lambda b, i, h: (b, h, i, 0)),
        out_shape=jax.ShapeDtypeStruct((B, MLA_HEADS, Lk, 2 * LANES), _act_dtype(hi)),
        compiler_params=_cparams(("parallel", "parallel", "parallel")),
        name="mla_kv_up",
    )(ckv, kpe, wkv)


def _attn_kernel(q_ref, kv_ref, o_ref, *, tq, tk, q_off, kv_len, hi):
    q0 = pl.program_id(2) * tq
    kmax = jnp.minimum(kv_len, ((q_off + q0 + tq - 1) // CHUNK + 1) * CHUNK)
    nblk = (kmax + tk - 1) // tk
    qchunk = jnp.right_shift(q_off + q0 + lax.broadcasted_iota(I32, (tq, 1), 0), CHUNK_SHIFT)
    kiota = lax.broadcasted_iota(I32, (1, tk), 1)
    outs = []
    for hd in range(2):
        q = q_ref[0, hd]

        def body(j, carry, hd=hd, q=q):
            m, l, acc = carry
            k0 = pl.multiple_of(j * tk, tk)
            k = kv_ref[0, hd, pl.ds(k0, tk), :]
            s = _dot_nt(q, k, hi)
            kpos = k0 + kiota
            vis = jnp.logical_and(jnp.right_shift(kpos, CHUNK_SHIFT) <= qchunk, kpos < kv_len)
            s = jnp.where(vis, s, NEG_BIG)
            m_new = jnp.maximum(m, jnp.max(s, axis=-1, keepdims=True))
            alpha = jnp.exp(m - m_new)
            p = jnp.exp(s - m_new)
            l = alpha * l + jnp.sum(p, axis=-1, keepdims=True)
            acc = alpha * acc + _dot(p, k[:, :LANES], hi)
            return m_new, l, acc

        m0 = jnp.full((tq, 1), NEG_BIG, F32)
        l0 = jnp.zeros((tq, 1), F32)
        a0 = jnp.zeros((tq, LANES), F32)
        _, l, acc = lax.fori_loop(0, nblk, body, (m0, l0, a0))
        outs.append(acc / l)
    lane = lax.broadcasted_iota(I32, (tq, LANES), 1)
    o = jnp.where(lane < MLA_V, pltpu.roll(outs[0], MLA_V, 1), outs[1])
    o_ref[0] = o.astype(o_ref.dtype)


def _attention(q, kv, tq, tk, q_off, kv_len, hi):
    B, H, L, _ = q.shape
    Lk = kv.shape[2]
    return pl.pallas_call(
        functools.partial(_attn_kernel, tq=tq, tk=tk, q_off=q_off, kv_len=kv_len, hi=hi),
        grid=(B, H // 2, L // tq),
        in_specs=[pl.BlockSpec((1, 2, tq, 2 * LANES), lambda b, h, i: (b, h, i, 0)),
                  pl.BlockSpec((1, 2, Lk, 2 * LANES), lambda b, h, i: (b, h, 0, 0))],
        out_specs=pl.BlockSpec((1, tq, LANES), lambda b, h, i: (b, i, h)),
        out_shape=jax.ShapeDtypeStruct((B, L, H * MLA_V), _act_dtype(hi)),
        compiler_params=_cparams(("parallel", "parallel", "parallel")),
        name="mla_attention",
    )(q, kv)


def _gdn_kernel(qkv_ref, gate_ref, ba_ref, cst_ref, s0_ref, cw_ref, dtb_ref, alog_ref, nrm_ref,
                o_ref, sout_ref, cout_ref, state_sc, cbuf, *, T, n_valid, hi):
    c = pl.program_id(1)
    nc = pl.num_programs(1)
    halo = GDN_CONV - 1
    base = SUBLANES - halo
    H = GDN_HEADS
    DK = GDN_DK

    @pl.when(c == 0)
    def _():
        state_sc[...] = s0_ref[0]
        cbuf[base:SUBLANES, :] = cst_ref[0]

    cbuf[SUBLANES:SUBLANES + T, :] = qkv_ref[0]
    conv = cw_ref[0:1, :] * cbuf[base:base + T, :]
    for k in range(1, GDN_CONV):
        conv = conv + cw_ref[k:k + 1, :] * cbuf[base + k:base + k + T, :]
    qkv = _silu(conv)

    c_last = (n_valid - 1) // T
    nv_last = n_valid - c_last * T

    @pl.when(c == c_last)
    def _():
        cout_ref[0] = cbuf[base + nv_last:base + nv_last + halo, :]

    cbuf[base:SUBLANES, :] = cbuf[base + T:SUBLANES + T, :]

    tok = lax.broadcasted_iota(I32, (T, 1), 0) + c * T
    valid = tok < n_valid
    ba = ba_ref[0]
    beta = jnp.where(valid, _sigmoid(ba), 0.0)
    g = jnp.where(valid, -jnp.exp(alog_ref[...]) * _softplus(ba + dtb_ref[...]), 0.0)
    ri = lax.broadcasted_iota(I32, (T, T), 0)
    ci = lax.broadcasted_iota(I32, (T, T), 1)
    tril = jnp.where(ci <= ri, 1.0, 0.0).astype(F32)
    gc = _dot_hi(tril, g)

    def l2n(x):
        return x * lax.rsqrt(jnp.sum(x * x, axis=-1, keepdims=True) + EPS)

    qn = [l2n(qkv[:, h * DK:(h + 1) * DK]) * (DK ** -0.5) for h in range(H)]
    kn = [l2n(qkv[:, (H + h) * DK:(H + h + 1) * DK]) for h in range(H)]
    vv = [qkv[:, (2 * H + h) * DK:(2 * H + h + 1) * DK] for h in range(H)]

    G = 4
    W = G * T
    bi = lax.broadcasted_iota(I32, (W, W), 0)
    bj = lax.broadcasted_iota(I32, (W, W), 1)
    t_shift = T.bit_length() - 1
    same = jnp.right_shift(bi, t_shift) == jnp.right_shift(bj, t_shift)
    strict = jnp.logical_and(same, bj < bi)
    incl = jnp.logical_and(same, bj <= bi)
    eye_w = jnp.where(bi == bj, 1.0, 0.0).astype(F32)
    rowhead = jnp.right_shift(lax.broadcasted_iota(I32, (W, DK), 0), t_shift)

    o_heads = [None] * H
    for grp in range(H // G):
        hs = [grp * G + j for j in range(G)]
        kst = jnp.concatenate([kn[h] for h in hs], axis=0)
        qst = jnp.concatenate([qn[h] for h in hs], axis=0)
        vst = jnp.concatenate([vv[h] for h in hs], axis=0)
        cb = jnp.concatenate([jnp.broadcast_to(gc[:, H + h:H + h + 1], (T, W)) for h in hs], axis=0)
        bb = jnp.concatenate([jnp.broadcast_to(beta[:, h:h + 1], (T, DK)) for h in hs], axis=0)
        glast = jnp.concatenate([jnp.broadcast_to(gc[T - 1:T, H + h:H + h + 1], (T, DK)) for h in hs], axis=0)
        dec = jnp.exp(jnp.minimum(cb - jnp.transpose(cb), 0.0))
        kk = _dot_nt(kst, kst, hi)
        qk = _dot_nt(qst, kst, hi)
        bbw = jnp.concatenate([bb, bb], axis=-1)
        a_mat = jnp.where(strict, bbw * kk * dec, 0.0)
        qkm = jnp.where(incl, qk * dec, 0.0)
        p_mat = eye_w - a_mat
        a_pow = a_mat
        n_sq = max(1, int(math.ceil(math.log2(T))) - 1)
        for it in range(n_sq):
            a_pow = _dot_hi(a_pow, a_pow)
            p_mat = p_mat + _dot_hi(p_mat, a_pow)
        cbk = cb[:, :DK]
        eg = jnp.exp(cbk)
        rhs = jnp.concatenate([vst * bb, kst * bb * eg], axis=-1)
        sol = _dot_hi(p_mat, rhs)
        u_st = sol[:, :DK]
        w_st = sol[:, DK:]
        qdec = qst * eg
        kdec_t = jnp.transpose(kst * jnp.exp(glast - cbk))
        vnew = []
        ooff = []
        for j, h in enumerate(hs):
            s_h = state_sc[h * DK:(h + 1) * DK, :]
            lhs = jnp.concatenate([w_st[j * T:(j + 1) * T], qdec[j * T:(j + 1) * T]], axis=0)
            ws = _dot(lhs, s_h, hi)
            vnew.append(u_st[j * T:(j + 1) * T] - ws[:T])
            ooff.append(ws[T:])
        vnew_st = jnp.concatenate(vnew, axis=0)
        o_st = jnp.concatenate(ooff, axis=0) + _dot(qkm, vnew_st, hi)
        for j, h in enumerate(hs):
            s_h = state_sc[h * DK:(h + 1) * DK, :]
            vm = jnp.where(rowhead == j, vnew_st, 0.0)
            last = jnp.exp(jnp.broadcast_to(gc[T - 1:T, H + h:H + h + 1], (DK, GDN_DV)))
            state_sc[h * DK:(h + 1) * DK, :] = s_h * last + _dot(kdec_t, vm, hi)
            o_heads[h] = o_st[j * T:(j + 1) * T]
    gate = gate_ref[0]
    outs = []
    for h in range(H):
        oh = o_heads[h]
        oh = oh * lax.rsqrt(jnp.mean(oh * oh, axis=-1, keepdims=True) + EPS) * nrm_ref[...]
        outs.append(oh * _silu(gate[:, h * GDN_DV:(h + 1) * GDN_DV]))
    o_ref[0] = jnp.concatenate(outs, axis=-1).astype(o_ref.dtype)

    @pl.when(c == nc - 1)
    def _():
        sout_ref[0] = state_sc[...]


def _gdn(qkv, gate, ba, cst, s0, cw, dtb, alog, nrm, T, n_valid, hi):
    B, Lp, _ = qkv.shape
    hk = GDN_HEADS * GDN_DK
    row = lambda b, c: (0, 0)
    return pl.pallas_call(
        functools.partial(_gdn_kernel, T=T, n_valid=n_valid, hi=hi),
        grid=(B, Lp // T),
        in_specs=[pl.BlockSpec((1, T, GDN_QKV), lambda b, c: (b, c, 0)),
                  pl.BlockSpec((1, T, GDN_HEADS * GDN_DV), lambda b, c: (b, c, 0)),
                  pl.BlockSpec((1, T, LANES), lambda b, c: (b, c, 0)),
                  pl.BlockSpec((1, GDN_CONV - 1, GDN_QKV), lambda b, c: (b, 0, 0)),
                  pl.BlockSpec((1, hk, GDN_DV), lambda b, c: (b, 0, 0)),
                  pl.BlockSpec((GDN_CONV, GDN_QKV), row),
                  pl.BlockSpec((1, LANES), row),
                  pl.BlockSpec((1, LANES), row),
                  pl.BlockSpec((1, GDN_DV), row)],
        out_specs=[pl.BlockSpec((1, T, GDN_HEADS * GDN_DV), lambda b, c: (b, c, 0)),
                   pl.BlockSpec((1, hk, GDN_DV), lambda b, c: (b, 0, 0)),
                   pl.BlockSpec((1, GDN_CONV - 1, GDN_QKV), lambda b, c: (b, 0, 0))],
        out_shape=[jax.ShapeDtypeStruct((B, Lp, GDN_HEADS * GDN_DV), _act_dtype(hi)),
                   jax.ShapeDtypeStruct((B, hk, GDN_DV), F32),
                   jax.ShapeDtypeStruct((B, GDN_CONV - 1, GDN_QKV), F32)],
        scratch_shapes=[pltpu.VMEM((hk, GDN_DV), F32),
                        pltpu.VMEM((T + SUBLANES, GDN_QKV), F32)],
        compiler_params=_cparams(("parallel", "arbitrary")),
        name="gdn_scan",
    )(qkv, gate, ba, cst, s0, cw, dtb, alog, nrm)


def _sconv_kernel(b_ref, c_ref, v_ref, cst_ref, w_ref, o_ref, cout_ref, cbuf, *, T):
    i = pl.program_id(1)
    halo = SC_CONV - 1
    base = SUBLANES - halo

    @pl.when(i == 0)
    def _():
        cbuf[base:SUBLANES, :] = cst_ref[0]

    cbuf[SUBLANES:SUBLANES + T, :] = c_ref[0] * v_ref[0]
    conv = w_ref[0:1, :] * cbuf[base:base + T, :]
    for k in range(1, SC_CONV):
        conv = conv + w_ref[k:k + 1, :] * cbuf[base + k:base + k + T, :]
    o_ref[0] = (b_ref[0] * conv).astype(o_ref.dtype)
    cout_ref[0] = cbuf[base + T:SUBLANES + T, :]
    cbuf[base:SUBLANES, :] = cbuf[base + T:SUBLANES + T, :]


def _sconv(scb, scc, scv, cst, w, T, hi):
    B, L, _ = scb.shape
    T = min(T, L)
    blk = pl.BlockSpec((1, T, SC_WIDTH), lambda b, i: (b, i, 0))
    st = pl.BlockSpec((1, SC_CONV - 1, SC_WIDTH), lambda b, i: (b, 0, 0))
    return pl.pallas_call(
        functools.partial(_sconv_kernel, T=T),
        grid=(B, L // T),
        in_specs=[blk, blk, blk, st, pl.BlockSpec((SC_CONV, SC_WIDTH), lambda b, i: (0, 0))],
        out_specs=[blk, st],
        out_shape=[jax.ShapeDtypeStruct((B, L, SC_WIDTH), _act_dtype(hi)),
                   jax.ShapeDtypeStruct((B, SC_CONV - 1, SC_WIDTH), F32)],
        scratch_shapes=[pltpu.VMEM((T + SUBLANES, SC_WIDTH), F32)],
        compiler_params=_cparams(("parallel", "arbitrary")),
        name="short_conv",
    )(scb, scc, scv, cst, w)


def _out_kernel(y_ref, o_ref, x_ref, mod_ref, wy_ref, wo_ref, g_ref, wr_ref, br_ref,
                xn_ref, h_ref, lg_ref, *, hi):
    mix = _dot(y_ref[0], wy_ref[...], hi) + _dot(o_ref[0], wo_ref[...], hi)
    xn = x_ref[0] + mod_ref[0, 2:3, :] * mix
    xn_ref[0] = xn
    h = xn * lax.rsqrt(jnp.mean(xn * xn, axis=-1, keepdims=True) + EPS) * g_ref[...]
    h = h * (1.0 + mod_ref[0, 4:5, :]) + mod_ref[0, 3:4, :]
    h_ref[0] = h
    lg_ref[0] = _dot_hi(h, wr_ref[...]) + br_ref[...]


def _out_proj(y, o, x, mod, wy, wo, g, wr, br, tm, hi):
    B, L, _ = x.shape
    tm = min(tm, L)
    blk = lambda dt_w: pl.BlockSpec((1, tm, dt_w), lambda b, i: (b, i, 0))
    full = lambda s: pl.BlockSpec(s, lambda b, i: (0, 0))
    return pl.pallas_call(
        functools.partial(_out_kernel, hi=hi),
        grid=(B, L // tm),
        in_specs=[blk(D_MODEL), blk(D_MODEL), blk(D_MODEL),
                  pl.BlockSpec((1, 6, D_MODEL), lambda b, i: (b, 0, 0)),
                  full((D_MODEL, D_MODEL)), full((D_MODEL, D_MODEL)), full((1, D_MODEL)),
                  full((D_MODEL, LANES)), full((1, LANES))],
        out_specs=[blk(D_MODEL), blk(D_MODEL), blk(LANES)],
        out_shape=[jax.ShapeDtypeStruct((B, L, D_MODEL), F32),
                   jax.ShapeDtypeStruct((B, L, D_MODEL), F32),
                   jax.ShapeDtypeStruct((B, L, LANES), F32)],
        compiler_params=_cparams(("parallel", "parallel")),
        name="out_proj",
    )(y, o, x, mod, wy, wo, g, wr, br)


def _route_kernel(lg_ref, e_ref, rank_ref, gate_ref, cnt_ref, base_sc, *, tm):
    i = pl.program_id(0)

    @pl.when(i == 0)
    def _():
        base_sc[...] = jnp.zeros_like(base_sc)

    lg = lg_ref[...]
    lane_i = lax.broadcasted_iota(I32, (tm, LANES), 1)
    lane = lane_i.astype(F32)
    vals, idxs = [], []
    cur = lg
    for _ in range(TOP_K):
        m = jnp.max(cur, axis=-1, keepdims=True)
        idx = jnp.min(jnp.where(cur == m, lane, float(LANES)), axis=-1, keepdims=True)
        vals.append(m)
        idxs.append(idx)
        cur = jnp.where(lane == idx, -jnp.inf, cur)
    ex = [jnp.exp(v - vals[0]) for v in vals]
    den = ex[0] + ex[1] + ex[2] + ex[3]
    onehot = jnp.zeros((tm, LANES), F32)
    for idx in idxs:
        onehot = onehot + jnp.where(lane == idx, 1.0, 0.0)
    ri = lax.broadcasted_iota(I32, (tm, tm), 0)
    ci = lax.broadcasted_iota(I32, (tm, tm), 1)
    before = _dot(jnp.where(ci < ri, 1.0, 0.0), onehot) + base_sc[...]
    e_out = jnp.zeros((tm, LANES), I32)
    r_out = jnp.zeros((tm, LANES), I32)
    g_out = jnp.zeros((tm, LANES), F32)
    for k in range(TOP_K):
        rk = jnp.sum(jnp.where(lane == idxs[k], before, 0.0), axis=-1, keepdims=True)
        e_out = jnp.where(lane_i == k, idxs[k].astype(I32), e_out)
        r_out = jnp.where(lane_i == k, rk.astype(I32), r_out)
        g_out = jnp.where(lane_i == k, ex[k] / den, g_out)
    e_ref[...] = e_out
    rank_ref[...] = r_out
    gate_ref[...] = g_out
    base_sc[...] = base_sc[...] + jnp.sum(onehot, axis=0, keepdims=True)
    cnt_ref[...] = base_sc[...].astype(I32)


def _route(logits, tm):
    n_tok = logits.shape[0]
    tm = min(tm, n_tok)
    blk = pl.BlockSpec((tm, LANES), lambda i: (i, 0))
    return pl.pallas_call(
        functools.partial(_route_kernel, tm=tm),
        grid=(n_tok // tm,),
        in_specs=[blk],
        out_specs=[blk, blk, blk, pl.BlockSpec((1, LANES), lambda i: (0, 0))],
        out_shape=[jax.ShapeDtypeStruct((n_tok, LANES), I32),
                   jax.ShapeDtypeStruct((n_tok, LANES), I32),
                   jax.ShapeDtypeStruct((n_tok, LANES), F32),
                   jax.ShapeDtypeStruct((1, LANES), I32)],
        scratch_shapes=[pltpu.VMEM((1, LANES), F32)],
        compiler_params=_cparams(("arbitrary",)),
        name="moe_route",
    )(logits)


def _dispatch_kernel(dest_ref, h_ref, xin_hbm, out_hbm, sem, *, tm):
    del xin_hbm

    def issue(r, carry):
        for k in range(TOP_K):
            d = dest_ref[r * TOP_K + k]
            pltpu.make_async_copy(h_ref.at[pl.ds(r, 1), :], out_hbm.at[pl.ds(d, 1), :], sem).start()
        return carry

    lax.fori_loop(0, tm, issue, 0)
    for _ in range(TOP_K):
        pltpu.make_async_copy(h_ref, out_hbm.at[pl.ds(0, tm), :], sem).wait()


def _dispatch(h, dest_flat, n_rows, tm):
    n_tok = h.shape[0]
    tm = min(tm, n_tok)
    xin0 = jnp.zeros((n_rows, D_MODEL), F32)
    return pl.pallas_call(
        functools.partial(_dispatch_kernel, tm=tm),
        grid=(n_tok // tm,),
        in_specs=[pl.BlockSpec((tm * TOP_K,), lambda i: (i,), memory_space=pltpu.SMEM),
                  pl.BlockSpec((tm, D_MODEL), lambda i: (i, 0)),
                  pl.BlockSpec(memory_space=pl.ANY)],
        out_specs=pl.BlockSpec(memory_space=pl.ANY),
        out_shape=jax.ShapeDtypeStruct((n_rows, D_MODEL), F32),
        scratch_shapes=[pltpu.SemaphoreType.DMA(())],
        input_output_aliases={2: 0},
        compiler_params=_cparams(("arbitrary",)),
        name="moe_dispatch",
    )(dest_flat, h, xin0)


def _ffn_kernel(blk_e_ref, nused_ref, x_ref, wgu_ref, bgu_ref, wd_ref, bd_ref, o_ref, wgu_sc, wd_sc):
    i = pl.program_id(0)
    prev = blk_e_ref[jnp.maximum(i - 1, 0)]
    fresh = jnp.logical_or(i == 0, blk_e_ref[i] != prev)
    active = i < nused_ref[0]

    @pl.when(jnp.logical_and(active, fresh))
    def _():
        wgu_sc[...] = wgu_ref[0].astype(BF16)
        wd_sc[...] = wd_ref[0].astype(BF16)

    @pl.when(active)
    def _():
        gu = jnp.dot(x_ref[...].astype(BF16), wgu_sc[...], preferred_element_type=F32) + bgu_ref[0]
        gate = jnp.minimum(gu[:, :D_FF], SWIGLU_LIMIT)
        up = jnp.clip(gu[:, D_FF:], -SWIGLU_LIMIT, SWIGLU_LIMIT)
        act = (up + 1.0) * gate * _sigmoid(SWIGLU_ALPHA * gate)
        o_ref[...] = jnp.dot(act.astype(BF16), wd_sc[...], preferred_element_type=F32) + bd_ref[0]

    @pl.when(jnp.logical_not(active))
    def _():
        o_ref[...] = jnp.zeros_like(o_ref)


def _expert_ffn(xin, blk_e, n_used, w_gu, b_gu, w_down, b_down):
    n_rows = xin.shape[0]
    n_blocks = n_rows // MOE_BM

    def row_map(i, be, nu):
        return (jnp.minimum(i, nu[0] - 1), 0)

    def e_map3(i, be, nu):
        return (be[jnp.minimum(i, nu[0] - 1)], 0, 0)

    grid_spec = pltpu.PrefetchScalarGridSpec(
        num_scalar_prefetch=2,
        grid=(n_blocks,),
        in_specs=[pl.BlockSpec((MOE_BM, D_MODEL), row_map),
                  pl.BlockSpec((1, D_MODEL, 2 * D_FF), e_map3),
                  pl.BlockSpec((1, 1, 2 * D_FF), e_map3),
                  pl.BlockSpec((1, D_FF, D_MODEL), e_map3),
                  pl.BlockSpec((1, 1, D_MODEL), e_map3)],
        out_specs=pl.BlockSpec((MOE_BM, D_MODEL), lambda i, be, nu: (i, 0)),
        scratch_shapes=[pltpu.VMEM((D_MODEL, 2 * D_FF), BF16),
                        pltpu.VMEM((D_FF, D_MODEL), BF16)],
    )
    return pl.pallas_call(
        _ffn_kernel,
        grid_spec=grid_spec,
        out_shape=jax.ShapeDtypeStruct((n_rows, D_MODEL), F32),
        compiler_params=_cparams(("arbitrary",)),
        name="moe_expert_ffn",
    )(blk_e, n_used, xin, w_gu, b_gu.reshape(N_EXPERTS, 1, 2 * D_FF), w_down,
      b_down.reshape(N_EXPERTS, 1, D_MODEL))


def _combine_kernel(dest_ref, f_hbm, gate_ref, x_ref, mod_ref, g_ref, o_ref, buf, sem, *, tm, final):
    def issue(r, carry):
        for k in range(TOP_K):
            d = dest_ref[r * TOP_K + k]
            pltpu.make_async_copy(f_hbm.at[pl.ds(d, 1), :], buf.at[k, pl.ds(r, 1), :], sem).start()
        return carry

    lax.fori_loop(0, tm, issue, 0)
    for k in range(TOP_K):
        pltpu.make_async_copy(f_hbm.at[pl.ds(0, tm), :], buf.at[k], sem).wait()
    gates = gate_ref[...]
    moe = gates[:, 0:1] * buf[0]
    for k in range(1, TOP_K):
        moe = moe + gates[:, k:k + 1] * buf[k]
    xo = x_ref[...] + mod_ref[0, 5:6, :] * moe
    if final:
        xo = xo * lax.rsqrt(jnp.mean(xo * xo, axis=-1, keepdims=True) + EPS) * g_ref[...]
    o_ref[...] = xo


def _combine(ffn_out, dest_flat, gates, x, mod, g_final, tiles_per_batch, tm, final):
    n_tok = x.shape[0]
    return pl.pallas_call(
        functools.partial(_combine_kernel, tm=tm, final=final),
        grid=(n_tok // tm,),
        in_specs=[pl.BlockSpec((tm * TOP_K,), lambda i: (i,), memory_space=pltpu.SMEM),
                  pl.BlockSpec(memory_space=pl.ANY),
                  pl.BlockSpec((tm, LANES), lambda i: (i, 0)),
                  pl.BlockSpec((tm, D_MODEL), lambda i: (i, 0)),
                  pl.BlockSpec((1, 6, D_MODEL), lambda i: (i // tiles_per_batch, 0, 0)),
                  pl.BlockSpec((1, D_MODEL), lambda i: (0, 0))],
        out_specs=pl.BlockSpec((tm, D_MODEL), lambda i: (i, 0)),
        out_shape=jax.ShapeDtypeStruct((n_tok, D_MODEL), F32),
        scratch_shapes=[pltpu.VMEM((TOP_K, tm, D_MODEL), F32), pltpu.SemaphoreType.DMA(())],
        compiler_params=_cparams(("arbitrary",)),
        name="moe_combine",
    )(dest_flat, ffn_out, gates, x, mod, g_final)


def _pad_cols(w, width):
    return jnp.pad(w, ((0, 0), (0, width - w.shape[1])))


def _pad_lanes(v, offset=0):
    return jnp.pad(v.astype(F32), (offset, LANES - offset - v.shape[0])).reshape(1, LANES)


def _rope_tables(pos):
    half = MLA_ROPE // 2
    inv = ROPE_THETA ** (-jnp.arange(half, dtype=F32) / half)
    ang = pos.astype(F32)[:, None] * inv[None, :]
    cos, sin = jnp.cos(ang), jnp.sin(ang)
    z = jnp.zeros_like(cos)
    pad = jnp.zeros((pos.shape[0], LANES - MLA_ROPE), F32)
    cos_t = jnp.concatenate([cos, cos, pad], axis=1)
    sin_a = jnp.concatenate([-sin, z, pad], axis=1)
    sin_b = jnp.concatenate([z, sin, pad], axis=1)
    return cos_t, sin_a, sin_b


def _pad_seq(t, lp):
    return jnp.pad(t, ((0, 0), (0, lp - t.shape[1]), (0, 0)))


def _even_weights(W, j, dt):
    w_in = W['ev_w_in'][j]
    o1 = SSD_D_INNER
    o2 = o1 + SSD_XBC
    o3 = o2 + SSD_HEADS
    o4 = o3 + MLA_Q_RANK
    w_ssd = jnp.concatenate([w_in[:, :o2], _pad_cols(w_in[:, o2:o3], LANES)], axis=1).astype(dt)
    w_mla = jnp.concatenate([w_in[:, o3:o4], _pad_cols(w_in[:, o4:], MLA_Q_RANK)], axis=1).astype(dt)
    wq = W['mla_w_q_up'][j].reshape(MLA_Q_RANK, MLA_HEADS, MLA_NOPE + MLA_ROPE)
    wq = jnp.concatenate([wq[..., :MLA_NOPE], jnp.zeros((MLA_Q_RANK, MLA_HEADS, LANES - MLA_NOPE), F32),
                          wq[..., MLA_NOPE:], jnp.zeros((MLA_Q_RANK, MLA_HEADS, LANES - MLA_ROPE), F32)], axis=-1)
    wq = jnp.transpose(wq, (1, 0, 2)).astype(dt)
    wkv = jnp.transpose(W['mla_w_kv_up'][j].reshape(MLA_KV_RANK, MLA_HEADS, MLA_NOPE + MLA_V),
                        (1, 0, 2)).astype(dt)
    expand = (jnp.arange(LANES)[:, None] == (jnp.arange(SSD_D_INNER) // SSD_HEADDIM)[None, :]).astype(F32)
    return dict(
        w_ssd=w_ssd, w_mla=w_mla, wq=wq, wkv=wkv, expand=expand,
        wy=W['ev_w_out'][j][:SSD_D_INNER].astype(dt), wo=W['ev_w_out'][j][SSD_D_INNER:].astype(dt),
        cw=W['ssd_conv_w'][j], cb=W['ssd_conv_b'][j].reshape(1, SSD_XBC),
        dtb=_pad_lanes(W['ssd_dt_bias'][j]), alog=_pad_lanes(W['ssd_a_log'][j]),
        dsk=jnp.repeat(W['ssd_d'][j].astype(F32), SSD_HEADDIM).reshape(1, SSD_D_INNER),
        nrm=W['ssd_norm'][j].reshape(1, SSD_D_INNER),
        qn=W['mla_q_norm'][j].reshape(1, MLA_Q_RANK), kvn=W['mla_kv_norm'][j].reshape(1, MLA_KV_RANK))


def _odd_weights(W, j, dt):
    w_in = W['od_w_in'][j]
    o1 = GDN_QKV
    o2 = o1 + GDN_HEADS * GDN_DV
    o3 = o2 + 2 * GDN_HEADS
    w_gdn = jnp.concatenate([w_in[:, :o2], _pad_cols(w_in[:, o2:o3], LANES)], axis=1).astype(dt)
    w_sc = w_in[:, o3:].astype(dt)
    return dict(
        w_gdn=w_gdn, w_sc=w_sc,
        wy=W['od_w_out'][j][:GDN_HEADS * GDN_DV].astype(dt), wo=W['od_w_out'][j][GDN_HEADS * GDN_DV:].astype(dt),
        cw=W['gdn_conv_w'][j], dtb=_pad_lanes(W['gdn_dt_bias'][j], GDN_HEADS),
        alog=_pad_lanes(W['gdn_a_log'][j], GDN_HEADS), nrm=W['gdn_norm'][j].reshape(1, GDN_DV),
        scw=W['sconv_w'][j])


def _moe(h, logits, x_new, mod, W, i, g_final, L, final):
    n_tok = h.shape[0]
    e_pad, rank_pad, gates, cnt = _route(logits, 256)
    counts = cnt[0, :N_EXPERTS]
    padded = (counts + MOE_BM - 1) // MOE_BM * MOE_BM
    pend = jnp.cumsum(padded)
    pstart = pend - padded
    e_sel = e_pad[:, :TOP_K]
    dest = (pstart[e_sel] + rank_pad[:, :TOP_K]).astype(I32).reshape(-1)
    n_blocks = n_tok * TOP_K // MOE_BM + N_EXPERTS
    n_rows = n_blocks * MOE_BM
    blk_start = jnp.arange(n_blocks, dtype=pend.dtype) * MOE_BM
    blk_e = jnp.minimum(jnp.sum((blk_start[:, None] >= pend[None, :]).astype(I32), axis=1), N_EXPERTS - 1)
    n_used = (pend[-1:] // MOE_BM).astype(I32)
    tm = min(256, L)
    xin = _dispatch(h, dest, n_rows, tm)
    f_out = _expert_ffn(xin, blk_e, n_used, W['w_gu'][i], W['b_gu'][i], W['w_down'][i], W['b_down'][i])
    return _combine(f_out, dest, gates, x_new, mod, g_final, L // tm, tm, final)


def _trunk(x, c_mod, pos0, caches, W, PW, seq_t, hi):
    B, L, _ = x.shape
    new = {}
    pos = pos0 + jnp.arange(L, dtype=I32)
    cos_t, sin_a, sin_b = _rope_tables(pos)
    wr_all = W['w_router']
    for i in range(DEPTH):
        mod = c_mod[i]
        g_mix = W['norm_mix'][i].reshape(1, D_MODEL)
        j = i // 2
        if i % 2 == 0:
            P = PW[i]
            z, xbc, dtr = _in_proj(x, mod, g_mix, P['w_ssd'],
                                   ((0, SSD_D_INNER), (SSD_D_INNER, SSD_XBC), (SSD_D_INNER + SSD_XBC, LANES)),
                                   (F32, F32, F32), 0, 1, 512, hi)
            latq, latkv = _in_proj(x, mod, g_mix, P['w_mla'], ((0, MLA_Q_RANK), (MLA_Q_RANK, MLA_Q_RANK)),
                                   (F32, F32), 0, 1, 512, hi)
            T = seq_t['ssd']
            lp = -(-L // T) * T
            y, s_new, cst_new = _ssd(_pad_seq(z, lp), _pad_seq(xbc, lp), _pad_seq(dtr, lp),
                                     caches['ssd_conv'][j], caches['ssd'][j].reshape(B, -1, SSD_STATE),
                                     P['cw'], P['cb'], P['dtb'], P['alog'], P['dsk'], P['nrm'], P['expand'], T, L, hi)
            y = y[:, :L]
            new['ssd'] = s_new.reshape(1, B, SSD_HEADS, SSD_HEADDIM, SSD_STATE)
            new['ssd_conv'] = cst_new[None]
            ckv_new, kpe_new = _latkv_post(latkv, P['kvn'], cos_t, sin_a, sin_b, 512)
            new['mla_ckv'] = ckv_new[None]
            new['mla_krope'] = kpe_new[None, :, :, :MLA_ROPE]
            ckv_past, kpe_past = caches['mla_ckv'][j], caches['mla_krope'][j]
            past = ckv_past.shape[1]
            kv_len = past + L
            tk = seq_t['tk']
            lk = -(-kv_len // tk) * tk
            ckv_all = _pad_seq(jnp.concatenate([ckv_past, ckv_new], axis=1), lk)
            kpe_all = _pad_seq(jnp.concatenate(
                [jnp.pad(kpe_past, ((0, 0), (0, 0), (0, LANES - MLA_ROPE))), kpe_new], axis=1), lk)
            kv = _kv_up(ckv_all, kpe_all, P['wkv'], tk, hi)
            q = _q_proj(latq, P['qn'], P['wq'], cos_t, sin_a, sin_b, 512, hi)
            o = _attention(q, kv, min(seq_t['tq'], L), tk, pos0, kv_len, hi)
        else:
            P = PW[i]
            hv = GDN_HEADS * GDN_DV
            qkv, gate, ba = _in_proj(x, mod, g_mix, P['w_gdn'],
                                     ((0, GDN_QKV), (GDN_QKV, hv), (GDN_QKV + hv, LANES)),
                                     (F32, F32, F32), 0, 1, 512, hi)
            scb, scc, scv = _in_proj(x, mod, g_mix, P['w_sc'],
                                     ((0, SC_WIDTH), (SC_WIDTH, SC_WIDTH), (2 * SC_WIDTH, SC_WIDTH)),
                                     (F32, F32, F32), 0, 1, 512, hi)
            T = CHUNK
            lp = -(-L // T) * T
            y, s_new, cst_new = _gdn(_pad_seq(qkv, lp), _pad_seq(gate, lp), _pad_seq(ba, lp),
                                     caches['gdn_conv'][j], caches['gdn'][j].reshape(B, -1, GDN_DV),
                                     P['cw'], P['dtb'], P['alog'], P['nrm'], T, L, hi)
            y = y[:, :L]
            new['gdn'] = s_new.reshape(1, B, GDN_HEADS, GDN_DK, GDN_DV)
            new['gdn_conv'] = cst_new[None]
            o, sc_new = _sconv(scb, scc, scv, caches['sconv'][j], P['scw'], 512, hi)
            new['sconv'] = sc_new[None]
        wr = _pad_cols(wr_all[i], LANES)
        br = jnp.concatenate([W['b_router'][i].astype(F32), jnp.full((LANES - N_EXPERTS,), NEG_BIG, F32)]).reshape(1, LANES)
        x_new, h, logits = _out_proj(y, o, x, mod, P['wy'], P['wo'], W['norm_ffn'][i].reshape(1, D_MODEL),
                                     wr, br, 512, hi)
        final = i == DEPTH - 1
        xo = _moe(h.reshape(B * L, D_MODEL), logits.reshape(B * L, LANES), x_new.reshape(B * L, D_MODEL),
                  mod, W, i, W['norm_final'].reshape(1, D_MODEL), L, final)
        x = xo.reshape(B, L, D_MODEL)
    return x, new


def _prep_weights(W, dt):
    PW = {}
    for i in range(DEPTH):
        PW[i] = _even_weights(W, i // 2, dt) if i % 2 == 0 else _odd_weights(W, i // 2, dt)
    return PW


def kernel(x_prompt, x_sample, c_prompt, c_sample, cache_mla_ckv, cache_mla_krope, state_ssd, state_ssd_conv, state_gdn, state_gdn_conv, state_sconv, norm_mix, norm_ffn, w_ada, b_ada, w_router, b_router, w_gu, b_gu, w_down, b_down, norm_final, ev_w_in, ev_w_out, ssd_conv_w, ssd_conv_b, ssd_dt_bias, ssd_a_log, ssd_d, ssd_norm, mla_q_norm, mla_w_q_up, mla_kv_norm, mla_w_kv_up, od_w_in, od_w_out, gdn_conv_w, gdn_dt_bias, gdn_a_log, gdn_norm, sconv_w):
    W = dict(norm_mix=norm_mix, norm_ffn=norm_ffn, w_ada=w_ada, b_ada=b_ada, w_router=w_router,
             b_router=b_router, w_gu=w_gu, b_gu=b_gu, w_down=w_down, b_down=b_down, norm_final=norm_final,
             ev_w_in=ev_w_in, ev_w_out=ev_w_out, ssd_conv_w=ssd_conv_w, ssd_conv_b=ssd_conv_b,
             ssd_dt_bias=ssd_dt_bias, ssd_a_log=ssd_a_log, ssd_d=ssd_d, ssd_norm=ssd_norm,
             mla_q_norm=mla_q_norm, mla_w_q_up=mla_w_q_up, mla_kv_norm=mla_kv_norm, mla_w_kv_up=mla_w_kv_up,
             od_w_in=od_w_in, od_w_out=od_w_out, gdn_conv_w=gdn_conv_w, gdn_dt_bias=gdn_dt_bias,
             gdn_a_log=gdn_a_log, gdn_norm=gdn_norm, sconv_w=sconv_w)
    bp, bs = x_prompt.shape[0], x_sample.shape[0]
    nb = 16
    c_all = jnp.concatenate([c_prompt, c_sample, jnp.zeros((nb - bp - bs, D_MODEL), F32)], axis=0)
    mod_all = _ada_mod(c_all, w_ada, b_ada).reshape(DEPTH, nb, 6, D_MODEL)
    n_even, n_odd = (DEPTH + 1) // 2, DEPTH // 2
    zero_caches = dict(
        mla_ckv=jnp.zeros((n_even, bp, 0, MLA_KV_RANK), F32), mla_krope=jnp.zeros((n_even, bp, 0, MLA_ROPE), F32),
        ssd=jnp.zeros((n_even, bp, SSD_HEADS, SSD_HEADDIM, SSD_STATE), F32),
        ssd_conv=jnp.zeros((n_even, bp, SSD_CONV - 1, SSD_XBC), F32),
        gdn=jnp.zeros((n_odd, bp, GDN_HEADS, GDN_DK, GDN_DV), F32),
        gdn_conv=jnp.zeros((n_odd, bp, GDN_CONV - 1, GDN_QKV), F32),
        sconv=jnp.zeros((n_odd, bp, SC_CONV - 1, SC_WIDTH), F32))
    y_p, sp = _trunk(x_prompt, mod_all[:, :bp], 0, zero_caches, W, _prep_weights(W, BF16),
                     dict(ssd=256, tq=256, tk=512), False)
    past = cache_mla_ckv.shape[2]
    caches = dict(mla_ckv=cache_mla_ckv, mla_krope=cache_mla_krope, ssd=state_ssd, ssd_conv=state_ssd_conv,
                  gdn=state_gdn, gdn_conv=state_gdn_conv, sconv=state_sconv)
    y_s, ss = _trunk(x_sample, mod_all[:, bp:bp + bs], past, caches, W, _prep_weights(W, F32),
                     dict(ssd=128, tq=32, tk=256), True)
    return (y_p, y_s,
            sp['mla_ckv'], ss['mla_ckv'], sp['mla_krope'], ss['mla_krope'],
            sp['ssd'], ss['ssd'], sp['ssd_conv'], ss['ssd_conv'],
            sp['gdn'], ss['gdn'], sp['gdn_conv'], ss['gdn_conv'],
            sp['sconv'], ss['sconv'])
```

```python
import functools
import math

import jax
import jax.numpy as jnp
from jax import lax
from jax.experimental import pallas as pl
from jax.experimental.pallas import tpu as pltpu

F32 = jnp.float32
BF16 = jnp.bfloat16
I32 = jnp.int32
HI = lax.Precision.HIGHEST

D_MODEL = 1024
DEPTH = 2
CHUNK = 64
CHUNK_SHIFT = 6
EPS = 1e-6
SSD_D_INNER = D_MODEL
SSD_HEADDIM = 64
SSD_HEADS = SSD_D_INNER // SSD_HEADDIM
SSD_GROUPS = 4
SSD_STATE = 128
SSD_CONV = 4
SSD_XBC = SSD_D_INNER + 2 * SSD_GROUPS * SSD_STATE
MLA_HEADS = 16
MLA_NOPE = 64
MLA_ROPE = 32
MLA_V = 64
MLA_Q_RANK = 384
MLA_KV_RANK = 256
ROPE_THETA = 10000.0
GDN_HEADS = 8
GDN_DK = 128
GDN_DV = 128
GDN_CONV = 4
GDN_QKV = GDN_HEADS * (2 * GDN_DK + GDN_DV)
SC_WIDTH = D_MODEL
SC_CONV = 3
N_EXPERTS = 32
TOP_K = 4
D_FF = D_MODEL
SWIGLU_LIMIT = 7.0
SWIGLU_ALPHA = 1.702

LANES = 128
SUBLANES = 8
VMEM_LIMIT = 56 * 1024 * 1024

NEG_BIG = -1e30
MOE_BM = 256


def _cparams(sem):
    return pltpu.CompilerParams(dimension_semantics=sem, vmem_limit_bytes=VMEM_LIMIT)


def _sigmoid(x):
    return 1.0 / (1.0 + jnp.exp(-x))


def _silu(x):
    return x * _sigmoid(x)


def _softplus(x):
    return jnp.maximum(x, 0.0) + jnp.log1p(jnp.exp(-jnp.abs(x)))


_NN = (((1,), (0,)), ((), ()))
_NT = (((1,), (1,)), ((), ()))


def _dot_split(a, b, dims=_NN):
    a_h = a.astype(BF16)
    b_h = b.astype(BF16)
    a_l = (a - a_h.astype(F32)).astype(BF16)
    b_l = (b - b_h.astype(F32)).astype(BF16)
    d = functools.partial(lax.dot_general, dimension_numbers=dims, preferred_element_type=F32)
    return d(a_h, b_h) + d(a_l, b_h) + d(a_h, b_l)


def _dot(a, b, hi=False):
    if hi:
        return _dot_split(a.astype(F32), b.astype(F32))
    return jnp.dot(a.astype(BF16), b.astype(BF16), preferred_element_type=F32)


def _dot_nt(a, b, hi=False):
    if hi:
        return _dot_split(a.astype(F32), b.astype(F32), _NT)
    return lax.dot_general(a.astype(BF16), b.astype(BF16), _NT, preferred_element_type=F32)


def _dot_mode(a, b, mode):
    if mode == "hi":
        return _dot_hi(a, b)
    if mode == "split":
        return _dot_split(a, b)
    return _dot(a, b)


def _act_dtype(hi):
    return F32 if hi else BF16


def _dot_hi(a, b):
    return jnp.dot(a, b, precision=HI, preferred_element_type=F32)


def _dot_nt_hi(a, b):
    return lax.dot_general(a, b, (((1,), (1,)), ((), ())), precision=HI, preferred_element_type=F32)


def _rope_rot(p, cos_t, sin_a, sin_b):
    return p * cos_t + pltpu.roll(p, LANES - MLA_ROPE // 2, 1) * sin_a + pltpu.roll(p, MLA_ROPE // 2, 1) * sin_b


def _ada_kernel(c_ref, w_ref, b_ref, o_ref):
    c = c_ref[...]
    o_ref[0] = _dot_hi(_silu(c), w_ref[0]) + b_ref[0]


def _ada_mod(c_all, w_ada, b_ada):
    nb = c_all.shape[0]
    return pl.pallas_call(
        _ada_kernel,
        grid=(DEPTH, 6),
        in_specs=[pl.BlockSpec((nb, D_MODEL), lambda i, j: (0, 0)),
                  pl.BlockSpec((1, D_MODEL, D_MODEL), lambda i, j: (i, 0, j)),
                  pl.BlockSpec((1, 1, D_MODEL), lambda i, j: (i, 0, j))],
        out_specs=pl.BlockSpec((1, nb, D_MODEL), lambda i, j: (i, 0, j)),
        out_shape=jax.ShapeDtypeStruct((DEPTH, nb, 6 * D_MODEL), F32),
        compiler_params=_cparams(("parallel", "parallel")),
        name="ada_mod",
    )(c_all, w_ada, b_ada.reshape(DEPTH, 1, 6 * D_MODEL))


def _in_kernel(x_ref, mod_ref, g_ref, w_ref, *out_refs, segs, shift_row, scale_row, hi):
    x = x_ref[0]
    h = x * lax.rsqrt(jnp.mean(x * x, axis=-1, keepdims=True) + EPS) * g_ref[...]
    h = h * (1.0 + mod_ref[0, scale_row:scale_row + 1, :]) + mod_ref[0, shift_row:shift_row + 1, :]
    r = _dot(h, w_ref[...], hi)
    for (off, width), o_ref in zip(segs, out_refs):
        o_ref[0] = r[:, off:off + width].astype(o_ref.dtype)


def _in_proj(x, mod, g, w, segs, dtypes, shift_row, scale_row, tm, hi):
    B, L, _ = x.shape
    n_p = w.shape[1]
    tm = min(tm, L)
    return pl.pallas_call(
        functools.partial(_in_kernel, segs=segs, shift_row=shift_row, scale_row=scale_row, hi=hi),
        grid=(B, L // tm),
        in_specs=[pl.BlockSpec((1, tm, D_MODEL), lambda b, i: (b, i, 0)),
                  pl.BlockSpec((1, 6, D_MODEL), lambda b, i: (b, 0, 0)),
                  pl.BlockSpec((1, D_MODEL), lambda b, i: (0, 0)),
                  pl.BlockSpec((D_MODEL, n_p), lambda b, i: (0, 0))],
        out_specs=[pl.BlockSpec((1, tm, wd), lambda b, i: (b, i, 0)) for _, wd in segs],
        out_shape=[jax.ShapeDtypeStruct((B, L, wd), dt) for (_, wd), dt in zip(segs, dtypes)],
        compiler_params=_cparams(("parallel", "parallel")),
        name="in_proj",
    )(x, mod, g, w)


def _ssd_kernel(z_ref, xbc_ref, dtr_ref, cst_ref, s0_ref, cw_ref, cb_ref, dtb_ref, alog_ref, dsk_ref,
                nrm_ref, e_ref, y_ref, sout_ref, cout_ref, state_sc, cbuf, *, T, n_valid, hi):
    c = pl.program_id(1)
    nc = pl.num_programs(1)
    halo = SSD_CONV - 1
    base = SUBLANES - halo

    @pl.when(c == 0)
    def _():
        state_sc[...] = s0_ref[0]
        cbuf[base:SUBLANES, :] = cst_ref[0]

    cbuf[SUBLANES:SUBLANES + T, :] = xbc_ref[0]
    conv = cb_ref[...] + cw_ref[0:1, :] * cbuf[base:base + T, :]
    for k in range(1, SSD_CONV):
        conv = conv + cw_ref[k:k + 1, :] * cbuf[base + k:base + k + T, :]
    xc = _silu(conv)

    c_last = (n_valid - 1) // T
    nv_last = n_valid - c_last * T

    @pl.when(c == c_last)
    def _():
        cout_ref[0] = cbuf[base + nv_last:base + nv_last + halo, :]

    cbuf[base:SUBLANES, :] = cbuf[base + T:SUBLANES + T, :]

    xs = xc[:, :SSD_D_INNER]
    gn = SSD_GROUPS * SSD_STATE
    bm = xc[:, SSD_D_INNER:SSD_D_INNER + gn]
    cm = xc[:, SSD_D_INNER + gn:]

    tok = lax.broadcasted_iota(I32, (T, 1), 0) + c * T
    dt = jnp.where(tok < n_valid, _softplus(dtr_ref[0] + dtb_ref[...]), 0.0)
    a = dt * (-jnp.exp(alog_ref[...]))
    ri = lax.broadcasted_iota(I32, (T, T), 0)
    ci = lax.broadcasted_iota(I32, (T, T), 1)
    causal = ci <= ri
    tril = jnp.where(causal, 1.0, 0.0).astype(F32)
    acum = _dot_hi(tril, a)
    eye = jnp.where(lax.broadcasted_iota(I32, (LANES, LANES), 0) == lax.broadcasted_iota(I32, (LANES, LANES), 1),
                    1.0, 0.0).astype(F32)
    acum_t = _dot_nt_hi(eye, acum)
    a_last = acum[T - 1:T, :]
    e = e_ref[...]
    xdt = xs * _dot_hi(dt, e)
    eacum_x = jnp.exp(_dot_hi(acum, e))
    xdend = xdt * jnp.exp(_dot_hi(a_last - acum, e))

    r = SSD_HEADS // SSD_GROUPS
    gw = r * SSD_HEADDIM
    y_groups = []
    for g in range(SSD_GROUPS):
        bg = bm[:, g * SSD_STATE:(g + 1) * SSD_STATE]
        cg = cm[:, g * SSD_STATE:(g + 1) * SSD_STATE]
        cb_mat = _dot_nt(cg, bg, hi)
        ys = []
        for j in range(r):
            h = g * r + j
            seg = acum[:, h:h + 1] - acum_t[h:h + 1, :]
            lm = jnp.where(causal, jnp.exp(jnp.minimum(seg, 0.0)), 0.0)
            ys.append(_dot(cb_mat * lm, xdt[:, h * SSD_HEADDIM:(h + 1) * SSD_HEADDIM], hi))
        y_diag = jnp.concatenate(ys, axis=-1)
        s_g = state_sc[g * gw:(g + 1) * gw, :]
        y_off = _dot_nt(cg, s_g, hi) * eacum_x[:, g * gw:(g + 1) * gw]
        y_groups.append(y_diag + y_off)
        cs = _dot(jnp.transpose(xdend[:, g * gw:(g + 1) * gw]), bg, hi)
        dec = jnp.concatenate(
            [jnp.broadcast_to(jnp.exp(acum_t[g * r + j:g * r + j + 1, T - 1:T]), (SSD_HEADDIM, SSD_STATE))
             for j in range(r)], axis=0)
        state_sc[g * gw:(g + 1) * gw, :] = s_g * dec + cs
    y = jnp.concatenate(y_groups, axis=-1)
    y = y + xs * dsk_ref[...]
    y = y * _silu(z_ref[0])
    outs = []
    for g in range(SSD_GROUPS):
        yg = y[:, g * gw:(g + 1) * gw]
        outs.append(yg * lax.rsqrt(jnp.mean(yg * yg, axis=-1, keepdims=True) + EPS))
    y_ref[0] = (jnp.concatenate(outs, axis=-1) * nrm_ref[...]).astype(y_ref.dtype)

    @pl.when(c == nc - 1)
    def _():
        sout_ref[0] = state_sc[...]


def _ssd(z, xbc, dtr, cst, s0, cw, cb, dtb, alog, dsk, nrm, e, T, n_valid, hi):
    B, Lp, _ = z.shape
    hp = SSD_HEADS * SSD_HEADDIM
    row = lambda b, c: (0, 0)
    return pl.pallas_call(
        functools.partial(_ssd_kernel, T=T, n_valid=n_valid, hi=hi),
        grid=(B, Lp // T),
        in_specs=[pl.BlockSpec((1, T, SSD_D_INNER), lambda b, c: (b, c, 0)),
                  pl.BlockSpec((1, T, SSD_XBC), lambda b, c: (b, c, 0)),
                  pl.BlockSpec((1, T, LANES), lambda b, c: (b, c, 0)),
                  pl.BlockSpec((1, SSD_CONV - 1, SSD_XBC), lambda b, c: (b, 0, 0)),
                  pl.BlockSpec((1, hp, SSD_STATE), lambda b, c: (b, 0, 0)),
                  pl.BlockSpec((SSD_CONV, SSD_XBC), row),
                  pl.BlockSpec((1, SSD_XBC), row),
                  pl.BlockSpec((1, LANES), row),
                  pl.BlockSpec((1, LANES), row),
                  pl.BlockSpec((1, SSD_D_INNER), row),
                  pl.BlockSpec((1, SSD_D_INNER), row),
                  pl.BlockSpec((LANES, SSD_D_INNER), row)],
        out_specs=[pl.BlockSpec((1, T, SSD_D_INNER), lambda b, c: (b, c, 0)),
                   pl.BlockSpec((1, hp, SSD_STATE), lambda b, c: (b, 0, 0)),
                   pl.BlockSpec((1, SSD_CONV - 1, SSD_XBC), lambda b, c: (b, 0, 0))],
        out_shape=[jax.ShapeDtypeStruct((B, Lp, SSD_D_INNER), _act_dtype(hi)),
                   jax.ShapeDtypeStruct((B, hp, SSD_STATE), F32),
                   jax.ShapeDtypeStruct((B, SSD_CONV - 1, SSD_XBC), F32)],
        scratch_shapes=[pltpu.VMEM((hp, SSD_STATE), F32),
                        pltpu.VMEM((T + SUBLANES, SSD_XBC), F32)],
        compiler_params=_cparams(("parallel", "arbitrary")),
        name="ssd_scan",
    )(z, xbc, dtr, cst, s0, cw, cb, dtb, alog, dsk, nrm, e)


def _latkv_kernel(lat_ref, g_ref, cos_ref, sa_ref, sb_ref, ckv_ref, kpe_ref):
    lat = lat_ref[0]
    cr = lat[:, :MLA_KV_RANK]
    ckv_ref[0] = cr * lax.rsqrt(jnp.mean(cr * cr, axis=-1, keepdims=True) + EPS) * g_ref[...]
    kpe_ref[0] = _rope_rot(lat[:, MLA_KV_RANK:], cos_ref[...], sa_ref[...], sb_ref[...])


def _latkv_post(latkv, g, cos_t, sin_a, sin_b, tm):
    B, L, wp = latkv.shape
    tm = min(tm, L)
    tab = pl.BlockSpec((tm, LANES), lambda b, i: (i, 0))
    return pl.pallas_call(
        _latkv_kernel,
        grid=(B, L // tm),
        in_specs=[pl.BlockSpec((1, tm, wp), lambda b, i: (b, i, 0)),
                  pl.BlockSpec((1, MLA_KV_RANK), lambda b, i: (0, 0)), tab, tab, tab],
        out_specs=[pl.BlockSpec((1, tm, MLA_KV_RANK), lambda b, i: (b, i, 0)),
                   pl.BlockSpec((1, tm, LANES), lambda b, i: (b, i, 0))],
        out_shape=[jax.ShapeDtypeStruct((B, L, MLA_KV_RANK), F32),
                   jax.ShapeDtypeStruct((B, L, LANES), F32)],
        compiler_params=_cparams(("parallel", "parallel")),
        name="mla_latent_kv",
    )(latkv, g, cos_t, sin_a, sin_b)


def _q_kernel(lat_ref, g_ref, w_ref, cos_ref, sa_ref, sb_ref, q_ref, *, scale, hi):
    lat = lat_ref[0]
    n = lat * lax.rsqrt(jnp.mean(lat * lat, axis=-1, keepdims=True) + EPS) * g_ref[...]
    q = _dot(n, w_ref[0], hi)
    a = q[:, :LANES] * scale
    p = _rope_rot(q[:, LANES:], cos_ref[...], sa_ref[...], sb_ref[...]) * scale
    q_ref[0, 0] = jnp.concatenate([a, p], axis=-1).astype(q_ref.dtype)


def _q_proj(latq, g, wq, cos_t, sin_a, sin_b, tm, hi):
    B, L, _ = latq.shape
    tm = min(tm, L)
    tab = pl.BlockSpec((tm, LANES), lambda b, i, h: (i, 0))
    scale = (MLA_NOPE + MLA_ROPE) ** -0.5 * math.log2(math.e)
    return pl.pallas_call(
        functools.partial(_q_kernel, scale=scale, hi=hi),
        grid=(B, L // tm, MLA_HEADS),
        in_specs=[pl.BlockSpec((1, tm, MLA_Q_RANK), lambda b, i, h: (b, i, 0)),
                  pl.BlockSpec((1, MLA_Q_RANK), lambda b, i, h: (0, 0)),
                  pl.BlockSpec((1, MLA_Q_RANK, 2 * LANES), lambda b, i, h: (h, 0, 0)), tab, tab, tab],
        out_specs=pl.BlockSpec((1, 1, tm, 2 * LANES), lambda b, i, h: (b, h, i, 0)),
        out_shape=jax.ShapeDtypeStruct((B, MLA_HEADS, L, 2 * LANES), _act_dtype(hi)),
        compiler_params=_cparams(("parallel", "parallel", "parallel")),
        name="mla_q_proj",
    )(latq, g, wq, cos_t, sin_a, sin_b)


def _kvup_kernel(ckv_ref, kpe_ref, w_ref, kv_ref, *, hi):
    kv = _dot(ckv_ref[0], w_ref[0], hi)
    kv_ref[0, 0] = jnp.concatenate([kv, kpe_ref[0]], axis=-1).astype(kv_ref.dtype)


def _kv_up(ckv, kpe, wkv, tm, hi):
    B, Lk, _ = ckv.shape
    tm = min(tm, Lk)
    return pl.pallas_call(
        functools.partial(_kvup_kernel, hi=hi),
        grid=(B, Lk // tm, MLA_HEADS),
        in_specs=[pl.BlockSpec((1, tm, MLA_KV_RANK), lambda b, i, h: (b, i, 0)),
                  pl.BlockSpec((1, tm, LANES), lambda b, i, h: (b, i, 0)),
                  pl.BlockSpec((1, MLA_KV_RANK, LANES), lambda b, i, h: (h, 0, 0))],
        out_specs=pl.BlockSpec((1, 1, tm, 2 * LANES), lambda b, i, h: (b, h, i, 0)),
        out_shape=jax.ShapeDtypeStruct((B, MLA_HEADS, Lk, 2 * LANES), _act_dtype(hi)),
        compiler_params=_cparams(("parallel", "parallel", "parallel")),
        name="mla_kv_up",
    )(ckv, kpe, wkv)


def _attn_kernel(q_ref, kv_ref, o_ref, m_sc, l_sc, acc_sc, *, tq, tk, tkm, q_off, kv_len, hi):
    q0 = pl.program_id(2) * tq
    first = q_off + q0
    n_all = jnp.minimum(kv_len, (first // CHUNK + 1) * CHUNK)
    n_any = jnp.minimum(kv_len, ((first + tq - 1) // CHUNK + 1) * CHUNK)
    n_full = n_all // tk
    m_lo = n_full * (tk // tkm)
    m_hi = (n_any + tkm - 1) // tkm
    qchunk = jnp.right_shift(first + lax.broadcasted_iota(I32, (tq, 1), 0), CHUNK_SHIFT)

    def block(hd, k0, width, masked):
        q = q_ref[0, hd]
        k = kv_ref[0, hd, pl.ds(k0, width), :]
        s = _dot_nt(q, k, hi)
        if masked:
            kpos = k0 + lax.broadcasted_iota(I32, (1, width), 1)
            vis = jnp.logical_and(jnp.right_shift(kpos, CHUNK_SHIFT) <= qchunk, kpos < kv_len)
            s = jnp.where(vis, s, NEG_BIG)
        m_prev = m_sc[hd]
        m_new = jnp.maximum(m_prev, jnp.max(s, axis=-1, keepdims=True))
        alpha = jnp.exp2(m_prev - m_new)
        p = jnp.exp2(s - jnp.tile(m_new, (1, width // LANES)))
        l_sc[hd] = alpha * l_sc[hd] + jnp.sum(p, axis=-1, keepdims=True)
        acc_sc[hd] = alpha * acc_sc[hd] + _dot(p, k[:, :LANES], hi)
        m_sc[hd] = m_new

    m_sc[...] = jnp.full((2, tq, LANES), NEG_BIG, F32)
    l_sc[...] = jnp.zeros((2, tq, LANES), F32)
    acc_sc[...] = jnp.zeros((2, tq, LANES), F32)

    def full_body(j, carry):
        for hd in range(2):
            block(hd, pl.multiple_of(j * tk, tk), tk, False)
        return carry

    def masked_body(j, carry):
        for hd in range(2):
            block(hd, pl.multiple_of(j * tkm, tkm), tkm, True)
        return carry

    lax.fori_loop(0, n_full, full_body, 0)
    lax.fori_loop(m_lo, m_hi, masked_body, 0)
    outs = [acc_sc[hd] / l_sc[hd] for hd in range(2)]
    lane = lax.broadcasted_iota(I32, (tq, LANES), 1)
    o = jnp.where(lane < MLA_V, pltpu.roll(outs[0], MLA_V, 1), outs[1])
    o_ref[0] = o.astype(o_ref.dtype)


def _attention(q, kv, tq, tk, tkm, q_off, kv_len, hi):
    B, H, L, _ = q.shape
    Lk = kv.shape[2]
    return pl.pallas_call(
        functools.partial(_attn_kernel, tq=tq, tk=tk, tkm=tkm, q_off=q_off, kv_len=kv_len, hi=hi),
        grid=(B, H // 2, L // tq),
        in_specs=[pl.BlockSpec((1, 2, tq, 2 * LANES), lambda b, h, i: (b, h, i, 0)),
                  pl.BlockSpec((1, 2, Lk, 2 * LANES), lambda b, h, i: (b, h, 0, 0))],
        out_specs=pl.BlockSpec((1, tq, LANES), lambda b, h, i: (b, i, h)),
        out_shape=jax.ShapeDtypeStruct((B, L, H * MLA_V), _act_dtype(hi)),
        scratch_shapes=[pltpu.VMEM((2, tq, LANES), F32)] * 3,
        compiler_params=_cparams(("parallel", "parallel", "parallel")),
        name="mla_attention",
    )(q, kv)


def _gdn_kernel(qkv_ref, gate_ref, ba_ref, cst_ref, s0_ref, cw_ref, dtb_ref, alog_ref, nrm_ref,
                o_ref, sout_ref, cout_ref, state_sc, cbuf, *, bb_n, **kw):
    for b in range(bb_n):
        _gdn_stream(qkv_ref.at[b], gate_ref.at[b], ba_ref.at[b], cst_ref.at[b], s0_ref.at[b], cw_ref, dtb_ref,
                    alog_ref, nrm_ref, o_ref.at[b], sout_ref.at[b], cout_ref.at[b], state_sc.at[b], cbuf.at[b], **kw)


def _gdn_stream(qkv_ref, gate_ref, ba_ref, cst_ref, s0_ref, cw_ref, dtb_ref, alog_ref, nrm_ref,
                o_ref, sout_ref, cout_ref, state_sc, cbuf, *, T, n_valid, hi, inv_mode):
    c = pl.program_id(1)
    nc = pl.num_programs(1)
    halo = GDN_CONV - 1
    base = SUBLANES - halo
    H = GDN_HEADS
    DK = GDN_DK

    @pl.when(c == 0)
    def _():
        state_sc[...] = s0_ref[...]
        cbuf[base:SUBLANES, :] = cst_ref[...]

    cbuf[SUBLANES:SUBLANES + T, :] = qkv_ref[...]
    conv = cw_ref[0:1, :] * cbuf[base:base + T, :]
    for k in range(1, GDN_CONV):
        conv = conv + cw_ref[k:k + 1, :] * cbuf[base + k:base + k + T, :]
    qkv = _silu(conv)

    c_last = (n_valid - 1) // T
    nv_last = n_valid - c_last * T

    @pl.when(c == c_last)
    def _():
        cout_ref[...] = cbuf[base + nv_last:base + nv_last + halo, :]

    cbuf[base:SUBLANES, :] = cbuf[base + T:SUBLANES + T, :]

    tok = lax.broadcasted_iota(I32, (T, 1), 0) + c * T
    valid = tok < n_valid
    ba = ba_ref[...]
    beta = jnp.where(valid, _sigmoid(ba), 0.0)
    g = jnp.where(valid, -jnp.exp(alog_ref[...]) * _softplus(ba + dtb_ref[...]), 0.0)
    ri = lax.broadcasted_iota(I32, (T, T), 0)
    ci = lax.broadcasted_iota(I32, (T, T), 1)
    tril = jnp.where(ci <= ri, 1.0, 0.0).astype(F32)
    gc = _dot_hi(tril, g)

    def l2n(x):
        return x * lax.rsqrt(jnp.sum(x * x, axis=-1, keepdims=True) + EPS)

    qn = [l2n(qkv[:, h * DK:(h + 1) * DK]) * (DK ** -0.5) for h in range(H)]
    kn = [l2n(qkv[:, (H + h) * DK:(H + h + 1) * DK]) for h in range(H)]
    vv = [qkv[:, (2 * H + h) * DK:(2 * H + h + 1) * DK] for h in range(H)]

    G = 4
    W = G * T
    bi = lax.broadcasted_iota(I32, (W, W), 0)
    bj = lax.broadcasted_iota(I32, (W, W), 1)
    t_shift = T.bit_length() - 1
    same = jnp.right_shift(bi, t_shift) == jnp.right_shift(bj, t_shift)
    strict = jnp.logical_and(same, bj < bi)
    incl = jnp.logical_and(same, bj <= bi)
    eye_w = jnp.where(bi == bj, 1.0, 0.0).astype(F32)
    rowhead = jnp.right_shift(lax.broadcasted_iota(I32, (W, DK), 0), t_shift)

    o_heads = [None] * H
    for grp in range(H // G):
        hs = [grp * G + j for j in range(G)]
        kst = jnp.concatenate([kn[h] for h in hs], axis=0)
        qst = jnp.concatenate([qn[h] for h in hs], axis=0)
        vst = jnp.concatenate([vv[h] for h in hs], axis=0)
        cb = jnp.concatenate([jnp.broadcast_to(gc[:, H + h:H + h + 1], (T, W)) for h in hs], axis=0)
        bb = jnp.concatenate([jnp.broadcast_to(beta[:, h:h + 1], (T, DK)) for h in hs], axis=0)
        glast = jnp.concatenate([jnp.broadcast_to(gc[T - 1:T, H + h:H + h + 1], (T, DK)) for h in hs], axis=0)
        dec = jnp.exp(jnp.minimum(cb - jnp.transpose(cb), 0.0))
        kk = _dot_nt(kst, kst, hi)
        qk = _dot_nt(qst, kst, hi)
        bbw = jnp.concatenate([bb, bb], axis=-1)
        a_mat = jnp.where(strict, bbw * kk * dec, 0.0)
        qkm = jnp.where(incl, qk * dec, 0.0)
        p_mat = eye_w - a_mat
        a_pow = a_mat
        n_sq = max(1, int(math.ceil(math.log2(T))) - 1)
        for it in range(n_sq):
            a_pow = _dot_mode(a_pow, a_pow, inv_mode)
            p_mat = p_mat + _dot_mode(p_mat, a_pow, inv_mode)
        cbk = cb[:, :DK]
        eg = jnp.exp(cbk)
        rhs = jnp.concatenate([vst * bb, kst * bb * eg], axis=-1)
        sol = _dot_mode(p_mat, rhs, inv_mode)
        u_st = sol[:, :DK]
        w_st = sol[:, DK:]
        qdec = qst * eg
        kdec_t = jnp.transpose(kst * jnp.exp(glast - cbk))
        vnew = []
        ooff = []
        for j, h in enumerate(hs):
            s_h = state_sc[h * DK:(h + 1) * DK, :]
            lhs = jnp.concatenate([w_st[j * T:(j + 1) * T], qdec[j * T:(j + 1) * T]], axis=0)
            ws = _dot(lhs, s_h, hi)
            vnew.append(u_st[j * T:(j + 1) * T] - ws[:T])
            ooff.append(ws[T:])
        vnew_st = jnp.concatenate(vnew, axis=0)
        o_st = jnp.concatenate(ooff, axis=0) + _dot(qkm, vnew_st, hi)
        for j, h in enumerate(hs):
            s_h = state_sc[h * DK:(h + 1) * DK, :]
            vm = jnp.where(rowhead == j, vnew_st, 0.0)
            last = jnp.exp(jnp.broadcast_to(gc[T - 1:T, H + h:H + h + 1], (DK, GDN_DV)))
            state_sc[h * DK:(h + 1) * DK, :] = s_h * last + _dot(kdec_t, vm, hi)
            o_heads[h] = o_st[j * T:(j + 1) * T]
    gate = gate_ref[...]
    outs = []
    for h in range(H):
        oh = o_heads[h]
        oh = oh * lax.rsqrt(jnp.mean(oh * oh, axis=-1, keepdims=True) + EPS) * nrm_ref[...]
        outs.append(oh * _silu(gate[:, h * GDN_DV:(h + 1) * GDN_DV]))
    o_ref[...] = jnp.concatenate(outs, axis=-1).astype(o_ref.dtype)

    @pl.when(c == nc - 1)
    def _():
        sout_ref[...] = state_sc[...]


def _gdn(qkv, gate, ba, cst, s0, cw, dtb, alog, nrm, T, n_valid, hi, inv_mode):
    B, Lp, _ = qkv.shape
    hk = GDN_HEADS * GDN_DK
    bb_n = 1
    row = lambda b, c: (0, 0)
    return pl.pallas_call(
        functools.partial(_gdn_kernel, bb_n=bb_n, T=T, n_valid=n_valid, hi=hi, inv_mode=inv_mode),
        grid=(B // bb_n, Lp // T),
        in_specs=[pl.BlockSpec((bb_n, T, GDN_QKV), lambda b, c: (b, c, 0)),
                  pl.BlockSpec((bb_n, T, GDN_HEADS * GDN_DV), lambda b, c: (b, c, 0)),
                  pl.BlockSpec((bb_n, T, LANES), lambda b, c: (b, c, 0)),
                  pl.BlockSpec((bb_n, GDN_CONV - 1, GDN_QKV), lambda b, c: (b, 0, 0)),
                  pl.BlockSpec((bb_n, hk, GDN_DV), lambda b, c: (b, 0, 0)),
                  pl.BlockSpec((GDN_CONV, GDN_QKV), row),
                  pl.BlockSpec((1, LANES), row),
                  pl.BlockSpec((1, LANES), row),
                  pl.BlockSpec((1, GDN_DV), row)],
        out_specs=[pl.BlockSpec((bb_n, T, GDN_HEADS * GDN_DV), lambda b, c: (b, c, 0)),
                   pl.BlockSpec((bb_n, hk, GDN_DV), lambda b, c: (b, 0, 0)),
                   pl.BlockSpec((bb_n, GDN_CONV - 1, GDN_QKV), lambda b, c: (b, 0, 0))],
        out_shape=[jax.ShapeDtypeStruct((B, Lp, GDN_HEADS * GDN_DV), _act_dtype(hi)),
                   jax.ShapeDtypeStruct((B, hk, GDN_DV), F32),
                   jax.ShapeDtypeStruct((B, GDN_CONV - 1, GDN_QKV), F32)],
        scratch_shapes=[pltpu.VMEM((bb_n, hk, GDN_DV), F32),
                        pltpu.VMEM((bb_n, T + SUBLANES, GDN_QKV), F32)],
        compiler_params=_cparams(("parallel", "arbitrary")),
        name="gdn_scan",
    )(qkv, gate, ba, cst, s0, cw, dtb, alog, nrm)


def _sconv_kernel(b_ref, c_ref, v_ref, cst_ref, w_ref, o_ref, cout_ref, cbuf, *, T):
    i = pl.program_id(1)
    halo = SC_CONV - 1
    base = SUBLANES - halo

    @pl.when(i == 0)
    def _():
        cbuf[base:SUBLANES, :] = cst_ref[0]

    cbuf[SUBLANES:SUBLANES + T, :] = c_ref[0] * v_ref[0]
    conv = w_ref[0:1, :] * cbuf[base:base + T, :]
    for k in range(1, SC_CONV):
        conv = conv + w_ref[k:k + 1, :] * cbuf[base + k:base + k + T, :]
    o_ref[0] = (b_ref[0] * conv).astype(o_ref.dtype)
    cout_ref[0] = cbuf[base + T:SUBLANES + T, :]
    cbuf[base:SUBLANES, :] = cbuf[base + T:SUBLANES + T, :]


def _sconv(scb, scc, scv, cst, w, T, hi):
    B, L, _ = scb.shape
    T = min(T, L)
    blk = pl.BlockSpec((1, T, SC_WIDTH), lambda b, i: (b, i, 0))
    st = pl.BlockSpec((1, SC_CONV - 1, SC_WIDTH), lambda b, i: (b, 0, 0))
    return pl.pallas_call(
        functools.partial(_sconv_kernel, T=T),
        grid=(B, L // T),
        in_specs=[blk, blk, blk, st, pl.BlockSpec((SC_CONV, SC_WIDTH), lambda b, i: (0, 0))],
        out_specs=[blk, st],
        out_shape=[jax.ShapeDtypeStruct((B, L, SC_WIDTH), _act_dtype(hi)),
                   jax.ShapeDtypeStruct((B, SC_CONV - 1, SC_WIDTH), F32)],
        scratch_shapes=[pltpu.VMEM((T + SUBLANES, SC_WIDTH), F32)],
        compiler_params=_cparams(("parallel", "arbitrary")),
        name="short_conv",
    )(scb, scc, scv, cst, w)


def _out_kernel(y_ref, o_ref, x_ref, mod_ref, wy_ref, wo_ref, g_ref, wr_ref, br_ref,
                xn_ref, h_ref, lg_ref, *, hi):
    mix = _dot(y_ref[0], wy_ref[...], hi) + _dot(o_ref[0], wo_ref[...], hi)
    xn = x_ref[0] + mod_ref[0, 2:3, :] * mix
    xn_ref[0] = xn
    h = xn * lax.rsqrt(jnp.mean(xn * xn, axis=-1, keepdims=True) + EPS) * g_ref[...]
    h = h * (1.0 + mod_ref[0, 4:5, :]) + mod_ref[0, 3:4, :]
    h_ref[0] = h
    lg_ref[0] = _dot_hi(h, wr_ref[...]) + br_ref[...]


def _out_proj(y, o, x, mod, wy, wo, g, wr, br, tm, hi):
    B, L, _ = x.shape
    tm = min(tm, L)
    blk = lambda dt_w: pl.BlockSpec((1, tm, dt_w), lambda b, i: (b, i, 0))
    full = lambda s: pl.BlockSpec(s, lambda b, i: (0, 0))
    return pl.pallas_call(
        functools.partial(_out_kernel, hi=hi),
        grid=(B, L // tm),
        in_specs=[blk(D_MODEL), blk(D_MODEL), blk(D_MODEL),
                  pl.BlockSpec((1, 6, D_MODEL), lambda b, i: (b, 0, 0)),
                  full((D_MODEL, D_MODEL)), full((D_MODEL, D_MODEL)), full((1, D_MODEL)),
                  full((D_MODEL, LANES)), full((1, LANES))],
        out_specs=[blk(D_MODEL), blk(D_MODEL), blk(LANES)],
        out_shape=[jax.ShapeDtypeStruct((B, L, D_MODEL), F32),
                   jax.ShapeDtypeStruct((B, L, D_MODEL), F32),
                   jax.ShapeDtypeStruct((B, L, LANES), F32)],
        compiler_params=_cparams(("parallel", "parallel")),
        name="out_proj",
    )(y, o, x, mod, wy, wo, g, wr, br)


def _route_kernel(lg_ref, e_ref, rank_ref, gate_ref, cnt_ref, base_sc, *, tm):
    i = pl.program_id(0)

    @pl.when(i == 0)
    def _():
        base_sc[...] = jnp.zeros_like(base_sc)

    lg = lg_ref[...]
    lane_i = lax.broadcasted_iota(I32, (tm, LANES), 1)
    lane = lane_i.astype(F32)
    vals, idxs = [], []
    cur = lg
    for _ in range(TOP_K):
        m = jnp.max(cur, axis=-1, keepdims=True)
        idx = jnp.min(jnp.where(cur == m, lane, float(LANES)), axis=-1, keepdims=True)
        vals.append(m)
        idxs.append(idx)
        cur = jnp.where(lane == idx, -jnp.inf, cur)
    ex = [jnp.exp(v - vals[0]) for v in vals]
    den = ex[0] + ex[1] + ex[2] + ex[3]
    onehot = jnp.zeros((tm, LANES), F32)
    for idx in idxs:
        onehot = onehot + jnp.where(lane == idx, 1.0, 0.0)
    ri = lax.broadcasted_iota(I32, (tm, tm), 0)
    ci = lax.broadcasted_iota(I32, (tm, tm), 1)
    before = _dot(jnp.where(ci < ri, 1.0, 0.0), onehot) + base_sc[...]
    e_out = jnp.zeros((tm, LANES), I32)
    r_out = jnp.zeros((tm, LANES), I32)
    g_out = jnp.zeros((tm, LANES), F32)
    for k in range(TOP_K):
        rk = jnp.sum(jnp.where(lane == idxs[k], before, 0.0), axis=-1, keepdims=True)
        e_out = jnp.where(lane_i == k, idxs[k].astype(I32), e_out)
        r_out = jnp.where(lane_i == k, rk.astype(I32), r_out)
        g_out = jnp.where(lane_i == k, ex[k] / den, g_out)
    e_ref[...] = e_out
    rank_ref[...] = r_out
    gate_ref[...] = g_out
    base_sc[...] = base_sc[...] + jnp.sum(onehot, axis=0, keepdims=True)
    cnt_ref[...] = base_sc[...].astype(I32)


def _route(logits, tm):
    n_tok = logits.shape[0]
    tm = min(tm, n_tok)
    blk = pl.BlockSpec((tm, LANES), lambda i: (i, 0))
    return pl.pallas_call(
        functools.partial(_route_kernel, tm=tm),
        grid=(n_tok // tm,),
        in_specs=[blk],
        out_specs=[blk, blk, blk, pl.BlockSpec((1, LANES), lambda i: (0, 0))],
        out_shape=[jax.ShapeDtypeStruct((n_tok, LANES), I32),
                   jax.ShapeDtypeStruct((n_tok, LANES), I32),
                   jax.ShapeDtypeStruct((n_tok, LANES), F32),
                   jax.ShapeDtypeStruct((1, LANES), I32)],
        scratch_shapes=[pltpu.VMEM((1, LANES), F32)],
        compiler_params=_cparams(("arbitrary",)),
        name="moe_route",
    )(logits)


def _dispatch_kernel(dest_ref, h_ref, xin_hbm, out_hbm, sem, *, tm):
    del xin_hbm

    def issue(r, carry):
        for k in range(TOP_K):
            d = dest_ref[r * TOP_K + k]
            pltpu.make_async_copy(h_ref.at[pl.ds(r, 1), :], out_hbm.at[pl.ds(d, 1), :], sem).start()
        return carry

    lax.fori_loop(0, tm, issue, 0)
    for _ in range(TOP_K):
        pltpu.make_async_copy(h_ref, out_hbm.at[pl.ds(0, tm), :], sem).wait()


def _dispatch(h, dest_flat, n_rows, tm):
    n_tok = h.shape[0]
    tm = min(tm, n_tok)
    xin0 = jnp.zeros((n_rows, D_MODEL), F32)
    return pl.pallas_call(
        functools.partial(_dispatch_kernel, tm=tm),
        grid=(n_tok // tm,),
        in_specs=[pl.BlockSpec((tm * TOP_K,), lambda i: (i,), memory_space=pltpu.SMEM),
                  pl.BlockSpec((tm, D_MODEL), lambda i: (i, 0)),
                  pl.BlockSpec(memory_space=pl.ANY)],
        out_specs=pl.BlockSpec(memory_space=pl.ANY),
        out_shape=jax.ShapeDtypeStruct((n_rows, D_MODEL), F32),
        scratch_shapes=[pltpu.SemaphoreType.DMA(())],
        input_output_aliases={2: 0},
        compiler_params=_cparams(("arbitrary",)),
        name="moe_dispatch",
    )(dest_flat, h, xin0)


def _ffn_kernel(blk_e_ref, nused_ref, x_ref, wgu_ref, bgu_ref, wd_ref, bd_ref, o_ref, wgu_sc, wd_sc):
    i = pl.program_id(0)
    prev = blk_e_ref[jnp.maximum(i - 1, 0)]
    fresh = jnp.logical_or(i == 0, blk_e_ref[i] != prev)
    active = i < nused_ref[0]

    @pl.when(jnp.logical_and(active, fresh))
    def _():
        wgu_sc[...] = wgu_ref[0, 0].astype(BF16)
        wd_sc[...] = wd_ref[0, 0].astype(BF16)

    @pl.when(active)
    def _():
        gu = jnp.dot(x_ref[...].astype(BF16), wgu_sc[...], preferred_element_type=F32) + bgu_ref[0]
        gate = jnp.minimum(gu[:, :D_FF], SWIGLU_LIMIT)
        up = jnp.clip(gu[:, D_FF:], -SWIGLU_LIMIT, SWIGLU_LIMIT)
        act = (up + 1.0) * gate * _sigmoid(SWIGLU_ALPHA * gate)
        o_ref[...] = jnp.dot(act.astype(BF16), wd_sc[...], preferred_element_type=F32) + bd_ref[0]

    @pl.when(jnp.logical_not(active))
    def _():
        o_ref[...] = jnp.zeros_like(o_ref)


def _expert_ffn(xin, blk_e, n_used, w_gu, b_gu, w_down, b_down, layer):
    n_rows = xin.shape[0]
    n_blocks = n_rows // MOE_BM

    def row_map(i, be, nu):
        return (jnp.minimum(i, nu[0] - 1), 0)

    def e_map4(i, be, nu):
        return (layer, be[jnp.minimum(i, nu[0] - 1)], 0, 0)

    def e_map3(i, be, nu):
        return (layer * N_EXPERTS + be[jnp.minimum(i, nu[0] - 1)], 0, 0)

    grid_spec = pltpu.PrefetchScalarGridSpec(
        num_scalar_prefetch=2,
        grid=(n_blocks,),
        in_specs=[pl.BlockSpec((MOE_BM, D_MODEL), row_map),
                  pl.BlockSpec((1, 1, D_MODEL, 2 * D_FF), e_map4),
                  pl.BlockSpec((1, 1, 2 * D_FF), e_map3),
                  pl.BlockSpec((1, 1, D_FF, D_MODEL), e_map4),
                  pl.BlockSpec((1, 1, D_MODEL), e_map3)],
        out_specs=pl.BlockSpec((MOE_BM, D_MODEL), lambda i, be, nu: (i, 0)),
        scratch_shapes=[pltpu.VMEM((D_MODEL, 2 * D_FF), BF16),
                        pltpu.VMEM((D_FF, D_MODEL), BF16)],
    )
    return pl.pallas_call(
        _ffn_kernel,
        grid_spec=grid_spec,
        out_shape=jax.ShapeDtypeStruct((n_rows, D_MODEL), F32),
        compiler_params=_cparams(("arbitrary",)),
        name="moe_expert_ffn",
    )(blk_e, n_used, xin, w_gu, b_gu.reshape(DEPTH * N_EXPERTS, 1, 2 * D_FF), w_down,
      b_down.reshape(DEPTH * N_EXPERTS, 1, D_MODEL))


def _combine_kernel(dest_ref, f_hbm, gate_ref, x_ref, mod_ref, g_ref, o_ref, buf, sem, *, tm, final):
    def issue(r, carry):
        for k in range(TOP_K):
            d = dest_ref[r * TOP_K + k]
            pltpu.make_async_copy(f_hbm.at[pl.ds(d, 1), :], buf.at[k, pl.ds(r, 1), :], sem).start()
        return carry

    lax.fori_loop(0, tm, issue, 0)
    for k in range(TOP_K):
        pltpu.make_async_copy(f_hbm.at[pl.ds(0, tm), :], buf.at[k], sem).wait()
    gates = gate_ref[...]
    moe = gates[:, 0:1] * buf[0]
    for k in range(1, TOP_K):
        moe = moe + gates[:, k:k + 1] * buf[k]
    xo = x_ref[...] + mod_ref[0, 5:6, :] * moe
    if final:
        xo = xo * lax.rsqrt(jnp.mean(xo * xo, axis=-1, keepdims=True) + EPS) * g_ref[...]
    o_ref[...] = xo


def _combine(ffn_out, dest_flat, gates, x, mod, g_final, tiles_per_batch, tm, final):
    n_tok = x.shape[0]
    return pl.pallas_call(
        functools.partial(_combine_kernel, tm=tm, final=final),
        grid=(n_tok // tm,),
        in_specs=[pl.BlockSpec((tm * TOP_K,), lambda i: (i,), memory_space=pltpu.SMEM),
                  pl.BlockSpec(memory_space=pl.ANY),
                  pl.BlockSpec((tm, LANES), lambda i: (i, 0)),
                  pl.BlockSpec((tm, D_MODEL), lambda i: (i, 0)),
                  pl.BlockSpec((1, 6, D_MODEL), lambda i: (i // tiles_per_batch, 0, 0)),
                  pl.BlockSpec((1, D_MODEL), lambda i: (0, 0))],
        out_specs=pl.BlockSpec((tm, D_MODEL), lambda i: (i, 0)),
        out_shape=jax.ShapeDtypeStruct((n_tok, D_MODEL), F32),
        scratch_shapes=[pltpu.VMEM((TOP_K, tm, D_MODEL), F32), pltpu.SemaphoreType.DMA(())],
        compiler_params=_cparams(("arbitrary",)),
        name="moe_combine",
    )(dest_flat, ffn_out, gates, x, mod, g_final)


def _pad_cols(w, width):
    return jnp.pad(w, ((0, 0), (0, width - w.shape[1])))


def _pad_lanes(v, offset=0):
    return jnp.pad(v.astype(F32), (offset, LANES - offset - v.shape[0])).reshape(1, LANES)


def _rope_tables(pos):
    half = MLA_ROPE // 2
    inv = ROPE_THETA ** (-jnp.arange(half, dtype=F32) / half)
    ang = pos.astype(F32)[:, None] * inv[None, :]
    cos, sin = jnp.cos(ang), jnp.sin(ang)
    z = jnp.zeros_like(cos)
    pad = jnp.zeros((pos.shape[0], LANES - MLA_ROPE), F32)
    cos_t = jnp.concatenate([cos, cos, pad], axis=1)
    sin_a = jnp.concatenate([-sin, z, pad], axis=1)
    sin_b = jnp.concatenate([z, sin, pad], axis=1)
    return cos_t, sin_a, sin_b


def _pad_seq(t, lp):
    return jnp.pad(t, ((0, 0), (0, lp - t.shape[1]), (0, 0)))


def _even_weights(W, j, dt):
    w_in = W['ev_w_in'][j]
    o1 = SSD_D_INNER
    o2 = o1 + SSD_XBC
    o3 = o2 + SSD_HEADS
    o4 = o3 + MLA_Q_RANK
    w_ssd = jnp.concatenate([w_in[:, :o2], _pad_cols(w_in[:, o2:o3], LANES)], axis=1).astype(dt)
    w_mla = jnp.concatenate([w_in[:, o3:o4], _pad_cols(w_in[:, o4:], MLA_Q_RANK)], axis=1).astype(dt)
    wq = W['mla_w_q_up'][j].reshape(MLA_Q_RANK, MLA_HEADS, MLA_NOPE + MLA_ROPE)
    wq = jnp.concatenate([wq[..., :MLA_NOPE], jnp.zeros((MLA_Q_RANK, MLA_HEADS, LANES - MLA_NOPE), F32),
                          wq[..., MLA_NOPE:], jnp.zeros((MLA_Q_RANK, MLA_HEADS, LANES - MLA_ROPE), F32)], axis=-1)
    wq = jnp.transpose(wq, (1, 0, 2)).astype(dt)
    wkv = jnp.transpose(W['mla_w_kv_up'][j].reshape(MLA_KV_RANK, MLA_HEADS, MLA_NOPE + MLA_V),
                        (1, 0, 2)).astype(dt)
    expand = (jnp.arange(LANES)[:, None] == (jnp.arange(SSD_D_INNER) // SSD_HEADDIM)[None, :]).astype(F32)
    return dict(
        w_ssd=w_ssd, w_mla=w_mla, wq=wq, wkv=wkv, expand=expand,
        wy=W['ev_w_out'][j][:SSD_D_INNER].astype(dt), wo=W['ev_w_out'][j][SSD_D_INNER:].astype(dt),
        cw=W['ssd_conv_w'][j], cb=W['ssd_conv_b'][j].reshape(1, SSD_XBC),
        dtb=_pad_lanes(W['ssd_dt_bias'][j]), alog=_pad_lanes(W['ssd_a_log'][j]),
        dsk=jnp.repeat(W['ssd_d'][j].astype(F32), SSD_HEADDIM).reshape(1, SSD_D_INNER),
        nrm=W['ssd_norm'][j].reshape(1, SSD_D_INNER),
        qn=W['mla_q_norm'][j].reshape(1, MLA_Q_RANK), kvn=W['mla_kv_norm'][j].reshape(1, MLA_KV_RANK))


def _odd_weights(W, j, dt):
    w_in = W['od_w_in'][j]
    o1 = GDN_QKV
    o2 = o1 + GDN_HEADS * GDN_DV
    o3 = o2 + 2 * GDN_HEADS
    w_gdn = jnp.concatenate([w_in[:, :o2], _pad_cols(w_in[:, o2:o3], LANES)], axis=1).astype(dt)
    w_sc = w_in[:, o3:].astype(dt)
    return dict(
        w_gdn=w_gdn, w_sc=w_sc,
        wy=W['od_w_out'][j][:GDN_HEADS * GDN_DV].astype(dt), wo=W['od_w_out'][j][GDN_HEADS * GDN_DV:].astype(dt),
        cw=W['gdn_conv_w'][j], dtb=_pad_lanes(W['gdn_dt_bias'][j], GDN_HEADS),
        alog=_pad_lanes(W['gdn_a_log'][j], GDN_HEADS), nrm=W['gdn_norm'][j].reshape(1, GDN_DV),
        scw=W['sconv_w'][j])


def _moe(h, logits, x_new, mod, W, i, g_final, L, final):
    n_tok = h.shape[0]
    e_pad, rank_pad, gates, cnt = _route(logits, 256)
    counts = cnt[0, :N_EXPERTS]
    padded = (counts + MOE_BM - 1) // MOE_BM * MOE_BM
    pend = jnp.cumsum(padded)
    pstart = pend - padded
    e_sel = e_pad[:, :TOP_K]
    dest = (pstart[e_sel] + rank_pad[:, :TOP_K]).astype(I32).reshape(-1)
    n_blocks = n_tok * TOP_K // MOE_BM + N_EXPERTS
    n_rows = n_blocks * MOE_BM
    blk_start = jnp.arange(n_blocks, dtype=pend.dtype) * MOE_BM
    blk_e = jnp.minimum(jnp.sum((blk_start[:, None] >= pend[None, :]).astype(I32), axis=1), N_EXPERTS - 1)
    n_used = (pend[-1:] // MOE_BM).astype(I32)
    tm = min(256, L)
    xin = _dispatch(h, dest, n_rows, tm)
    f_out = _expert_ffn(xin, blk_e, n_used, W['w_gu'], W['b_gu'], W['w_down'], W['b_down'], i)
    return _combine(f_out, dest, gates, x_new, mod, g_final, L // tm, tm, final)


def _trunk(x, c_mod, pos0, caches, W, PW, seq_t, hi):
    B, L, _ = x.shape
    new = {}
    pos = pos0 + jnp.arange(L, dtype=I32)
    cos_t, sin_a, sin_b = _rope_tables(pos)
    wr_all = W['w_router']
    for i in range(DEPTH):
        mod = c_mod[i]
        g_mix = W['norm_mix'][i].reshape(1, D_MODEL)
        j = i // 2
        if i % 2 == 0:
            P = PW[i]
            z, xbc, dtr = _in_proj(x, mod, g_mix, P['w_ssd'],
                                   ((0, SSD_D_INNER), (SSD_D_INNER, SSD_XBC), (SSD_D_INNER + SSD_XBC, LANES)),
                                   (F32, F32, F32), 0, 1, 512, hi)
            latq, latkv = _in_proj(x, mod, g_mix, P['w_mla'], ((0, MLA_Q_RANK), (MLA_Q_RANK, MLA_Q_RANK)),
                                   (F32, F32), 0, 1, 512, hi)
            T = seq_t['ssd']
            lp = -(-L // T) * T
            y, s_new, cst_new = _ssd(_pad_seq(z, lp), _pad_seq(xbc, lp), _pad_seq(dtr, lp),
                                     caches['ssd_conv'][j], caches['ssd'][j].reshape(B, -1, SSD_STATE),
                                     P['cw'], P['cb'], P['dtb'], P['alog'], P['dsk'], P['nrm'], P['expand'], T, L, hi)
            y = y[:, :L]
            new['ssd'] = s_new.reshape(1, B, SSD_HEADS, SSD_HEADDIM, SSD_STATE)
            new['ssd_conv'] = cst_new[None]
            ckv_new, kpe_new = _latkv_post(latkv, P['kvn'], cos_t, sin_a, sin_b, 512)
            new['mla_ckv'] = ckv_new[None]
            new['mla_krope'] = kpe_new[None, :, :, :MLA_ROPE]
            ckv_past, kpe_past = caches['mla_ckv'][j], caches['mla_krope'][j]
            past = ckv_past.shape[1]
            kv_len = past + L
            tk = seq_t['tk']
            lk = -(-kv_len // tk) * tk
            ckv_all = _pad_seq(jnp.concatenate([ckv_past, ckv_new], axis=1), lk)
            kpe_all = _pad_seq(jnp.concatenate(
                [jnp.pad(kpe_past, ((0, 0), (0, 0), (0, LANES - MLA_ROPE))), kpe_new], axis=1), lk)
            kv = _kv_up(ckv_all, kpe_all, P['wkv'], tk, hi)
            q = _q_proj(latq, P['qn'], P['wq'], cos_t, sin_a, sin_b, 512, hi)
            o = _attention(q, kv, min(seq_t['tq'], L), tk, seq_t['tkm'], pos0, kv_len, hi)
        else:
            P = PW[i]
            hv = GDN_HEADS * GDN_DV
            qkv, gate, ba = _in_proj(x, mod, g_mix, P['w_gdn'],
                                     ((0, GDN_QKV), (GDN_QKV, hv), (GDN_QKV + hv, LANES)),
                                     (F32, F32, F32), 0, 1, 512, hi)
            scb, scc, scv = _in_proj(x, mod, g_mix, P['w_sc'],
                                     ((0, SC_WIDTH), (SC_WIDTH, SC_WIDTH), (2 * SC_WIDTH, SC_WIDTH)),
                                     (F32, F32, F32), 0, 1, 512, hi)
            T = CHUNK
            lp = -(-L // T) * T
            y, s_new, cst_new = _gdn(_pad_seq(qkv, lp), _pad_seq(gate, lp), _pad_seq(ba, lp),
                                     caches['gdn_conv'][j], caches['gdn'][j].reshape(B, -1, GDN_DV),
                                     P['cw'], P['dtb'], P['alog'], P['nrm'], T, L, hi,
                                     "hi" if hi else seq_t['gdn_inv'])
            y = y[:, :L]
            new['gdn'] = s_new.reshape(1, B, GDN_HEADS, GDN_DK, GDN_DV)
            new['gdn_conv'] = cst_new[None]
            o, sc_new = _sconv(scb, scc, scv, caches['sconv'][j], P['scw'], 512, hi)
            new['sconv'] = sc_new[None]
        wr = _pad_cols(wr_all[i], LANES)
        br = jnp.concatenate([W['b_router'][i].astype(F32), jnp.full((LANES - N_EXPERTS,), NEG_BIG, F32)]).reshape(1, LANES)
        x_new, h, logits = _out_proj(y, o, x, mod, P['wy'], P['wo'], W['norm_ffn'][i].reshape(1, D_MODEL),
                                     wr, br, 512, hi)
        final = i == DEPTH - 1
        xo = _moe(h.reshape(B * L, D_MODEL), logits.reshape(B * L, LANES), x_new.reshape(B * L, D_MODEL),
                  mod, W, i, W['norm_final'].reshape(1, D_MODEL), L, final)
        x = xo.reshape(B, L, D_MODEL)
    return x, new


def _prep_weights(W, dt):
    PW = {}
    for i in range(DEPTH):
        PW[i] = _even_weights(W, i // 2, dt) if i % 2 == 0 else _odd_weights(W, i // 2, dt)
    return PW


def kernel(x_prompt, x_sample, c_prompt, c_sample, cache_mla_ckv, cache_mla_krope, state_ssd, state_ssd_conv, state_gdn, state_gdn_conv, state_sconv, norm_mix, norm_ffn, w_ada, b_ada, w_router, b_router, w_gu, b_gu, w_down, b_down, norm_final, ev_w_in, ev_w_out, ssd_conv_w, ssd_conv_b, ssd_dt_bias, ssd_a_log, ssd_d, ssd_norm, mla_q_norm, mla_w_q_up, mla_kv_norm, mla_w_kv_up, od_w_in, od_w_out, gdn_conv_w, gdn_dt_bias, gdn_a_log, gdn_norm, sconv_w):
    W = dict(norm_mix=norm_mix, norm_ffn=norm_ffn, w_ada=w_ada, b_ada=b_ada, w_router=w_router,
             b_router=b_router, w_gu=w_gu, b_gu=b_gu, w_down=w_down, b_down=b_down, norm_final=norm_final,
             ev_w_in=ev_w_in, ev_w_out=ev_w_out, ssd_conv_w=ssd_conv_w, ssd_conv_b=ssd_conv_b,
             ssd_dt_bias=ssd_dt_bias, ssd_a_log=ssd_a_log, ssd_d=ssd_d, ssd_norm=ssd_norm,
             mla_q_norm=mla_q_norm, mla_w_q_up=mla_w_q_up, mla_kv_norm=mla_kv_norm, mla_w_kv_up=mla_w_kv_up,
             od_w_in=od_w_in, od_w_out=od_w_out, gdn_conv_w=gdn_conv_w, gdn_dt_bias=gdn_dt_bias,
             gdn_a_log=gdn_a_log, gdn_norm=gdn_norm, sconv_w=sconv_w)
    bp, bs = x_prompt.shape[0], x_sample.shape[0]
    nb = 16
    c_all = jnp.concatenate([c_prompt, c_sample, jnp.zeros((nb - bp - bs, D_MODEL), F32)], axis=0)
    mod_all = _ada_mod(c_all, w_ada, b_ada).reshape(DEPTH, nb, 6, D_MODEL)
    n_even, n_odd = (DEPTH + 1) // 2, DEPTH // 2
    zero_caches = dict(
        mla_ckv=jnp.zeros((n_even, bp, 0, MLA_KV_RANK), F32), mla_krope=jnp.zeros((n_even, bp, 0, MLA_ROPE), F32),
        ssd=jnp.zeros((n_even, bp, SSD_HEADS, SSD_HEADDIM, SSD_STATE), F32),
        ssd_conv=jnp.zeros((n_even, bp, SSD_CONV - 1, SSD_XBC), F32),
        gdn=jnp.zeros((n_odd, bp, GDN_HEADS, GDN_DK, GDN_DV), F32),
        gdn_conv=jnp.zeros((n_odd, bp, GDN_CONV - 1, GDN_QKV), F32),
        sconv=jnp.zeros((n_odd, bp, SC_CONV - 1, SC_WIDTH), F32))
    y_p, sp = _trunk(x_prompt, mod_all[:, :bp], 0, zero_caches, W, _prep_weights(W, BF16),
                     dict(ssd=256, tq=512, tk=1024, tkm=512, gdn_inv="split"), False)
    past = cache_mla_ckv.shape[2]
    caches = dict(mla_ckv=cache_mla_ckv, mla_krope=cache_mla_krope, ssd=state_ssd, ssd_conv=state_ssd_conv,
                  gdn=state_gdn, gdn_conv=state_gdn_conv, sconv=state_sconv)
    y_s, ss = _trunk(x_sample, mod_all[:, bp:bp + bs], past, caches, W, _prep_weights(W, F32),
                     dict(ssd=128, tq=32, tk=256, tkm=256, gdn_inv="hi"), True)
    return (y_p, y_s,
            sp['mla_ckv'], ss['mla_ckv'], sp['mla_krope'], ss['mla_krope'],
            sp['ssd'], ss['ssd'], sp['ssd_conv'], ss['ssd_conv'],
            sp['gdn'], ss['gdn'], sp['gdn_conv'], ss['gdn_conv'],
            sp['sconv'], ss['sconv'])
```

```python
import functools
import math

import jax
import jax.numpy as jnp
from jax import lax
from jax.experimental import pallas as pl
from jax.experimental.pallas import tpu as pltpu

F32 = jnp.float32
BF16 = jnp.bfloat16
I32 = jnp.int32
HI = lax.Precision.HIGHEST

D_MODEL = 1024
DEPTH = 2
CHUNK = 64
CHUNK_SHIFT = 6
EPS = 1e-6
SSD_D_INNER = D_MODEL
SSD_HEADDIM = 64
SSD_HEADS = SSD_D_INNER // SSD_HEADDIM
SSD_GROUPS = 4
SSD_STATE = 128
SSD_CONV = 4
SSD_XBC = SSD_D_INNER + 2 * SSD_GROUPS * SSD_STATE
MLA_HEADS = 16
MLA_NOPE = 64
MLA_ROPE = 32
MLA_V = 64
MLA_Q_RANK = 384
MLA_KV_RANK = 256
ROPE_THETA = 10000.0
GDN_HEADS = 8
GDN_DK = 128
GDN_DV = 128
GDN_CONV = 4
GDN_QKV = GDN_HEADS * (2 * GDN_DK + GDN_DV)
SC_WIDTH = D_MODEL
SC_CONV = 3
N_EXPERTS = 32
TOP_K = 4
D_FF = D_MODEL
SWIGLU_LIMIT = 7.0
SWIGLU_ALPHA = 1.702

LANES = 128
SUBLANES = 8
VMEM_LIMIT = 56 * 1024 * 1024

NEG_BIG = -1e30
MOE_BM = 512


def _cparams(sem):
    return pltpu.CompilerParams(dimension_semantics=sem, vmem_limit_bytes=VMEM_LIMIT)


def _sigmoid(x):
    return 1.0 / (1.0 + jnp.exp(-x))


def _silu(x):
    return x * _sigmoid(x)


def _softplus(x):
    return jnp.maximum(x, 0.0) + jnp.log1p(jnp.exp(-jnp.abs(x)))


_NN = (((1,), (0,)), ((), ()))
_NT = (((1,), (1,)), ((), ()))


def _dot_split(a, b, dims=_NN):
    a_h = a.astype(BF16)
    b_h = b.astype(BF16)
    a_l = (a - a_h.astype(F32)).astype(BF16)
    b_l = (b - b_h.astype(F32)).astype(BF16)
    d = functools.partial(lax.dot_general, dimension_numbers=dims, preferred_element_type=F32)
    return d(a_h, b_h) + d(a_l, b_h) + d(a_h, b_l)


def _dot(a, b, hi=False):
    if hi:
        return _dot_split(a.astype(F32), b.astype(F32))
    return jnp.dot(a.astype(BF16), b.astype(BF16), preferred_element_type=F32)


def _dot_nt(a, b, hi=False):
    if hi:
        return _dot_split(a.astype(F32), b.astype(F32), _NT)
    return lax.dot_general(a.astype(BF16), b.astype(BF16), _NT, preferred_element_type=F32)


def _dot_mode(a, b, mode):
    if mode == "hi":
        return _dot_hi(a, b)
    if mode == "split":
        return _dot_split(a, b)
    return _dot(a, b)


def _act_dtype(hi):
    return F32 if hi else BF16


def _dot_hi(a, b):
    return jnp.dot(a, b, precision=HI, preferred_element_type=F32)


def _dot_nt_hi(a, b):
    return lax.dot_general(a, b, (((1,), (1,)), ((), ())), precision=HI, preferred_element_type=F32)


def _rope_rot(p, cos_t, sin_a, sin_b):
    return p * cos_t + pltpu.roll(p, LANES - MLA_ROPE // 2, 1) * sin_a + pltpu.roll(p, MLA_ROPE // 2, 1) * sin_b


def _ada_kernel(c_ref, w_ref, b_ref, o_ref):
    c = c_ref[...]
    o_ref[0] = _dot_hi(_silu(c), w_ref[0]) + b_ref[0]


def _ada_mod(c_all, w_ada, b_ada):
    nb = c_all.shape[0]
    return pl.pallas_call(
        _ada_kernel,
        grid=(DEPTH, 6),
        in_specs=[pl.BlockSpec((nb, D_MODEL), lambda i, j: (0, 0)),
                  pl.BlockSpec((1, D_MODEL, D_MODEL), lambda i, j: (i, 0, j)),
                  pl.BlockSpec((1, 1, D_MODEL), lambda i, j: (i, 0, j))],
        out_specs=pl.BlockSpec((1, nb, D_MODEL), lambda i, j: (i, 0, j)),
        out_shape=jax.ShapeDtypeStruct((DEPTH, nb, 6 * D_MODEL), F32),
        compiler_params=_cparams(("parallel", "parallel")),
        name="ada_mod",
    )(c_all, w_ada, b_ada.reshape(DEPTH, 1, 6 * D_MODEL))


def _in_kernel(x_ref, mod_ref, g_ref, w_ref, *out_refs, segs, shift_row, scale_row, hi):
    x = x_ref[0]
    h = x * lax.rsqrt(jnp.mean(x * x, axis=-1, keepdims=True) + EPS) * g_ref[...]
    h = h * (1.0 + mod_ref[0, scale_row:scale_row + 1, :]) + mod_ref[0, shift_row:shift_row + 1, :]
    r = _dot(h, w_ref[...], hi)
    for (off, width), o_ref in zip(segs, out_refs):
        o_ref[0] = r[:, off:off + width].astype(o_ref.dtype)


def _in_proj(x, mod, g, w, segs, dtypes, shift_row, scale_row, tm, hi):
    B, L, _ = x.shape
    n_p = w.shape[1]
    tm = min(tm, L)
    return pl.pallas_call(
        functools.partial(_in_kernel, segs=segs, shift_row=shift_row, scale_row=scale_row, hi=hi),
        grid=(B, L // tm),
        in_specs=[pl.BlockSpec((1, tm, D_MODEL), lambda b, i: (b, i, 0)),
                  pl.BlockSpec((1, 6, D_MODEL), lambda b, i: (b, 0, 0)),
                  pl.BlockSpec((1, D_MODEL), lambda b, i: (0, 0)),
                  pl.BlockSpec((D_MODEL, n_p), lambda b, i: (0, 0))],
        out_specs=[pl.BlockSpec((1, tm, wd), lambda b, i: (b, i, 0)) for _, wd in segs],
        out_shape=[jax.ShapeDtypeStruct((B, L, wd), dt) for (_, wd), dt in zip(segs, dtypes)],
        compiler_params=_cparams(("parallel", "parallel")),
        name="in_proj",
    )(x, mod, g, w)


def _ssd_kernel(z_ref, xbc_ref, dtr_ref, cst_ref, s0_ref, cw_ref, cb_ref, dtb_ref, alog_ref, dsk_ref,
                nrm_ref, e_ref, y_ref, sout_ref, cout_ref, state_sc, cbuf, *, T, n_valid, hi):
    c = pl.program_id(1)
    nc = pl.num_programs(1)
    halo = SSD_CONV - 1
    base = SUBLANES - halo

    @pl.when(c == 0)
    def _():
        state_sc[...] = s0_ref[0]
        cbuf[base:SUBLANES, :] = cst_ref[0]

    cbuf[SUBLANES:SUBLANES + T, :] = xbc_ref[0]
    conv = cb_ref[...] + cw_ref[0:1, :] * cbuf[base:base + T, :]
    for k in range(1, SSD_CONV):
        conv = conv + cw_ref[k:k + 1, :] * cbuf[base + k:base + k + T, :]
    xc = _silu(conv)

    c_last = (n_valid - 1) // T
    nv_last = n_valid - c_last * T

    @pl.when(c == c_last)
    def _():
        cout_ref[0] = cbuf[base + nv_last:base + nv_last + halo, :]

    cbuf[base:SUBLANES, :] = cbuf[base + T:SUBLANES + T, :]

    xs = xc[:, :SSD_D_INNER]
    gn = SSD_GROUPS * SSD_STATE
    bm = xc[:, SSD_D_INNER:SSD_D_INNER + gn]
    cm = xc[:, SSD_D_INNER + gn:]

    tok = lax.broadcasted_iota(I32, (T, 1), 0) + c * T
    dt = jnp.where(tok < n_valid, _softplus(dtr_ref[0] + dtb_ref[...]), 0.0)
    a = dt * (-jnp.exp(alog_ref[...]))
    ri = lax.broadcasted_iota(I32, (T, T), 0)
    ci = lax.broadcasted_iota(I32, (T, T), 1)
    causal = ci <= ri
    tril = jnp.where(causal, 1.0, 0.0).astype(F32)
    acum = _dot_hi(tril, a)
    eye = jnp.where(lax.broadcasted_iota(I32, (LANES, LANES), 0) == lax.broadcasted_iota(I32, (LANES, LANES), 1),
                    1.0, 0.0).astype(F32)
    acum_t = _dot_nt_hi(eye, acum)
    a_last = acum[T - 1:T, :]
    e = e_ref[...]
    xdt = xs * _dot_hi(dt, e)
    eacum_x = jnp.exp(_dot_hi(acum, e))
    xdend = xdt * jnp.exp(_dot_hi(a_last - acum, e))

    r = SSD_HEADS // SSD_GROUPS
    gw = r * SSD_HEADDIM
    y_groups = []
    for g in range(SSD_GROUPS):
        bg = bm[:, g * SSD_STATE:(g + 1) * SSD_STATE]
        cg = cm[:, g * SSD_STATE:(g + 1) * SSD_STATE]
        cb_mat = _dot_nt(cg, bg, hi)
        ys = []
        for j in range(r):
            h = g * r + j
            seg = acum[:, h:h + 1] - acum_t[h:h + 1, :]
            lm = jnp.where(causal, jnp.exp(jnp.minimum(seg, 0.0)), 0.0)
            ys.append(_dot(cb_mat * lm, xdt[:, h * SSD_HEADDIM:(h + 1) * SSD_HEADDIM], hi))
        y_diag = jnp.concatenate(ys, axis=-1)
        s_g = state_sc[g * gw:(g + 1) * gw, :]
        y_off = _dot_nt(cg, s_g, hi) * eacum_x[:, g * gw:(g + 1) * gw]
        y_groups.append(y_diag + y_off)
        cs = _dot(jnp.transpose(xdend[:, g * gw:(g + 1) * gw]), bg, hi)
        dec = jnp.concatenate(
            [jnp.broadcast_to(jnp.exp(acum_t[g * r + j:g * r + j + 1, T - 1:T]), (SSD_HEADDIM, SSD_STATE))
             for j in range(r)], axis=0)
        state_sc[g * gw:(g + 1) * gw, :] = s_g * dec + cs
    y = jnp.concatenate(y_groups, axis=-1)
    y = y + xs * dsk_ref[...]
    y = y * _silu(z_ref[0])
    outs = []
    for g in range(SSD_GROUPS):
        yg = y[:, g * gw:(g + 1) * gw]
        outs.append(yg * lax.rsqrt(jnp.mean(yg * yg, axis=-1, keepdims=True) + EPS))
    y_ref[0] = (jnp.concatenate(outs, axis=-1) * nrm_ref[...]).astype(y_ref.dtype)

    @pl.when(c == nc - 1)
    def _():
        sout_ref[0] = state_sc[...]


def _ssd(z, xbc, dtr, cst, s0, cw, cb, dtb, alog, dsk, nrm, e, T, n_valid, hi):
    B, Lp, _ = z.shape
    hp = SSD_HEADS * SSD_HEADDIM
    row = lambda b, c: (0, 0)
    return pl.pallas_call(
        functools.partial(_ssd_kernel, T=T, n_valid=n_valid, hi=hi),
        grid=(B, Lp // T),
        in_specs=[pl.BlockSpec((1, T, SSD_D_INNER), lambda b, c: (b, c, 0)),
                  pl.BlockSpec((1, T, SSD_XBC), lambda b, c: (b, c, 0)),
                  pl.BlockSpec((1, T, LANES), lambda b, c: (b, c, 0)),
                  pl.BlockSpec((1, SSD_CONV - 1, SSD_XBC), lambda b, c: (b, 0, 0)),
                  pl.BlockSpec((1, hp, SSD_STATE), lambda b, c: (b, 0, 0)),
                  pl.BlockSpec((SSD_CONV, SSD_XBC), row),
                  pl.BlockSpec((1, SSD_XBC), row),
                  pl.BlockSpec((1, LANES), row),
                  pl.BlockSpec((1, LANES), row),
                  pl.BlockSpec((1, SSD_D_INNER), row),
                  pl.BlockSpec((1, SSD_D_INNER), row),
                  pl.BlockSpec((LANES, SSD_D_INNER), row)],
        out_specs=[pl.BlockSpec((1, T, SSD_D_INNER), lambda b, c: (b, c, 0)),
                   pl.BlockSpec((1, hp, SSD_STATE), lambda b, c: (b, 0, 0)),
                   pl.BlockSpec((1, SSD_CONV - 1, SSD_XBC), lambda b, c: (b, 0, 0))],
        out_shape=[jax.ShapeDtypeStruct((B, Lp, SSD_D_INNER), _act_dtype(hi)),
                   jax.ShapeDtypeStruct((B, hp, SSD_STATE), F32),
                   jax.ShapeDtypeStruct((B, SSD_CONV - 1, SSD_XBC), F32)],
        scratch_shapes=[pltpu.VMEM((hp, SSD_STATE), F32),
                        pltpu.VMEM((T + SUBLANES, SSD_XBC), F32)],
        compiler_params=_cparams(("parallel", "arbitrary")),
        name="ssd_scan",
    )(z, xbc, dtr, cst, s0, cw, cb, dtb, alog, dsk, nrm, e)


def _latkv_kernel(lat_ref, g_ref, cos_ref, sa_ref, sb_ref, ckv_ref, kpe_ref):
    lat = lat_ref[0]
    cr = lat[:, :MLA_KV_RANK]
    ckv_ref[0] = cr * lax.rsqrt(jnp.mean(cr * cr, axis=-1, keepdims=True) + EPS) * g_ref[...]
    kpe_ref[0] = _rope_rot(lat[:, MLA_KV_RANK:], cos_ref[...], sa_ref[...], sb_ref[...])


def _latkv_post(latkv, g, cos_t, sin_a, sin_b, tm):
    B, L, wp = latkv.shape
    tm = min(tm, L)
    tab = pl.BlockSpec((tm, LANES), lambda b, i: (i, 0))
    return pl.pallas_call(
        _latkv_kernel,
        grid=(B, L // tm),
        in_specs=[pl.BlockSpec((1, tm, wp), lambda b, i: (b, i, 0)),
                  pl.BlockSpec((1, MLA_KV_RANK), lambda b, i: (0, 0)), tab, tab, tab],
        out_specs=[pl.BlockSpec((1, tm, MLA_KV_RANK), lambda b, i: (b, i, 0)),
                   pl.BlockSpec((1, tm, LANES), lambda b, i: (b, i, 0))],
        out_shape=[jax.ShapeDtypeStruct((B, L, MLA_KV_RANK), F32),
                   jax.ShapeDtypeStruct((B, L, LANES), F32)],
        compiler_params=_cparams(("parallel", "parallel")),
        name="mla_latent_kv",
    )(latkv, g, cos_t, sin_a, sin_b)


def _q_kernel(lat_ref, g_ref, w_ref, cos_ref, sa_ref, sb_ref, q_ref, *, scale, hi):
    lat = lat_ref[0]
    n = lat * lax.rsqrt(jnp.mean(lat * lat, axis=-1, keepdims=True) + EPS) * g_ref[...]
    cos_t, sin_a, sin_b = cos_ref[...], sa_ref[...], sb_ref[...]
    for h in range(MLA_HEADS):
        q = _dot(n, w_ref[h], hi)
        a = q[:, :LANES] * scale
        p = _rope_rot(q[:, LANES:], cos_t, sin_a, sin_b) * scale
        q_ref[0, h] = jnp.concatenate([a, p], axis=-1).astype(q_ref.dtype)


def _q_proj(latq, g, wq, cos_t, sin_a, sin_b, tm, hi):
    B, L, _ = latq.shape
    tm = min(tm, L)
    tab = pl.BlockSpec((tm, LANES), lambda b, i: (i, 0))
    scale = (MLA_NOPE + MLA_ROPE) ** -0.5 * math.log2(math.e)
    return pl.pallas_call(
        functools.partial(_q_kernel, scale=scale, hi=hi),
        grid=(B, L // tm),
        in_specs=[pl.BlockSpec((1, tm, MLA_Q_RANK), lambda b, i: (b, i, 0)),
                  pl.BlockSpec((1, MLA_Q_RANK), lambda b, i: (0, 0)),
                  pl.BlockSpec((MLA_HEADS, MLA_Q_RANK, 2 * LANES), lambda b, i: (0, 0, 0)), tab, tab, tab],
        out_specs=pl.BlockSpec((1, MLA_HEADS, tm, 2 * LANES), lambda b, i: (b, 0, i, 0)),
        out_shape=jax.ShapeDtypeStruct((B, MLA_HEADS, L, 2 * LANES), _act_dtype(hi)),
        compiler_params=_cparams(("parallel", "parallel")),
        name="mla_q_proj",
    )(latq, g, wq, cos_t, sin_a, sin_b)


def _kvup_kernel(ckv_ref, kpe_ref, w_ref, kv_ref, *, hi):
    ckv = ckv_ref[0]
    kpe = kpe_ref[0]
    for h in range(MLA_HEADS):
        kv = _dot(ckv, w_ref[h], hi)
        kv_ref[0, h] = jnp.concatenate([kv, kpe], axis=-1).astype(kv_ref.dtype)


def _kv_up(ckv, kpe, wkv, tm, hi):
    B, Lk, _ = ckv.shape
    tm = min(tm, Lk)
    return pl.pallas_call(
        functools.partial(_kvup_kernel, hi=hi),
        grid=(B, Lk // tm),
        in_specs=[pl.BlockSpec((1, tm, MLA_KV_RANK), lambda b, i: (b, i, 0)),
                  pl.BlockSpec((1, tm, LANES), lambda b, i: (b, i, 0)),
                  pl.BlockSpec((MLA_HEADS, MLA_KV_RANK, LANES), lambda b, i: (0, 0, 0))],
        out_specs=pl.BlockSpec((1, MLA_HEADS, tm, 2 * LANES), lambda b, i: (b, 0, i, 0)),
        out_shape=jax.ShapeDtypeStruct((B, MLA_HEADS, Lk, 2 * LANES), _act_dtype(hi)),
        compiler_params=_cparams(("parallel", "parallel")),
        name="mla_kv_up",
    )(ckv, kpe, wkv)


def _attn_kernel(q_ref, kv_ref, o_ref, m_sc, l_sc, acc_sc, *, tq, tk, tkm, q_off, kv_len, hi):
    q0 = pl.program_id(2) * tq
    first = q_off + q0
    n_all = jnp.minimum(kv_len, (first // CHUNK + 1) * CHUNK)
    n_any = jnp.minimum(kv_len, ((first + tq - 1) // CHUNK + 1) * CHUNK)
    n_full = n_all // tk
    m_lo = n_full * (tk // tkm)
    m_hi = (n_any + tkm - 1) // tkm
    qchunk = jnp.right_shift(first + lax.broadcasted_iota(I32, (tq, 1), 0), CHUNK_SHIFT)

    def block(hd, k0, width, masked):
        q = q_ref[0, hd]
        k = kv_ref[0, hd, pl.ds(k0, width), :]
        s = _dot_nt(q, k, hi)
        if masked:
            kpos = k0 + lax.broadcasted_iota(I32, (1, width), 1)
            vis = jnp.logical_and(jnp.right_shift(kpos, CHUNK_SHIFT) <= qchunk, kpos < kv_len)
            s = jnp.where(vis, s, NEG_BIG)
        m_prev = m_sc[hd]
        m_new = jnp.maximum(m_prev, jnp.max(s, axis=-1, keepdims=True))
        alpha = jnp.exp2(m_prev - m_new)
        p = jnp.exp2(s - jnp.tile(m_new, (1, width // LANES)))
        l_sc[hd] = alpha * l_sc[hd] + jnp.sum(p, axis=-1, keepdims=True)
        acc_sc[hd] = alpha * acc_sc[hd] + _dot(p, k[:, :LANES], hi)
        m_sc[hd] = m_new

    m_sc[...] = jnp.full((2, tq, LANES), NEG_BIG, F32)
    l_sc[...] = jnp.zeros((2, tq, LANES), F32)
    acc_sc[...] = jnp.zeros((2, tq, LANES), F32)

    def full_body(j, carry):
        for hd in range(2):
            block(hd, pl.multiple_of(j * tk, tk), tk, False)
        return carry

    def masked_body(j, carry):
        for hd in range(2):
            block(hd, pl.multiple_of(j * tkm, tkm), tkm, True)
        return carry

    lax.fori_loop(0, n_full, full_body, 0)
    lax.fori_loop(m_lo, m_hi, masked_body, 0)
    outs = [acc_sc[hd] / l_sc[hd] for hd in range(2)]
    lane = lax.broadcasted_iota(I32, (tq, LANES), 1)
    o = jnp.where(lane < MLA_V, pltpu.roll(outs[0], MLA_V, 1), outs[1])
    o_ref[0] = o.astype(o_ref.dtype)


def _attention(q, kv, tq, tk, tkm, q_off, kv_len, hi):
    B, H, L, _ = q.shape
    Lk = kv.shape[2]
    return pl.pallas_call(
        functools.partial(_attn_kernel, tq=tq, tk=tk, tkm=tkm, q_off=q_off, kv_len=kv_len, hi=hi),
        grid=(B, H // 2, L // tq),
        in_specs=[pl.BlockSpec((1, 2, tq, 2 * LANES), lambda b, h, i: (b, h, i, 0)),
                  pl.BlockSpec((1, 2, Lk, 2 * LANES), lambda b, h, i: (b, h, 0, 0))],
        out_specs=pl.BlockSpec((1, tq, LANES), lambda b, h, i: (b, i, h)),
        out_shape=jax.ShapeDtypeStruct((B, L, H * MLA_V), _act_dtype(hi)),
        scratch_shapes=[pltpu.VMEM((2, tq, LANES), F32)] * 3,
        compiler_params=_cparams(("parallel", "parallel", "parallel")),
        name="mla_attention",
    )(q, kv)


def _attn_latent_kernel(q_ref, ckv_ref, kpe_ref, wuk_ref, wuv_ref, o_ref, *, L, q_off, kv_len, hi):
    H = MLA_HEADS
    lk = ckv_ref.shape[1]
    ckv = ckv_ref[0]
    kpe = kpe_ref[0]
    qa = jnp.concatenate([_dot(q_ref[0, h][:, :LANES], wuk_ref[h], hi) for h in range(H)], axis=0)
    qp = jnp.concatenate([q_ref[0, h][:, LANES:] for h in range(H)], axis=0)
    s = _dot_nt(qa, ckv, hi) + _dot_nt(qp, kpe, hi)
    row = lax.broadcasted_iota(I32, (H * L, 1), 0)
    qchunk = jnp.right_shift(q_off + jnp.bitwise_and(row, L - 1), CHUNK_SHIFT)
    kpos = lax.broadcasted_iota(I32, (1, lk), 1)
    vis = jnp.logical_and(jnp.right_shift(kpos, CHUNK_SHIFT) <= qchunk, kpos < kv_len)
    s = jnp.where(vis, s, NEG_BIG)
    p = jnp.exp2(s - jnp.max(s, axis=-1, keepdims=True))
    lat = _dot(p, ckv, hi) / jnp.sum(p, axis=-1, keepdims=True)
    outs = [_dot(lat[h * L:(h + 1) * L], wuv_ref[h], hi)[:, :MLA_V] for h in range(H)]
    o_ref[0] = jnp.concatenate(outs, axis=-1).astype(o_ref.dtype)


def _attention_latent(q, ckv, kpe, wuk, wuv, q_off, kv_len, hi):
    B, H, L, _ = q.shape
    lk = ckv.shape[1]
    return pl.pallas_call(
        functools.partial(_attn_latent_kernel, L=L, q_off=q_off, kv_len=kv_len, hi=hi),
        grid=(B,),
        in_specs=[pl.BlockSpec((1, H, L, 2 * LANES), lambda b: (b, 0, 0, 0)),
                  pl.BlockSpec((1, lk, MLA_KV_RANK), lambda b: (b, 0, 0)),
                  pl.BlockSpec((1, lk, LANES), lambda b: (b, 0, 0)),
                  pl.BlockSpec((H, LANES, MLA_KV_RANK), lambda b: (0, 0, 0)),
                  pl.BlockSpec((H, MLA_KV_RANK, LANES), lambda b: (0, 0, 0))],
        out_specs=pl.BlockSpec((1, L, H * MLA_V), lambda b: (b, 0, 0)),
        out_shape=jax.ShapeDtypeStruct((B, L, H * MLA_V), _act_dtype(hi)),
        compiler_params=_cparams(("parallel",)),
        name="mla_attention_latent",
    )(q, ckv, kpe, wuk, wuv)


def _gdn_kernel(qkv_ref, gate_ref, ba_ref, cst_ref, s0_ref, cw_ref, dtb_ref, alog_ref, nrm_ref,
                o_ref, sout_ref, cout_ref, state_sc, cbuf, *, bb_n, **kw):
    for b in range(bb_n):
        _gdn_stream(qkv_ref.at[b], gate_ref.at[b], ba_ref.at[b], cst_ref.at[b], s0_ref.at[b], cw_ref, dtb_ref,
                    alog_ref, nrm_ref, o_ref.at[b], sout_ref.at[b], cout_ref.at[b], state_sc.at[b], cbuf.at[b], **kw)


def _gdn_stream(qkv_ref, gate_ref, ba_ref, cst_ref, s0_ref, cw_ref, dtb_ref, alog_ref, nrm_ref,
                o_ref, sout_ref, cout_ref, state_sc, cbuf, *, T, n_valid, hi, inv_mode):
    c = pl.program_id(1)
    nc = pl.num_programs(1)
    halo = GDN_CONV - 1
    base = SUBLANES - halo
    H = GDN_HEADS
    DK = GDN_DK

    @pl.when(c == 0)
    def _():
        state_sc[...] = s0_ref[...]
        cbuf[base:SUBLANES, :] = cst_ref[...]

    cbuf[SUBLANES:SUBLANES + T, :] = qkv_ref[...]
    conv = cw_ref[0:1, :] * cbuf[base:base + T, :]
    for k in range(1, GDN_CONV):
        conv = conv + cw_ref[k:k + 1, :] * cbuf[base + k:base + k + T, :]
    qkv = _silu(conv)

    c_last = (n_valid - 1) // T
    nv_last = n_valid - c_last * T

    @pl.when(c == c_last)
    def _():
        cout_ref[...] = cbuf[base + nv_last:base + nv_last + halo, :]

    cbuf[base:SUBLANES, :] = cbuf[base + T:SUBLANES + T, :]

    tok = lax.broadcasted_iota(I32, (T, 1), 0) + c * T
    valid = tok < n_valid
    ba = ba_ref[...]
    beta = jnp.where(valid, _sigmoid(ba), 0.0)
    g = jnp.where(valid, -jnp.exp(alog_ref[...]) * _softplus(ba + dtb_ref[...]), 0.0)
    ri = lax.broadcasted_iota(I32, (T, T), 0)
    ci = lax.broadcasted_iota(I32, (T, T), 1)
    tril = jnp.where(ci <= ri, 1.0, 0.0).astype(F32)
    gc = _dot_hi(tril, g)

    def l2n(x):
        return x * lax.rsqrt(jnp.sum(x * x, axis=-1, keepdims=True) + EPS)

    qn = [l2n(qkv[:, h * DK:(h + 1) * DK]) * (DK ** -0.5) for h in range(H)]
    kn = [l2n(qkv[:, (H + h) * DK:(H + h + 1) * DK]) for h in range(H)]
    vv = [qkv[:, (2 * H + h) * DK:(2 * H + h + 1) * DK] for h in range(H)]

    G = 4
    W = G * T
    bi = lax.broadcasted_iota(I32, (W, W), 0)
    bj = lax.broadcasted_iota(I32, (W, W), 1)
    t_shift = T.bit_length() - 1
    same = jnp.right_shift(bi, t_shift) == jnp.right_shift(bj, t_shift)
    strict = jnp.logical_and(same, bj < bi)
    incl = jnp.logical_and(same, bj <= bi)
    eye_w = jnp.where(bi == bj, 1.0, 0.0).astype(F32)
    rowhead = jnp.right_shift(lax.broadcasted_iota(I32, (W, DK), 0), t_shift)

    o_heads = [None] * H
    for grp in range(H // G):
        hs = [grp * G + j for j in range(G)]
        kst = jnp.concatenate([kn[h] for h in hs], axis=0)
        qst = jnp.concatenate([qn[h] for h in hs], axis=0)
        vst = jnp.concatenate([vv[h] for h in hs], axis=0)
        cb = jnp.concatenate([jnp.broadcast_to(gc[:, H + h:H + h + 1], (T, W)) for h in hs], axis=0)
        bb = jnp.concatenate([jnp.broadcast_to(beta[:, h:h + 1], (T, DK)) for h in hs], axis=0)
        glast = jnp.concatenate([jnp.broadcast_to(gc[T - 1:T, H + h:H + h + 1], (T, DK)) for h in hs], axis=0)
        dec = jnp.exp(jnp.minimum(cb - jnp.transpose(cb), 0.0))
        kk = _dot_nt(kst, kst, hi)
        qk = _dot_nt(qst, kst, hi)
        bbw = jnp.concatenate([bb, bb], axis=-1)
        a_mat = jnp.where(strict, bbw * kk * dec, 0.0)
        qkm = jnp.where(incl, qk * dec, 0.0)
        p_mat = eye_w - a_mat
        a_pow = a_mat
        n_sq = max(1, int(math.ceil(math.log2(T))) - 1)
        for it in range(n_sq):
            a_pow = _dot_mode(a_pow, a_pow, inv_mode)
            p_mat = p_mat + _dot_mode(p_mat, a_pow, inv_mode)
        cbk = cb[:, :DK]
        eg = jnp.exp(cbk)
        rhs = jnp.concatenate([vst * bb, kst * bb * eg], axis=-1)
        sol = _dot_mode(p_mat, rhs, inv_mode)
        u_st = sol[:, :DK]
        w_st = sol[:, DK:]
        qdec = qst * eg
        kdec_t = jnp.transpose(kst * jnp.exp(glast - cbk))
        vnew = []
        ooff = []
        for j, h in enumerate(hs):
            s_h = state_sc[h * DK:(h + 1) * DK, :]
            lhs = jnp.concatenate([w_st[j * T:(j + 1) * T], qdec[j * T:(j + 1) * T]], axis=0)
            ws = _dot(lhs, s_h, hi)
            vnew.append(u_st[j * T:(j + 1) * T] - ws[:T])
            ooff.append(ws[T:])
        vnew_st = jnp.concatenate(vnew, axis=0)
        o_st = jnp.concatenate(ooff, axis=0) + _dot(qkm, vnew_st, hi)
        for j, h in enumerate(hs):
            s_h = state_sc[h * DK:(h + 1) * DK, :]
            vm = jnp.where(rowhead == j, vnew_st, 0.0)
            last = jnp.exp(jnp.broadcast_to(gc[T - 1:T, H + h:H + h + 1], (DK, GDN_DV)))
            state_sc[h * DK:(h + 1) * DK, :] = s_h * last + _dot(kdec_t, vm, hi)
            o_heads[h] = o_st[j * T:(j + 1) * T]
    gate = gate_ref[...]
    outs = []
    for h in range(H):
        oh = o_heads[h]
        oh = oh * lax.rsqrt(jnp.mean(oh * oh, axis=-1, keepdims=True) + EPS) * nrm_ref[...]
        outs.append(oh * _silu(gate[:, h * GDN_DV:(h + 1) * GDN_DV]))
    o_ref[...] = jnp.concatenate(outs, axis=-1).astype(o_ref.dtype)

    @pl.when(c == nc - 1)
    def _():
        sout_ref[...] = state_sc[...]


def _gdn(qkv, gate, ba, cst, s0, cw, dtb, alog, nrm, T, n_valid, hi, inv_mode):
    B, Lp, _ = qkv.shape
    hk = GDN_HEADS * GDN_DK
    bb_n = 1
    row = lambda b, c: (0, 0)
    return pl.pallas_call(
        functools.partial(_gdn_kernel, bb_n=bb_n, T=T, n_valid=n_valid, hi=hi, inv_mode=inv_mode),
        grid=(B // bb_n, Lp // T),
        in_specs=[pl.BlockSpec((bb_n, T, GDN_QKV), lambda b, c: (b, c, 0)),
                  pl.BlockSpec((bb_n, T, GDN_HEADS * GDN_DV), lambda b, c: (b, c, 0)),
                  pl.BlockSpec((bb_n, T, LANES), lambda b, c: (b, c, 0)),
                  pl.BlockSpec((bb_n, GDN_CONV - 1, GDN_QKV), lambda b, c: (b, 0, 0)),
                  pl.BlockSpec((bb_n, hk, GDN_DV), lambda b, c: (b, 0, 0)),
                  pl.BlockSpec((GDN_CONV, GDN_QKV), row),
                  pl.BlockSpec((1, LANES), row),
                  pl.BlockSpec((1, LANES), row),
                  pl.BlockSpec((1, GDN_DV), row)],
        out_specs=[pl.BlockSpec((bb_n, T, GDN_HEADS * GDN_DV), lambda b, c: (b, c, 0)),
                   pl.BlockSpec((bb_n, hk, GDN_DV), lambda b, c: (b, 0, 0)),
                   pl.BlockSpec((bb_n, GDN_CONV - 1, GDN_QKV), lambda b, c: (b, 0, 0))],
        out_shape=[jax.ShapeDtypeStruct((B, Lp, GDN_HEADS * GDN_DV), _act_dtype(hi)),
                   jax.ShapeDtypeStruct((B, hk, GDN_DV), F32),
                   jax.ShapeDtypeStruct((B, GDN_CONV - 1, GDN_QKV), F32)],
        scratch_shapes=[pltpu.VMEM((bb_n, hk, GDN_DV), F32),
                        pltpu.VMEM((bb_n, T + SUBLANES, GDN_QKV), F32)],
        compiler_params=_cparams(("parallel", "arbitrary")),
        name="gdn_scan",
    )(qkv, gate, ba, cst, s0, cw, dtb, alog, nrm)


def _sconv_kernel(b_ref, c_ref, v_ref, cst_ref, w_ref, o_ref, cout_ref, cbuf, *, T):
    i = pl.program_id(1)
    halo = SC_CONV - 1
    base = SUBLANES - halo

    @pl.when(i == 0)
    def _():
        cbuf[base:SUBLANES, :] = cst_ref[0]

    cbuf[SUBLANES:SUBLANES + T, :] = c_ref[0] * v_ref[0]
    conv = w_ref[0:1, :] * cbuf[base:base + T, :]
    for k in range(1, SC_CONV):
        conv = conv + w_ref[k:k + 1, :] * cbuf[base + k:base + k + T, :]
    o_ref[0] = (b_ref[0] * conv).astype(o_ref.dtype)
    cout_ref[0] = cbuf[base + T:SUBLANES + T, :]
    cbuf[base:SUBLANES, :] = cbuf[base + T:SUBLANES + T, :]


def _sconv(scb, scc, scv, cst, w, T, hi):
    B, L, _ = scb.shape
    T = min(T, L)
    blk = pl.BlockSpec((1, T, SC_WIDTH), lambda b, i: (b, i, 0))
    st = pl.BlockSpec((1, SC_CONV - 1, SC_WIDTH), lambda b, i: (b, 0, 0))
    return pl.pallas_call(
        functools.partial(_sconv_kernel, T=T),
        grid=(B, L // T),
        in_specs=[blk, blk, blk, st, pl.BlockSpec((SC_CONV, SC_WIDTH), lambda b, i: (0, 0))],
        out_specs=[blk, st],
        out_shape=[jax.ShapeDtypeStruct((B, L, SC_WIDTH), _act_dtype(hi)),
                   jax.ShapeDtypeStruct((B, SC_CONV - 1, SC_WIDTH), F32)],
        scratch_shapes=[pltpu.VMEM((T + SUBLANES, SC_WIDTH), F32)],
        compiler_params=_cparams(("parallel", "arbitrary")),
        name="short_conv",
    )(scb, scc, scv, cst, w)


def _out_kernel(y_ref, o_ref, x_ref, mod_ref, wy_ref, wo_ref, g_ref, wr_ref, br_ref,
                xn_ref, h_ref, lg_ref, *, hi):
    mix = _dot(y_ref[0], wy_ref[...], hi) + _dot(o_ref[0], wo_ref[...], hi)
    xn = x_ref[0] + mod_ref[0, 2:3, :] * mix
    xn_ref[0] = xn
    h = xn * lax.rsqrt(jnp.mean(xn * xn, axis=-1, keepdims=True) + EPS) * g_ref[...]
    h = h * (1.0 + mod_ref[0, 4:5, :]) + mod_ref[0, 3:4, :]
    h_ref[0] = h
    lg_ref[0] = _dot_split(h, wr_ref[...]) + br_ref[...]


def _out_proj(y, o, x, mod, wy, wo, g, wr, br, tm, hi):
    B, L, _ = x.shape
    tm = min(tm, L)
    blk = lambda dt_w: pl.BlockSpec((1, tm, dt_w), lambda b, i: (b, i, 0))
    full = lambda s: pl.BlockSpec(s, lambda b, i: (0, 0))
    return pl.pallas_call(
        functools.partial(_out_kernel, hi=hi),
        grid=(B, L // tm),
        in_specs=[blk(D_MODEL), blk(D_MODEL), blk(D_MODEL),
                  pl.BlockSpec((1, 6, D_MODEL), lambda b, i: (b, 0, 0)),
                  full((D_MODEL, D_MODEL)), full((D_MODEL, D_MODEL)), full((1, D_MODEL)),
                  full((D_MODEL, LANES)), full((1, LANES))],
        out_specs=[blk(D_MODEL), blk(D_MODEL), blk(LANES)],
        out_shape=[jax.ShapeDtypeStruct((B, L, D_MODEL), F32),
                   jax.ShapeDtypeStruct((B, L, D_MODEL), F32),
                   jax.ShapeDtypeStruct((B, L, LANES), F32)],
        compiler_params=_cparams(("parallel", "parallel")),
        name="out_proj",
    )(y, o, x, mod, wy, wo, g, wr, br)


def _route_kernel(lg_ref, e_ref, rank_ref, gate_ref, cnt_ref, base_sc, *, tm):
    i = pl.program_id(0)

    @pl.when(i == 0)
    def _():
        base_sc[...] = jnp.zeros_like(base_sc)

    lg = lg_ref[...]
    lane_i = lax.broadcasted_iota(I32, (tm, LANES), 1)
    lane = lane_i.astype(F32)
    vals, idxs = [], []
    cur = lg
    for _ in range(TOP_K):
        m = jnp.max(cur, axis=-1, keepdims=True)
        idx = jnp.min(jnp.where(cur == m, lane, float(LANES)), axis=-1, keepdims=True)
        vals.append(m)
        idxs.append(idx)
        cur = jnp.where(lane == idx, -jnp.inf, cur)
    ex = [jnp.exp(v - vals[0]) for v in vals]
    den = ex[0] + ex[1] + ex[2] + ex[3]
    onehot = jnp.zeros((tm, LANES), F32)
    for idx in idxs:
        onehot = onehot + jnp.where(lane == idx, 1.0, 0.0)
    ri = lax.broadcasted_iota(I32, (tm, tm), 0)
    ci = lax.broadcasted_iota(I32, (tm, tm), 1)
    before = _dot(jnp.where(ci < ri, 1.0, 0.0), onehot) + base_sc[...]
    e_out = jnp.zeros((tm, LANES), I32)
    r_out = jnp.zeros((tm, LANES), I32)
    g_out = jnp.zeros((tm, LANES), F32)
    for k in range(TOP_K):
        rk = jnp.sum(jnp.where(lane == idxs[k], before, 0.0), axis=-1, keepdims=True)
        e_out = jnp.where(lane_i == k, idxs[k].astype(I32), e_out)
        r_out = jnp.where(lane_i == k, rk.astype(I32), r_out)
        g_out = jnp.where(lane_i == k, ex[k] / den, g_out)
    e_ref[...] = e_out
    rank_ref[...] = r_out
    gate_ref[...] = g_out
    base_sc[...] = base_sc[...] + jnp.sum(onehot, axis=0, keepdims=True)
    cnt_ref[...] = base_sc[...].astype(I32)


def _route(logits, tm):
    n_tok = logits.shape[0]
    tm = min(tm, n_tok)
    blk = pl.BlockSpec((tm, LANES), lambda i: (i, 0))
    return pl.pallas_call(
        functools.partial(_route_kernel, tm=tm),
        grid=(n_tok // tm,),
        in_specs=[blk],
        out_specs=[blk, blk, blk, pl.BlockSpec((1, LANES), lambda i: (0, 0))],
        out_shape=[jax.ShapeDtypeStruct((n_tok, LANES), I32),
                   jax.ShapeDtypeStruct((n_tok, LANES), I32),
                   jax.ShapeDtypeStruct((n_tok, LANES), F32),
                   jax.ShapeDtypeStruct((1, LANES), I32)],
        scratch_shapes=[pltpu.VMEM((1, LANES), F32)],
        compiler_params=_cparams(("arbitrary",)),
        name="moe_route",
    )(logits)


def _dispatch_kernel(zs_ref, nz_ref, dest_ref, h_ref, out_hbm, zbuf, sem, zsem, *, tm, n_blocks):
    def zero_copy(row0):
        return pltpu.make_async_copy(zbuf, out_hbm.at[pl.ds(row0, MOE_BM), :], zsem)

    @pl.when(pl.program_id(0) == 0)
    def _():
        zbuf[...] = jnp.zeros_like(zbuf)
        n_used = nz_ref[0]
        for e in range(N_EXPERTS):
            @pl.when(zs_ref[e] >= 0)
            def _(e=e):
                zero_copy(pl.multiple_of(zs_ref[e], MOE_BM)).start()

        def tail(j, carry):
            zero_copy(pl.multiple_of(j * MOE_BM, MOE_BM)).start()
            return carry

        lax.fori_loop(n_used, n_blocks, tail, 0)

        def drain(j, carry):
            zero_copy(0).wait()
            return carry

        lax.fori_loop(0, nz_ref[1], drain, 0)

    def issue(r, carry):
        for k in range(TOP_K):
            d = dest_ref[r * TOP_K + k]
            pltpu.make_async_copy(h_ref.at[pl.ds(r, 1), :], out_hbm.at[pl.ds(d, 1), :], sem).start()
        return carry

    lax.fori_loop(0, tm, issue, 0)
    for _ in range(TOP_K):
        pltpu.make_async_copy(h_ref, out_hbm.at[pl.ds(0, tm), :], sem).wait()


def _dispatch(h, dest_flat, zero_start, zero_counts, n_rows, tm):
    n_tok = h.shape[0]
    tm = min(tm, n_tok)
    grid_spec = pltpu.PrefetchScalarGridSpec(
        num_scalar_prefetch=2,
        grid=(n_tok // tm,),
        in_specs=[pl.BlockSpec((tm * TOP_K,), lambda i, zs, nz: (i,), memory_space=pltpu.SMEM),
                  pl.BlockSpec((tm, D_MODEL), lambda i, zs, nz: (i, 0))],
        out_specs=pl.BlockSpec(memory_space=pl.ANY),
        scratch_shapes=[pltpu.VMEM((MOE_BM, D_MODEL), F32), pltpu.SemaphoreType.DMA(()),
                        pltpu.SemaphoreType.DMA(())],
    )
    return pl.pallas_call(
        functools.partial(_dispatch_kernel, tm=tm, n_blocks=n_rows // MOE_BM),
        grid_spec=grid_spec,
        out_shape=jax.ShapeDtypeStruct((n_rows, D_MODEL), F32),
        compiler_params=_cparams(("arbitrary",)),
        name="moe_dispatch",
    )(zero_start, zero_counts, dest_flat, h)


def _ffn_kernel(blk_e_ref, nused_ref, x_ref, wgu_ref, bgu_ref, wd_ref, bd_ref, o_ref, wgu_sc, wd_sc):
    i = pl.program_id(0)
    prev = blk_e_ref[jnp.maximum(i - 1, 0)]
    fresh = jnp.logical_or(i == 0, blk_e_ref[i] != prev)
    active = i < nused_ref[0]

    @pl.when(jnp.logical_and(active, fresh))
    def _():
        wgu_sc[...] = wgu_ref[0, 0].astype(BF16)
        wd_sc[...] = wd_ref[0, 0].astype(BF16)

    @pl.when(active)
    def _():
        gu = jnp.dot(x_ref[...].astype(BF16), wgu_sc[...], preferred_element_type=F32) + bgu_ref[0]
        gate = jnp.minimum(gu[:, :D_FF], SWIGLU_LIMIT)
        up = jnp.clip(gu[:, D_FF:], -SWIGLU_LIMIT, SWIGLU_LIMIT)
        act = (up + 1.0) * gate * _sigmoid(SWIGLU_ALPHA * gate)
        o_ref[...] = jnp.dot(act.astype(BF16), wd_sc[...], preferred_element_type=F32) + bd_ref[0]

    @pl.when(jnp.logical_not(active))
    def _():
        o_ref[...] = jnp.zeros_like(o_ref)


def _expert_ffn(xin, blk_e, n_used, w_gu, b_gu, w_down, b_down, layer):
    n_rows = xin.shape[0]
    n_blocks = n_rows // MOE_BM

    def row_map(i, be, nu):
        return (jnp.minimum(i, nu[0] - 1), 0)

    def e_map4(i, be, nu):
        return (layer, be[jnp.minimum(i, nu[0] - 1)], 0, 0)

    def e_map3(i, be, nu):
        return (layer * N_EXPERTS + be[jnp.minimum(i, nu[0] - 1)], 0, 0)

    grid_spec = pltpu.PrefetchScalarGridSpec(
        num_scalar_prefetch=2,
        grid=(n_blocks,),
        in_specs=[pl.BlockSpec((MOE_BM, D_MODEL), row_map),
                  pl.BlockSpec((1, 1, D_MODEL, 2 * D_FF), e_map4),
                  pl.BlockSpec((1, 1, 2 * D_FF), e_map3),
                  pl.BlockSpec((1, 1, D_FF, D_MODEL), e_map4),
                  pl.BlockSpec((1, 1, D_MODEL), e_map3)],
        out_specs=pl.BlockSpec((MOE_BM, D_MODEL), lambda i, be, nu: (i, 0)),
        scratch_shapes=[pltpu.VMEM((D_MODEL, 2 * D_FF), BF16),
                        pltpu.VMEM((D_FF, D_MODEL), BF16)],
    )
    return pl.pallas_call(
        _ffn_kernel,
        grid_spec=grid_spec,
        out_shape=jax.ShapeDtypeStruct((n_rows, D_MODEL), F32),
        compiler_params=_cparams(("arbitrary",)),
        name="moe_expert_ffn",
    )(blk_e, n_used, xin, w_gu, b_gu.reshape(DEPTH * N_EXPERTS, 1, 2 * D_FF), w_down,
      b_down.reshape(DEPTH * N_EXPERTS, 1, D_MODEL))


def _combine_kernel(dest_ref, f_hbm, gate_ref, x_ref, mod_ref, g_ref, o_ref, buf, sem, *, tm, final):
    def issue(r, carry):
        for k in range(TOP_K):
            d = dest_ref[r * TOP_K + k]
            pltpu.make_async_copy(f_hbm.at[pl.ds(d, 1), :], buf.at[k, pl.ds(r, 1), :], sem).start()
        return carry

    lax.fori_loop(0, tm, issue, 0)
    for k in range(TOP_K):
        pltpu.make_async_copy(f_hbm.at[pl.ds(0, tm), :], buf.at[k], sem).wait()
    gates = gate_ref[...]
    moe = gates[:, 0:1] * buf[0]
    for k in range(1, TOP_K):
        moe = moe + gates[:, k:k + 1] * buf[k]
    xo = x_ref[...] + mod_ref[0, 5:6, :] * moe
    if final:
        xo = xo * lax.rsqrt(jnp.mean(xo * xo, axis=-1, keepdims=True) + EPS) * g_ref[...]
    o_ref[...] = xo


def _combine(ffn_out, dest_flat, gates, x, mod, g_final, tiles_per_batch, tm, final):
    n_tok = x.shape[0]
    return pl.pallas_call(
        functools.partial(_combine_kernel, tm=tm, final=final),
        grid=(n_tok // tm,),
        in_specs=[pl.BlockSpec((tm * TOP_K,), lambda i: (i,), memory_space=pltpu.SMEM),
                  pl.BlockSpec(memory_space=pl.ANY),
                  pl.BlockSpec((tm, LANES), lambda i: (i, 0)),
                  pl.BlockSpec((tm, D_MODEL), lambda i: (i, 0)),
                  pl.BlockSpec((1, 6, D_MODEL), lambda i: (i // tiles_per_batch, 0, 0)),
                  pl.BlockSpec((1, D_MODEL), lambda i: (0, 0))],
        out_specs=pl.BlockSpec((tm, D_MODEL), lambda i: (i, 0)),
        out_shape=jax.ShapeDtypeStruct((n_tok, D_MODEL), F32),
        scratch_shapes=[pltpu.VMEM((TOP_K, tm, D_MODEL), F32), pltpu.SemaphoreType.DMA(())],
        compiler_params=_cparams(("arbitrary",)),
        name="moe_combine",
    )(dest_flat, ffn_out, gates, x, mod, g_final)


def _pad_cols(w, width):
    return jnp.pad(w, ((0, 0), (0, width - w.shape[1])))


def _pad_lanes(v, offset=0):
    return jnp.pad(v.astype(F32), (offset, LANES - offset - v.shape[0])).reshape(1, LANES)


def _rope_tables(pos):
    half = MLA_ROPE // 2
    inv = ROPE_THETA ** (-jnp.arange(half, dtype=F32) / half)
    ang = pos.astype(F32)[:, None] * inv[None, :]
    cos, sin = jnp.cos(ang), jnp.sin(ang)
    z = jnp.zeros_like(cos)
    pad = jnp.zeros((pos.shape[0], LANES - MLA_ROPE), F32)
    cos_t = jnp.concatenate([cos, cos, pad], axis=1)
    sin_a = jnp.concatenate([-sin, z, pad], axis=1)
    sin_b = jnp.concatenate([z, sin, pad], axis=1)
    return cos_t, sin_a, sin_b


def _pad_seq(t, lp):
    return jnp.pad(t, ((0, 0), (0, lp - t.shape[1]), (0, 0)))


def _even_weights(W, j, dt):
    w_in = W['ev_w_in'][j]
    o1 = SSD_D_INNER
    o2 = o1 + SSD_XBC
    o3 = o2 + SSD_HEADS
    o4 = o3 + MLA_Q_RANK
    w_ssd = jnp.concatenate([w_in[:, :o2], _pad_cols(w_in[:, o2:o3], LANES)], axis=1).astype(dt)
    w_mla = jnp.concatenate([w_in[:, o3:o4], _pad_cols(w_in[:, o4:], MLA_Q_RANK)], axis=1).astype(dt)
    wq = W['mla_w_q_up'][j].reshape(MLA_Q_RANK, MLA_HEADS, MLA_NOPE + MLA_ROPE)
    wq = jnp.concatenate([wq[..., :MLA_NOPE], jnp.zeros((MLA_Q_RANK, MLA_HEADS, LANES - MLA_NOPE), F32),
                          wq[..., MLA_NOPE:], jnp.zeros((MLA_Q_RANK, MLA_HEADS, LANES - MLA_ROPE), F32)], axis=-1)
    wq = jnp.transpose(wq, (1, 0, 2)).astype(dt)
    wkv = jnp.transpose(W['mla_w_kv_up'][j].reshape(MLA_KV_RANK, MLA_HEADS, MLA_NOPE + MLA_V),
                        (1, 0, 2)).astype(dt)
    expand = (jnp.arange(LANES)[:, None] == (jnp.arange(SSD_D_INNER) // SSD_HEADDIM)[None, :]).astype(F32)
    wuk = jnp.pad(jnp.transpose(wkv[:, :, :MLA_NOPE], (0, 2, 1)), ((0, 0), (0, LANES - MLA_NOPE), (0, 0)))
    wuv = jnp.pad(wkv[:, :, MLA_NOPE:], ((0, 0), (0, 0), (0, LANES - MLA_V)))
    return dict(
        w_ssd=w_ssd, w_mla=w_mla, wq=wq, wkv=wkv, wuk=wuk, wuv=wuv, expand=expand,
        wy=W['ev_w_out'][j][:SSD_D_INNER].astype(dt), wo=W['ev_w_out'][j][SSD_D_INNER:].astype(dt),
        cw=W['ssd_conv_w'][j], cb=W['ssd_conv_b'][j].reshape(1, SSD_XBC),
        dtb=_pad_lanes(W['ssd_dt_bias'][j]), alog=_pad_lanes(W['ssd_a_log'][j]),
        dsk=jnp.repeat(W['ssd_d'][j].astype(F32), SSD_HEADDIM).reshape(1, SSD_D_INNER),
        nrm=W['ssd_norm'][j].reshape(1, SSD_D_INNER),
        qn=W['mla_q_norm'][j].reshape(1, MLA_Q_RANK), kvn=W['mla_kv_norm'][j].reshape(1, MLA_KV_RANK))


def _odd_weights(W, j, dt):
    w_in = W['od_w_in'][j]
    o1 = GDN_QKV
    o2 = o1 + GDN_HEADS * GDN_DV
    o3 = o2 + 2 * GDN_HEADS
    w_gdn = jnp.concatenate([w_in[:, :o2], _pad_cols(w_in[:, o2:o3], LANES)], axis=1).astype(dt)
    w_sc = w_in[:, o3:].astype(dt)
    return dict(
        w_gdn=w_gdn, w_sc=w_sc,
        wy=W['od_w_out'][j][:GDN_HEADS * GDN_DV].astype(dt), wo=W['od_w_out'][j][GDN_HEADS * GDN_DV:].astype(dt),
        cw=W['gdn_conv_w'][j], dtb=_pad_lanes(W['gdn_dt_bias'][j], GDN_HEADS),
        alog=_pad_lanes(W['gdn_a_log'][j], GDN_HEADS), nrm=W['gdn_norm'][j].reshape(1, GDN_DV),
        scw=W['sconv_w'][j])


def _moe(h, logits, x_new, mod, W, i, g_final, L, final):
    n_tok = h.shape[0]
    e_pad, rank_pad, gates, cnt = _route(logits, 256)
    counts = cnt[0, :N_EXPERTS]
    padded = (counts + MOE_BM - 1) // MOE_BM * MOE_BM
    pend = jnp.cumsum(padded)
    pstart = pend - padded
    e_sel = e_pad[:, :TOP_K]
    dest = (pstart[e_sel] + rank_pad[:, :TOP_K]).astype(I32).reshape(-1)
    n_blocks = n_tok * TOP_K // MOE_BM + N_EXPERTS
    n_rows = n_blocks * MOE_BM
    blk_start = jnp.arange(n_blocks, dtype=pend.dtype) * MOE_BM
    blk_e = jnp.minimum(jnp.sum((blk_start[:, None] >= pend[None, :]).astype(I32), axis=1), N_EXPERTS - 1)
    n_used = (pend[-1:] // MOE_BM).astype(I32)
    zero_start = jnp.where(counts > 0, pend - MOE_BM, -1).astype(I32)
    n_zero = jnp.sum((counts > 0).astype(I32)) + n_blocks - n_used[0]
    zero_counts = jnp.stack([n_used[0], n_zero]).astype(I32)
    tm = min(256, L)
    xin = _dispatch(h, dest, zero_start, zero_counts, n_rows, tm)
    f_out = _expert_ffn(xin, blk_e, n_used, W['w_gu'], W['b_gu'], W['w_down'], W['b_down'], i)
    return _combine(f_out, dest, gates, x_new, mod, g_final, L // tm, tm, final)


def _trunk(x, c_mod, pos0, caches, W, PW, seq_t, hi):
    B, L, _ = x.shape
    new = {}
    pos = pos0 + jnp.arange(L, dtype=I32)
    cos_t, sin_a, sin_b = _rope_tables(pos)
    wr_all = W['w_router']
    for i in range(DEPTH):
        mod = c_mod[i]
        g_mix = W['norm_mix'][i].reshape(1, D_MODEL)
        j = i // 2
        if i % 2 == 0:
            P = PW[i]
            z, xbc, dtr = _in_proj(x, mod, g_mix, P['w_ssd'],
                                   ((0, SSD_D_INNER), (SSD_D_INNER, SSD_XBC), (SSD_D_INNER + SSD_XBC, LANES)),
                                   (F32, F32, F32), 0, 1, 512, hi)
            latq, latkv = _in_proj(x, mod, g_mix, P['w_mla'], ((0, MLA_Q_RANK), (MLA_Q_RANK, MLA_Q_RANK)),
                                   (F32, F32), 0, 1, 512, hi)
            T = seq_t['ssd']
            lp = -(-L // T) * T
            y, s_new, cst_new = _ssd(_pad_seq(z, lp), _pad_seq(xbc, lp), _pad_seq(dtr, lp),
                                     caches['ssd_conv'][j], caches['ssd'][j].reshape(B, -1, SSD_STATE),
                                     P['cw'], P['cb'], P['dtb'], P['alog'], P['dsk'], P['nrm'], P['expand'], T, L, hi)
            y = y[:, :L]
            new['ssd'] = s_new.reshape(1, B, SSD_HEADS, SSD_HEADDIM, SSD_STATE)
            new['ssd_conv'] = cst_new[None]
            ckv_new, kpe_new = _latkv_post(latkv, P['kvn'], cos_t, sin_a, sin_b, 512)
            new['mla_ckv'] = ckv_new[None]
            new['mla_krope'] = kpe_new[None, :, :, :MLA_ROPE]
            ckv_past, kpe_past = caches['mla_ckv'][j], caches['mla_krope'][j]
            past = ckv_past.shape[1]
            kv_len = past + L
            tk = seq_t['tk']
            lk = -(-kv_len // tk) * tk
            ckv_all = _pad_seq(jnp.concatenate([ckv_past, ckv_new], axis=1), lk)
            kpe_all = _pad_seq(jnp.concatenate(
                [jnp.pad(kpe_past, ((0, 0), (0, 0), (0, LANES - MLA_ROPE))), kpe_new], axis=1), lk)
            q = _q_proj(latq, P['qn'], P['wq'], cos_t, sin_a, sin_b, 512, hi)
            if seq_t['latent']:
                o = _attention_latent(q, ckv_all, kpe_all, P['wuk'], P['wuv'], pos0, kv_len, hi)
            else:
                kv = _kv_up(ckv_all, kpe_all, P['wkv'], min(tk, 512), hi)
                o = _attention(q, kv, min(seq_t['tq'], L), tk, seq_t['tkm'], pos0, kv_len, hi)
        else:
            P = PW[i]
            hv = GDN_HEADS * GDN_DV
            qkv, gate, ba = _in_proj(x, mod, g_mix, P['w_gdn'],
                                     ((0, GDN_QKV), (GDN_QKV, hv), (GDN_QKV + hv, LANES)),
                                     (F32, F32, F32), 0, 1, 512, hi)
            scb, scc, scv = _in_proj(x, mod, g_mix, P['w_sc'],
                                     ((0, SC_WIDTH), (SC_WIDTH, SC_WIDTH), (2 * SC_WIDTH, SC_WIDTH)),
                                     (F32, F32, F32), 0, 1, 512, hi)
            T = CHUNK
            lp = -(-L // T) * T
            y, s_new, cst_new = _gdn(_pad_seq(qkv, lp), _pad_seq(gate, lp), _pad_seq(ba, lp),
                                     caches['gdn_conv'][j], caches['gdn'][j].reshape(B, -1, GDN_DV),
                                     P['cw'], P['dtb'], P['alog'], P['nrm'], T, L, hi,
                                     "hi" if hi else seq_t['gdn_inv'])
            y = y[:, :L]
            new['gdn'] = s_new.reshape(1, B, GDN_HEADS, GDN_DK, GDN_DV)
            new['gdn_conv'] = cst_new[None]
            o, sc_new = _sconv(scb, scc, scv, caches['sconv'][j], P['scw'], 512, hi)
            new['sconv'] = sc_new[None]
        wr = _pad_cols(wr_all[i], LANES)
        br = jnp.concatenate([W['b_router'][i].astype(F32), jnp.full((LANES - N_EXPERTS,), NEG_BIG, F32)]).reshape(1, LANES)
        x_new, h, logits = _out_proj(y, o, x, mod, P['wy'], P['wo'], W['norm_ffn'][i].reshape(1, D_MODEL),
                                     wr, br, 512, hi)
        final = i == DEPTH - 1
        xo = _moe(h.reshape(B * L, D_MODEL), logits.reshape(B * L, LANES), x_new.reshape(B * L, D_MODEL),
                  mod, W, i, W['norm_final'].reshape(1, D_MODEL), L, final)
        x = xo.reshape(B, L, D_MODEL)
    return x, new


def _prep_weights(W, dt):
    PW = {}
    for i in range(DEPTH):
        PW[i] = _even_weights(W, i // 2, dt) if i % 2 == 0 else _odd_weights(W, i // 2, dt)
    return PW


def kernel(x_prompt, x_sample, c_prompt, c_sample, cache_mla_ckv, cache_mla_krope, state_ssd, state_ssd_conv, state_gdn, state_gdn_conv, state_sconv, norm_mix, norm_ffn, w_ada, b_ada, w_router, b_router, w_gu, b_gu, w_down, b_down, norm_final, ev_w_in, ev_w_out, ssd_conv_w, ssd_conv_b, ssd_dt_bias, ssd_a_log, ssd_d, ssd_norm, mla_q_norm, mla_w_q_up, mla_kv_norm, mla_w_kv_up, od_w_in, od_w_out, gdn_conv_w, gdn_dt_bias, gdn_a_log, gdn_norm, sconv_w):
    W = dict(norm_mix=norm_mix, norm_ffn=norm_ffn, w_ada=w_ada, b_ada=b_ada, w_router=w_router,
             b_router=b_router, w_gu=w_gu, b_gu=b_gu, w_down=w_down, b_down=b_down, norm_final=norm_final,
             ev_w_in=ev_w_in, ev_w_out=ev_w_out, ssd_conv_w=ssd_conv_w, ssd_conv_b=ssd_conv_b,
             ssd_dt_bias=ssd_dt_bias, ssd_a_log=ssd_a_log, ssd_d=ssd_d, ssd_norm=ssd_norm,
             mla_q_norm=mla_q_norm, mla_w_q_up=mla_w_q_up, mla_kv_norm=mla_kv_norm, mla_w_kv_up=mla_w_kv_up,
             od_w_in=od_w_in, od_w_out=od_w_out, gdn_conv_w=gdn_conv_w, gdn_dt_bias=gdn_dt_bias,
             gdn_a_log=gdn_a_log, gdn_norm=gdn_norm, sconv_w=sconv_w)
    bp, bs = x_prompt.shape[0], x_sample.shape[0]
    nb = 16
    c_all = jnp.concatenate([c_prompt, c_sample, jnp.zeros((nb - bp - bs, D_MODEL), F32)], axis=0)
    mod_all = _ada_mod(c_all, w_ada, b_ada).reshape(DEPTH, nb, 6, D_MODEL)
    n_even, n_odd = (DEPTH + 1) // 2, DEPTH // 2
    zero_caches = dict(
        mla_ckv=jnp.zeros((n_even, bp, 0, MLA_KV_RANK), F32), mla_krope=jnp.zeros((n_even, bp, 0, MLA_ROPE), F32),
        ssd=jnp.zeros((n_even, bp, SSD_HEADS, SSD_HEADDIM, SSD_STATE), F32),
        ssd_conv=jnp.zeros((n_even, bp, SSD_CONV - 1, SSD_XBC), F32),
        gdn=jnp.zeros((n_odd, bp, GDN_HEADS, GDN_DK, GDN_DV), F32),
        gdn_conv=jnp.zeros((n_odd, bp, GDN_CONV - 1, GDN_QKV), F32),
        sconv=jnp.zeros((n_odd, bp, SC_CONV - 1, SC_WIDTH), F32))
    y_p, sp = _trunk(x_prompt, mod_all[:, :bp], 0, zero_caches, W, _prep_weights(W, BF16),
                     dict(ssd=256, tq=512, tk=1024, tkm=512, gdn_inv="split", latent=False), False)
    past = cache_mla_ckv.shape[2]
    caches = dict(mla_ckv=cache_mla_ckv, mla_krope=cache_mla_krope, ssd=state_ssd, ssd_conv=state_ssd_conv,
                  gdn=state_gdn, gdn_conv=state_gdn_conv, sconv=state_sconv)
    y_s, ss = _trunk(x_sample, mod_all[:, bp:bp + bs], past, caches, W, _prep_weights(W, F32),
                     dict(ssd=128, tq=32, tk=256, tkm=256, gdn_inv="split", latent=True), True)
    return (y_p, y_s,
            sp['mla_ckv'], ss['mla_ckv'], sp['mla_krope'], ss['mla_krope'],
            sp['ssd'], ss['ssd'], sp['ssd_conv'], ss['ssd_conv'],
            sp['gdn'], ss['gdn'], sp['gdn_conv'], ss['gdn_conv'],
            sp['sconv'], ss['sconv'])
```

```python
import functools
import math

import jax
import jax.numpy as jnp
from jax import lax
from jax.experimental import pallas as pl
from jax.experimental.pallas import tpu as pltpu

F32 = jnp.float32
BF16 = jnp.bfloat16
I32 = jnp.int32

D_MODEL = 1024
DEPTH = 2
CHUNK = 64
CHUNK_SHIFT = 6
EPS = 1e-6
SSD_D_INNER = D_MODEL
SSD_HEADDIM = 64
SSD_HEADS = SSD_D_INNER // SSD_HEADDIM
SSD_GROUPS = 4
SSD_STATE = 128
SSD_CONV = 4
SSD_XBC = SSD_D_INNER + 2 * SSD_GROUPS * SSD_STATE
MLA_HEADS = 16
MLA_NOPE = 64
MLA_ROPE = 32
MLA_V = 64
MLA_Q_RANK = 384
MLA_KV_RANK = 256
ROPE_THETA = 10000.0
GDN_HEADS = 8
GDN_DK = 128
GDN_DV = 128
GDN_CONV = 4
GDN_QKV = GDN_HEADS * (2 * GDN_DK + GDN_DV)
SC_WIDTH = D_MODEL
SC_CONV = 3
N_EXPERTS = 32
TOP_K = 4
D_FF = D_MODEL
SWIGLU_LIMIT = 7.0
SWIGLU_ALPHA = 1.702

LANES = 128
SUBLANES = 8
VMEM_LIMIT = 56 * 1024 * 1024

NEG_BIG = -1e30
MOE_BM = 512
MOE_BM_SMALL = 128
DISPATCH_TM = 1024


def _cparams(sem):
    return pltpu.CompilerParams(dimension_semantics=sem, vmem_limit_bytes=VMEM_LIMIT)


def _sigmoid(x):
    return 1.0 / (1.0 + jnp.exp(-x))


def _silu(x):
    return x * _sigmoid(x)


def _softplus(x):
    return jnp.maximum(x, 0.0) + jnp.log1p(jnp.exp(-jnp.abs(x)))


_NN = (((1,), (0,)), ((), ()))
_NT = (((1,), (1,)), ((), ()))


def _dot_split(a, b, dims=_NN):
    a_h = a.astype(BF16)
    b_h = b.astype(BF16)
    a_l = (a - a_h.astype(F32)).astype(BF16)
    b_l = (b - b_h.astype(F32)).astype(BF16)
    d = functools.partial(lax.dot_general, dimension_numbers=dims, preferred_element_type=F32)
    return d(a_h, b_h) + d(a_l, b_h) + d(a_h, b_l)


def _dot(a, b, hi=False):
    if hi:
        return _dot_split(a.astype(F32), b.astype(F32))
    return jnp.dot(a.astype(BF16), b.astype(BF16), preferred_element_type=F32)


def _dot_nt(a, b, hi=False):
    if hi:
        return _dot_split(a.astype(F32), b.astype(F32), _NT)
    return lax.dot_general(a.astype(BF16), b.astype(BF16), _NT, preferred_element_type=F32)


def _split3(a):
    a1 = a.astype(BF16)
    r = a - a1.astype(F32)
    a2 = r.astype(BF16)
    a3 = (r - a2.astype(F32)).astype(BF16)
    return a1, a2, a3


def _dot_sel(sel, a, dims=_NN, sel_left=True):
    s = sel.astype(BF16)
    d = functools.partial(lax.dot_general, dimension_numbers=dims, preferred_element_type=F32)
    if sel_left:
        return sum(d(s, p) for p in _split3(a))
    return sum(d(p, s) for p in _split3(a))


def _act_dtype(hi):
    return F32 if hi else BF16


def _rope_rot(p, cos_t, sin_a, sin_b):
    return p * cos_t + pltpu.roll(p, LANES - MLA_ROPE // 2, 1) * sin_a + pltpu.roll(p, MLA_ROPE // 2, 1) * sin_b


def _ada_kernel(c_ref, w_ref, b_ref, o_ref):
    c = c_ref[...]
    o_ref[0] = _dot_split(_silu(c), w_ref[0]) + b_ref[0]


def _ada_mod(c_all, w_ada, b_ada):
    nb = c_all.shape[0]
    return pl.pallas_call(
        _ada_kernel,
        grid=(DEPTH, 6),
        in_specs=[pl.BlockSpec((nb, D_MODEL), lambda i, j: (0, 0)),
                  pl.BlockSpec((1, D_MODEL, D_MODEL), lambda i, j: (i, 0, j)),
                  pl.BlockSpec((1, 1, D_MODEL), lambda i, j: (i, 0, j))],
        out_specs=pl.BlockSpec((1, nb, D_MODEL), lambda i, j: (i, 0, j)),
        out_shape=jax.ShapeDtypeStruct((DEPTH, nb, 6 * D_MODEL), F32),
        compiler_params=_cparams(("parallel", "parallel")),
        name="ada_mod",
    )(c_all, w_ada, b_ada.reshape(DEPTH, 1, 6 * D_MODEL))


def _in_kernel(x_ref, mod_ref, g_ref, w_ref, *out_refs, segs, shift_row, scale_row, hi):
    x = x_ref[0]
    h = x * lax.rsqrt(jnp.mean(x * x, axis=-1, keepdims=True) + EPS) * g_ref[...]
    h = h * (1.0 + mod_ref[0, scale_row:scale_row + 1, :]) + mod_ref[0, shift_row:shift_row + 1, :]
    r = _dot(h, w_ref[...], hi)
    for (off, width), o_ref in zip(segs, out_refs):
        o_ref[0] = r[:, off:off + width].astype(o_ref.dtype)


def _in_proj(x, mod, g, w, segs, dtypes, shift_row, scale_row, tm, hi):
    B, L, _ = x.shape
    n_p = w.shape[1]
    tm = min(tm, L)
    return pl.pallas_call(
        functools.partial(_in_kernel, segs=segs, shift_row=shift_row, scale_row=scale_row, hi=hi),
        grid=(B, L // tm),
        in_specs=[pl.BlockSpec((1, tm, D_MODEL), lambda b, i: (b, i, 0)),
                  pl.BlockSpec((1, 6, D_MODEL), lambda b, i: (b, 0, 0)),
                  pl.BlockSpec((1, D_MODEL), lambda b, i: (0, 0)),
                  pl.BlockSpec((D_MODEL, n_p), lambda b, i: (0, 0))],
        out_specs=[pl.BlockSpec((1, tm, wd), lambda b, i: (b, i, 0)) for _, wd in segs],
        out_shape=[jax.ShapeDtypeStruct((B, L, wd), dt) for (_, wd), dt in zip(segs, dtypes)],
        compiler_params=_cparams(("parallel", "parallel")),
        name="in_proj",
    )(x, mod, g, w)


def _ssd_kernel(z_ref, xbc_ref, dtr_ref, cst_ref, s0_ref, cw_ref, cb_ref, dtb_ref, alog_ref, dsk_ref,
                nrm_ref, e_ref, y_ref, sout_ref, cout_ref, state_sc, cbuf, *, T, n_valid, hi):
    c = pl.program_id(1)
    nc = pl.num_programs(1)
    halo = SSD_CONV - 1
    base = SUBLANES - halo

    @pl.when(c == 0)
    def _():
        state_sc[...] = s0_ref[0]
        cbuf[base:SUBLANES, :] = cst_ref[0]

    cbuf[SUBLANES:SUBLANES + T, :] = xbc_ref[0]
    conv = cb_ref[...] + cw_ref[0:1, :] * cbuf[base:base + T, :]
    for k in range(1, SSD_CONV):
        conv = conv + cw_ref[k:k + 1, :] * cbuf[base + k:base + k + T, :]
    xc = _silu(conv)

    c_last = (n_valid - 1) // T
    nv_last = n_valid - c_last * T

    @pl.when(c == c_last)
    def _():
        cout_ref[0] = cbuf[base + nv_last:base + nv_last + halo, :]

    cbuf[base:SUBLANES, :] = cbuf[base + T:SUBLANES + T, :]

    xs = xc[:, :SSD_D_INNER]
    gn = SSD_GROUPS * SSD_STATE
    bm = xc[:, SSD_D_INNER:SSD_D_INNER + gn]
    cm = xc[:, SSD_D_INNER + gn:]

    tok = lax.broadcasted_iota(I32, (T, 1), 0) + c * T
    dt = jnp.where(tok < n_valid, _softplus(dtr_ref[0] + dtb_ref[...]), 0.0)
    a = dt * (-jnp.exp(alog_ref[...]))
    ri = lax.broadcasted_iota(I32, (T, T), 0)
    ci = lax.broadcasted_iota(I32, (T, T), 1)
    causal = ci <= ri
    tril = jnp.where(causal, 1.0, 0.0).astype(F32)
    acum = _dot_sel(tril, a)
    eye = jnp.where(lax.broadcasted_iota(I32, (LANES, LANES), 0) == lax.broadcasted_iota(I32, (LANES, LANES), 1),
                    1.0, 0.0).astype(F32)
    acum_t = _dot_sel(eye, acum, _NT)
    a_last = acum[T - 1:T, :]
    e = e_ref[...]
    xdt = xs * _dot_sel(e, dt, sel_left=False)
    eacum_x = jnp.exp(_dot_sel(e, acum, sel_left=False))
    xdend = xdt * jnp.exp(_dot_sel(e, a_last - acum, sel_left=False))

    r = SSD_HEADS // SSD_GROUPS
    gw = r * SSD_HEADDIM
    y_groups = []
    for g in range(SSD_GROUPS):
        bg = bm[:, g * SSD_STATE:(g + 1) * SSD_STATE]
        cg = cm[:, g * SSD_STATE:(g + 1) * SSD_STATE]
        cb_mat = _dot_nt(cg, bg, hi)
        ys = []
        for j in range(r):
            h = g * r + j
            seg = acum[:, h:h + 1] - acum_t[h:h + 1, :]
            lm = jnp.where(causal, jnp.exp(jnp.minimum(seg, 0.0)), 0.0)
            ys.append(_dot(cb_mat * lm, xdt[:, h * SSD_HEADDIM:(h + 1) * SSD_HEADDIM], hi))
        y_diag = jnp.concatenate(ys, axis=-1)
        s_g = state_sc[g * gw:(g + 1) * gw, :]
        y_off = _dot_nt(cg, s_g, hi) * eacum_x[:, g * gw:(g + 1) * gw]
        y_groups.append(y_diag + y_off)
        cs = _dot(jnp.transpose(xdend[:, g * gw:(g + 1) * gw]), bg, hi)
        dec = jnp.concatenate(
            [jnp.broadcast_to(jnp.exp(acum_t[g * r + j:g * r + j + 1, T - 1:T]), (SSD_HEADDIM, SSD_STATE))
             for j in range(r)], axis=0)
        state_sc[g * gw:(g + 1) * gw, :] = s_g * dec + cs
    y = jnp.concatenate(y_groups, axis=-1)
    y = y + xs * dsk_ref[...]
    y = y * _silu(z_ref[0])
    outs = []
    for g in range(SSD_GROUPS):
        yg = y[:, g * gw:(g + 1) * gw]
        outs.append(yg * lax.rsqrt(jnp.mean(yg * yg, axis=-1, keepdims=True) + EPS))
    y_ref[0] = (jnp.concatenate(outs, axis=-1) * nrm_ref[...]).astype(y_ref.dtype)

    @pl.when(c == nc - 1)
    def _():
        sout_ref[0] = state_sc[...]


def _ssd(z, xbc, dtr, cst, s0, cw, cb, dtb, alog, dsk, nrm, e, T, n_valid, hi):
    B, Lp, _ = z.shape
    hp = SSD_HEADS * SSD_HEADDIM
    row = lambda b, c: (0, 0)
    return pl.pallas_call(
        functools.partial(_ssd_kernel, T=T, n_valid=n_valid, hi=hi),
        grid=(B, Lp // T),
        in_specs=[pl.BlockSpec((1, T, SSD_D_INNER), lambda b, c: (b, c, 0)),
                  pl.BlockSpec((1, T, SSD_XBC), lambda b, c: (b, c, 0)),
                  pl.BlockSpec((1, T, LANES), lambda b, c: (b, c, 0)),
                  pl.BlockSpec((1, SSD_CONV - 1, SSD_XBC), lambda b, c: (b, 0, 0)),
                  pl.BlockSpec((1, hp, SSD_STATE), lambda b, c: (b, 0, 0)),
                  pl.BlockSpec((SSD_CONV, SSD_XBC), row),
                  pl.BlockSpec((1, SSD_XBC), row),
                  pl.BlockSpec((1, LANES), row),
                  pl.BlockSpec((1, LANES), row),
                  pl.BlockSpec((1, SSD_D_INNER), row),
                  pl.BlockSpec((1, SSD_D_INNER), row),
                  pl.BlockSpec((LANES, SSD_D_INNER), row)],
        out_specs=[pl.BlockSpec((1, T, SSD_D_INNER), lambda b, c: (b, c, 0)),
                   pl.BlockSpec((1, hp, SSD_STATE), lambda b, c: (b, 0, 0)),
                   pl.BlockSpec((1, SSD_CONV - 1, SSD_XBC), lambda b, c: (b, 0, 0))],
        out_shape=[jax.ShapeDtypeStruct((B, Lp, SSD_D_INNER), _act_dtype(hi)),
                   jax.ShapeDtypeStruct((B, hp, SSD_STATE), F32),
                   jax.ShapeDtypeStruct((B, SSD_CONV - 1, SSD_XBC), F32)],
        scratch_shapes=[pltpu.VMEM((hp, SSD_STATE), F32),
                        pltpu.VMEM((T + SUBLANES, SSD_XBC), F32)],
        compiler_params=_cparams(("parallel", "arbitrary")),
        name="ssd_scan",
    )(z, xbc, dtr, cst, s0, cw, cb, dtb, alog, dsk, nrm, e)


def _latkv_kernel(lat_ref, g_ref, cos_ref, sa_ref, sb_ref, ckv_ref, kpe_ref):
    lat = lat_ref[0]
    cr = lat[:, :MLA_KV_RANK]
    ckv_ref[0] = cr * lax.rsqrt(jnp.mean(cr * cr, axis=-1, keepdims=True) + EPS) * g_ref[...]
    kpe_ref[0] = _rope_rot(lat[:, MLA_KV_RANK:], cos_ref[...], sa_ref[...], sb_ref[...])


def _latkv_post(latkv, g, cos_t, sin_a, sin_b, tm):
    B, L, wp = latkv.shape
    tm = min(tm, L)
    tab = pl.BlockSpec((tm, LANES), lambda b, i: (i, 0))
    return pl.pallas_call(
        _latkv_kernel,
        grid=(B, L // tm),
        in_specs=[pl.BlockSpec((1, tm, wp), lambda b, i: (b, i, 0)),
                  pl.BlockSpec((1, MLA_KV_RANK), lambda b, i: (0, 0)), tab, tab, tab],
        out_specs=[pl.BlockSpec((1, tm, MLA_KV_RANK), lambda b, i: (b, i, 0)),
                   pl.BlockSpec((1, tm, LANES), lambda b, i: (b, i, 0))],
        out_shape=[jax.ShapeDtypeStruct((B, L, MLA_KV_RANK), F32),
                   jax.ShapeDtypeStruct((B, L, LANES), F32)],
        compiler_params=_cparams(("parallel", "parallel")),
        name="mla_latent_kv",
    )(latkv, g, cos_t, sin_a, sin_b)


def _q_kernel(lat_ref, g_ref, w_ref, cos_ref, sa_ref, sb_ref, q_ref, *, scale, hi):
    lat = lat_ref[0]
    n = lat * lax.rsqrt(jnp.mean(lat * lat, axis=-1, keepdims=True) + EPS) * g_ref[...]
    cos_t, sin_a, sin_b = cos_ref[...], sa_ref[...], sb_ref[...]
    for h in range(MLA_HEADS):
        q = _dot(n, w_ref[h], hi)
        a = q[:, :LANES] * scale
        p = _rope_rot(q[:, LANES:], cos_t, sin_a, sin_b) * scale
        q_ref[0, h] = jnp.concatenate([a, p], axis=-1).astype(q_ref.dtype)


def _q_proj(latq, g, wq, cos_t, sin_a, sin_b, tm, hi):
    B, L, _ = latq.shape
    tm = min(tm, L)
    tab = pl.BlockSpec((tm, LANES), lambda b, i: (i, 0))
    scale = (MLA_NOPE + MLA_ROPE) ** -0.5 * math.log2(math.e)
    return pl.pallas_call(
        functools.partial(_q_kernel, scale=scale, hi=hi),
        grid=(B, L // tm),
        in_specs=[pl.BlockSpec((1, tm, MLA_Q_RANK), lambda b, i: (b, i, 0)),
                  pl.BlockSpec((1, MLA_Q_RANK), lambda b, i: (0, 0)),
                  pl.BlockSpec((MLA_HEADS, MLA_Q_RANK, 2 * LANES), lambda b, i: (0, 0, 0)), tab, tab, tab],
        out_specs=pl.BlockSpec((1, MLA_HEADS, tm, 2 * LANES), lambda b, i: (b, 0, i, 0)),
        out_shape=jax.ShapeDtypeStruct((B, MLA_HEADS, L, 2 * LANES), _act_dtype(hi)),
        compiler_params=_cparams(("parallel", "parallel")),
        name="mla_q_proj",
    )(latq, g, wq, cos_t, sin_a, sin_b)


def _kvup_kernel(ckv_ref, kpe_ref, w_ref, kv_ref, *, hi):
    ckv = ckv_ref[0]
    kpe = kpe_ref[0]
    for h in range(MLA_HEADS):
        kv = _dot(ckv, w_ref[h], hi)
        kv_ref[0, h] = jnp.concatenate([kv, kpe], axis=-1).astype(kv_ref.dtype)


def _kv_up(ckv, kpe, wkv, tm, hi):
    B, Lk, _ = ckv.shape
    tm = min(tm, Lk)
    return pl.pallas_call(
        functools.partial(_kvup_kernel, hi=hi),
        grid=(B, Lk // tm),
        in_specs=[pl.BlockSpec((1, tm, MLA_KV_RANK), lambda b, i: (b, i, 0)),
                  pl.BlockSpec((1, tm, LANES), lambda b, i: (b, i, 0)),
                  pl.BlockSpec((MLA_HEADS, MLA_KV_RANK, LANES), lambda b, i: (0, 0, 0))],
        out_specs=pl.BlockSpec((1, MLA_HEADS, tm, 2 * LANES), lambda b, i: (b, 0, i, 0)),
        out_shape=jax.ShapeDtypeStruct((B, MLA_HEADS, Lk, 2 * LANES), _act_dtype(hi)),
        compiler_params=_cparams(("parallel", "parallel")),
        name="mla_kv_up",
    )(ckv, kpe, wkv)


def _attn_kernel(q_ref, kv_ref, o_ref, m_sc, l_sc, acc_sc, *, tq, tk, tkm, q_off, kv_len, hi):
    q0 = pl.program_id(2) * tq
    first = q_off + q0
    n_all = jnp.minimum(kv_len, (first // CHUNK + 1) * CHUNK)
    n_any = jnp.minimum(kv_len, ((first + tq - 1) // CHUNK + 1) * CHUNK)
    n_full = n_all // tk
    m_lo = n_full * (tk // tkm)
    m_hi = (n_any + tkm - 1) // tkm
    qchunk = jnp.right_shift(first + lax.broadcasted_iota(I32, (tq, 1), 0), CHUNK_SHIFT)

    def block(k0, width, masked):
        hds = range(2)
        k = [kv_ref[0, hd, pl.ds(k0, width), :] for hd in hds]
        s = [_dot_nt(q_ref[0, hd], k[hd], hi) for hd in hds]
        if masked:
            kpos = k0 + lax.broadcasted_iota(I32, (1, width), 1)
            vis = jnp.logical_and(jnp.right_shift(kpos, CHUNK_SHIFT) <= qchunk, kpos < kv_len)
            s = [jnp.where(vis, s_, NEG_BIG) for s_ in s]
        m_prev = [m_sc[hd] for hd in hds]
        m_new = [jnp.maximum(m_prev[hd], jnp.max(s[hd], axis=-1, keepdims=True)) for hd in hds]
        alpha = [jnp.exp2(m_prev[hd] - m_new[hd]) for hd in hds]
        p = [jnp.exp2(s[hd] - jnp.tile(m_new[hd], (1, width // LANES))) for hd in hds]
        pv = [_dot(p[hd], k[hd][:, :LANES], hi) for hd in hds]
        for hd in hds:
            l_sc[hd] = alpha[hd] * l_sc[hd] + jnp.sum(p[hd], axis=-1, keepdims=True)
            acc_sc[hd] = alpha[hd] * acc_sc[hd] + pv[hd]
            m_sc[hd] = m_new[hd]

    m_sc[...] = jnp.full((2, tq, LANES), NEG_BIG, F32)
    l_sc[...] = jnp.zeros((2, tq, LANES), F32)
    acc_sc[...] = jnp.zeros((2, tq, LANES), F32)

    def full_body(j, carry):
        block(pl.multiple_of(j * tk, tk), tk, False)
        return carry

    def masked_body(j, carry):
        block(pl.multiple_of(j * tkm, tkm), tkm, True)
        return carry

    lax.fori_loop(0, n_full, full_body, 0)
    lax.fori_loop(m_lo, m_hi, masked_body, 0)
    outs = [acc_sc[hd] / l_sc[hd] for hd in range(2)]
    lane = lax.broadcasted_iota(I32, (tq, LANES), 1)
    o = jnp.where(lane < MLA_V, pltpu.roll(outs[0], MLA_V, 1), outs[1])
    o_ref[0] = o.astype(o_ref.dtype)


def _attention(q, kv, tq, tk, tkm, q_off, kv_len, hi):
    B, H, L, _ = q.shape
    Lk = kv.shape[2]
    return pl.pallas_call(
        functools.partial(_attn_kernel, tq=tq, tk=tk, tkm=tkm, q_off=q_off, kv_len=kv_len, hi=hi),
        grid=(B, H // 2, L // tq),
        in_specs=[pl.BlockSpec((1, 2, tq, 2 * LANES), lambda b, h, i: (b, h, i, 0)),
                  pl.BlockSpec((1, 2, Lk, 2 * LANES), lambda b, h, i: (b, h, 0, 0))],
        out_specs=pl.BlockSpec((1, tq, LANES), lambda b, h, i: (b, i, h)),
        out_shape=jax.ShapeDtypeStruct((B, L, H * MLA_V), _act_dtype(hi)),
        scratch_shapes=[pltpu.VMEM((2, tq, LANES), F32)] * 3,
        compiler_params=_cparams(("parallel", "parallel", "parallel")),
        name="mla_attention",
    )(q, kv)


def _attn_latent_kernel(q_ref, ckv_ref, kpe_ref, wuk_ref, wuv_ref, o_ref, *, L, q_off, kv_len, hi):
    H = MLA_HEADS
    lk = ckv_ref.shape[1]
    ckv = ckv_ref[0]
    kpe = kpe_ref[0]
    qa = jnp.concatenate([_dot(q_ref[0, h][:, :LANES], wuk_ref[h], hi) for h in range(H)], axis=0)
    qp = jnp.concatenate([q_ref[0, h][:, LANES:] for h in range(H)], axis=0)
    s = _dot_nt(qa, ckv, hi) + _dot_nt(qp, kpe, hi)
    row = lax.broadcasted_iota(I32, (H * L, 1), 0)
    qchunk = jnp.right_shift(q_off + jnp.bitwise_and(row, L - 1), CHUNK_SHIFT)
    kpos = lax.broadcasted_iota(I32, (1, lk), 1)
    vis = jnp.logical_and(jnp.right_shift(kpos, CHUNK_SHIFT) <= qchunk, kpos < kv_len)
    s = jnp.where(vis, s, NEG_BIG)
    p = jnp.exp2(s - jnp.max(s, axis=-1, keepdims=True))
    lat = _dot(p, ckv, hi) / jnp.sum(p, axis=-1, keepdims=True)
    outs = [_dot(lat[h * L:(h + 1) * L], wuv_ref[h], hi)[:, :MLA_V] for h in range(H)]
    o_ref[0] = jnp.concatenate(outs, axis=-1).astype(o_ref.dtype)


def _attention_latent(q, ckv, kpe, wuk, wuv, q_off, kv_len, hi):
    B, H, L, _ = q.shape
    lk = ckv.shape[1]
    return pl.pallas_call(
        functools.partial(_attn_latent_kernel, L=L, q_off=q_off, kv_len=kv_len, hi=hi),
        grid=(B,),
        in_specs=[pl.BlockSpec((1, H, L, 2 * LANES), lambda b: (b, 0, 0, 0)),
                  pl.BlockSpec((1, lk, MLA_KV_RANK), lambda b: (b, 0, 0)),
                  pl.BlockSpec((1, lk, LANES), lambda b: (b, 0, 0)),
                  pl.BlockSpec((H, LANES, MLA_KV_RANK), lambda b: (0, 0, 0)),
                  pl.BlockSpec((H, MLA_KV_RANK, LANES), lambda b: (0, 0, 0))],
        out_specs=pl.BlockSpec((1, L, H * MLA_V), lambda b: (b, 0, 0)),
        out_shape=jax.ShapeDtypeStruct((B, L, H * MLA_V), _act_dtype(hi)),
        compiler_params=_cparams(("parallel",)),
        name="mla_attention_latent",
    )(q, ckv, kpe, wuk, wuv)


def _gdn_kernel(qkv_ref, gate_ref, ba_ref, cst_ref, s0_ref, cw_ref, dtb_ref, alog_ref, nrm_ref,
                o_ref, sout_ref, cout_ref, state_sc, cbuf, *, bb_n, **kw):
    for b in range(bb_n):
        _gdn_stream(qkv_ref.at[b], gate_ref.at[b], ba_ref.at[b], cst_ref.at[b], s0_ref.at[b], cw_ref, dtb_ref,
                    alog_ref, nrm_ref, o_ref.at[b], sout_ref.at[b], cout_ref.at[b], state_sc.at[b], cbuf.at[b], **kw)


def _gdn_stream(qkv_ref, gate_ref, ba_ref, cst_ref, s0_ref, cw_ref, dtb_ref, alog_ref, nrm_ref,
                o_ref, sout_ref, cout_ref, state_sc, cbuf, *, T, n_valid, hi):
    c = pl.program_id(1)
    nc = pl.num_programs(1)
    halo = GDN_CONV - 1
    base = SUBLANES - halo
    H = GDN_HEADS
    DK = GDN_DK

    @pl.when(c == 0)
    def _():
        state_sc[...] = s0_ref[...]
        cbuf[base:SUBLANES, :] = cst_ref[...]

    cbuf[SUBLANES:SUBLANES + T, :] = qkv_ref[...]
    conv = cw_ref[0:1, :] * cbuf[base:base + T, :]
    for k in range(1, GDN_CONV):
        conv = conv + cw_ref[k:k + 1, :] * cbuf[base + k:base + k + T, :]
    qkv = _silu(conv)

    c_last = (n_valid - 1) // T
    nv_last = n_valid - c_last * T

    @pl.when(c == c_last)
    def _():
        cout_ref[...] = cbuf[base + nv_last:base + nv_last + halo, :]

    cbuf[base:SUBLANES, :] = cbuf[base + T:SUBLANES + T, :]

    tok = lax.broadcasted_iota(I32, (T, 1), 0) + c * T
    valid = tok < n_valid
    ba = ba_ref[...]
    beta = jnp.where(valid, _sigmoid(ba), 0.0)
    g = jnp.where(valid, -jnp.exp(alog_ref[...]) * _softplus(ba + dtb_ref[...]), 0.0)
    ri = lax.broadcasted_iota(I32, (T, T), 0)
    ci = lax.broadcasted_iota(I32, (T, T), 1)
    tril = jnp.where(ci <= ri, 1.0, 0.0).astype(F32)
    gc = _dot_sel(tril, g)

    def l2n(x):
        return x * lax.rsqrt(jnp.sum(x * x, axis=-1, keepdims=True) + EPS)

    qn = [l2n(qkv[:, h * DK:(h + 1) * DK]) * (DK ** -0.5) for h in range(H)]
    kn = [l2n(qkv[:, (H + h) * DK:(H + h + 1) * DK]) for h in range(H)]
    vv = [qkv[:, (2 * H + h) * DK:(2 * H + h + 1) * DK] for h in range(H)]

    G = 2
    W = G * T
    bi = lax.broadcasted_iota(I32, (W, W), 0)
    bj = lax.broadcasted_iota(I32, (W, W), 1)
    t_shift = T.bit_length() - 1
    same = jnp.right_shift(bi, t_shift) == jnp.right_shift(bj, t_shift)
    strict = jnp.logical_and(same, bj < bi)
    incl = jnp.logical_and(same, bj <= bi)
    eye_w = jnp.where(bi == bj, 1.0, 0.0).astype(F32)
    rowhead = jnp.right_shift(lax.broadcasted_iota(I32, (W, DK), 0), t_shift)

    groups = [[grp * G + j for j in range(G)] for grp in range(H // G)]
    n_g = len(groups)
    kst = [jnp.concatenate([kn[h] for h in hs], axis=0) for hs in groups]
    qst = [jnp.concatenate([qn[h] for h in hs], axis=0) for hs in groups]
    vst = [jnp.concatenate([vv[h] for h in hs], axis=0) for hs in groups]
    cb = [jnp.concatenate([jnp.broadcast_to(gc[:, H + h:H + h + 1], (T, W)) for h in hs], axis=0)
          for hs in groups]
    bb = [jnp.concatenate([jnp.broadcast_to(beta[:, h:h + 1], (T, DK)) for h in hs], axis=0) for hs in groups]
    glast = [jnp.concatenate([jnp.broadcast_to(gc[T - 1:T, H + h:H + h + 1], (T, DK)) for h in hs], axis=0)
             for hs in groups]
    dec = [jnp.exp(jnp.minimum(c_ - jnp.transpose(c_), 0.0)) for c_ in cb]
    kk = [_dot_nt(k_, k_, hi) for k_ in kst]
    qk = [_dot_nt(q_, k_, hi) for q_, k_ in zip(qst, kst)]
    a_mat = [jnp.where(strict, jnp.tile(b_, (1, W // DK)) * kk_ * d_, 0.0) for b_, kk_, d_ in zip(bb, kk, dec)]
    qkm = [jnp.where(incl, qk_ * d_, 0.0) for qk_, d_ in zip(qk, dec)]
    p_mat = [eye_w - a_ for a_ in a_mat]
    a_pow = a_mat
    for _ in range(T.bit_length() - 2):
        a_pow = [_dot_split(a_, a_) for a_ in a_pow]
        p_mat = [p_ + _dot_split(p_, a_) for p_, a_ in zip(p_mat, a_pow)]
    cbk = [c_[:, :DK] for c_ in cb]
    eg = [jnp.exp(c_) for c_ in cbk]
    rhs = [jnp.concatenate([v_ * b_, k_ * b_ * e_], axis=-1) for v_, b_, k_, e_ in zip(vst, bb, kst, eg)]
    sol = [_dot_split(p_, r_) for p_, r_ in zip(p_mat, rhs)]
    qdec = [q_ * e_ for q_, e_ in zip(qst, eg)]
    kdec_t = [jnp.transpose(k_ * jnp.exp(g_ - c_)) for k_, g_, c_ in zip(kst, glast, cbk)]
    s_old = [state_sc[h * DK:(h + 1) * DK, :] for h in range(H)]
    ws = [[_dot(jnp.concatenate([sol[g][j * T:(j + 1) * T, DK:], qdec[g][j * T:(j + 1) * T]], axis=0),
                s_old[h], hi) for j, h in enumerate(groups[g])] for g in range(n_g)]
    vnew_st = [jnp.concatenate([sol[g][j * T:(j + 1) * T, :DK] - ws[g][j][:T] for j in range(G)], axis=0)
               for g in range(n_g)]
    o_st = [jnp.concatenate([ws[g][j][T:] for j in range(G)], axis=0) + _dot(qkm[g], vnew_st[g], hi)
            for g in range(n_g)]
    o_heads = [None] * H
    for g in range(n_g):
        for j, h in enumerate(groups[g]):
            vm = jnp.where(rowhead == j, vnew_st[g], 0.0)
            last = jnp.exp(jnp.broadcast_to(gc[T - 1:T, H + h:H + h + 1], (DK, GDN_DV)))
            state_sc[h * DK:(h + 1) * DK, :] = s_old[h] * last + _dot(kdec_t[g], vm, hi)
            o_heads[h] = o_st[g][j * T:(j + 1) * T]
    gate = gate_ref[...]
    outs = []
    for h in range(H):
        oh = o_heads[h]
        oh = oh * lax.rsqrt(jnp.mean(oh * oh, axis=-1, keepdims=True) + EPS) * nrm_ref[...]
        outs.append(oh * _silu(gate[:, h * GDN_DV:(h + 1) * GDN_DV]))
    o_ref[...] = jnp.concatenate(outs, axis=-1).astype(o_ref.dtype)

    @pl.when(c == nc - 1)
    def _():
        sout_ref[...] = state_sc[...]


def _gdn(qkv, gate, ba, cst, s0, cw, dtb, alog, nrm, T, n_valid, hi):
    B, Lp, _ = qkv.shape
    hk = GDN_HEADS * GDN_DK
    bb_n = 1
    row = lambda b, c: (0, 0)
    return pl.pallas_call(
        functools.partial(_gdn_kernel, bb_n=bb_n, T=T, n_valid=n_valid, hi=hi),
        grid=(B // bb_n, Lp // T),
        in_specs=[pl.BlockSpec((bb_n, T, GDN_QKV), lambda b, c: (b, c, 0)),
                  pl.BlockSpec((bb_n, T, GDN_HEADS * GDN_DV), lambda b, c: (b, c, 0)),
                  pl.BlockSpec((bb_n, T, LANES), lambda b, c: (b, c, 0)),
                  pl.BlockSpec((bb_n, GDN_CONV - 1, GDN_QKV), lambda b, c: (b, 0, 0)),
                  pl.BlockSpec((bb_n, hk, GDN_DV), lambda b, c: (b, 0, 0)),
                  pl.BlockSpec((GDN_CONV, GDN_QKV), row),
                  pl.BlockSpec((1, LANES), row),
                  pl.BlockSpec((1, LANES), row),
                  pl.BlockSpec((1, GDN_DV), row)],
        out_specs=[pl.BlockSpec((bb_n, T, GDN_HEADS * GDN_DV), lambda b, c: (b, c, 0)),
                   pl.BlockSpec((bb_n, hk, GDN_DV), lambda b, c: (b, 0, 0)),
                   pl.BlockSpec((bb_n, GDN_CONV - 1, GDN_QKV), lambda b, c: (b, 0, 0))],
        out_shape=[jax.ShapeDtypeStruct((B, Lp, GDN_HEADS * GDN_DV), _act_dtype(hi)),
                   jax.ShapeDtypeStruct((B, hk, GDN_DV), F32),
                   jax.ShapeDtypeStruct((B, GDN_CONV - 1, GDN_QKV), F32)],
        scratch_shapes=[pltpu.VMEM((bb_n, hk, GDN_DV), F32),
                        pltpu.VMEM((bb_n, T + SUBLANES, GDN_QKV), F32)],
        compiler_params=_cparams(("parallel", "arbitrary")),
        name="gdn_scan",
    )(qkv, gate, ba, cst, s0, cw, dtb, alog, nrm)


def _sconv_kernel(b_ref, c_ref, v_ref, cst_ref, w_ref, o_ref, cout_ref, cbuf, *, T):
    i = pl.program_id(1)
    halo = SC_CONV - 1
    base = SUBLANES - halo

    @pl.when(i == 0)
    def _():
        cbuf[base:SUBLANES, :] = cst_ref[0]

    cbuf[SUBLANES:SUBLANES + T, :] = c_ref[0] * v_ref[0]
    conv = w_ref[0:1, :] * cbuf[base:base + T, :]
    for k in range(1, SC_CONV):
        conv = conv + w_ref[k:k + 1, :] * cbuf[base + k:base + k + T, :]
    o_ref[0] = (b_ref[0] * conv).astype(o_ref.dtype)
    cout_ref[0] = cbuf[base + T:SUBLANES + T, :]
    cbuf[base:SUBLANES, :] = cbuf[base + T:SUBLANES + T, :]


def _sconv(scb, scc, scv, cst, w, T, hi):
    B, L, _ = scb.shape
    T = min(T, L)
    blk = pl.BlockSpec((1, T, SC_WIDTH), lambda b, i: (b, i, 0))
    st = pl.BlockSpec((1, SC_CONV - 1, SC_WIDTH), lambda b, i: (b, 0, 0))
    return pl.pallas_call(
        functools.partial(_sconv_kernel, T=T),
        grid=(B, L // T),
        in_specs=[blk, blk, blk, st, pl.BlockSpec((SC_CONV, SC_WIDTH), lambda b, i: (0, 0))],
        out_specs=[blk, st],
        out_shape=[jax.ShapeDtypeStruct((B, L, SC_WIDTH), _act_dtype(hi)),
                   jax.ShapeDtypeStruct((B, SC_CONV - 1, SC_WIDTH), F32)],
        scratch_shapes=[pltpu.VMEM((T + SUBLANES, SC_WIDTH), F32)],
        compiler_params=_cparams(("parallel", "arbitrary")),
        name="short_conv",
    )(scb, scc, scv, cst, w)


def _out_kernel(y_ref, o_ref, x_ref, mod_ref, wy_ref, wo_ref, g_ref, wr_ref, br_ref,
                xn_ref, h_ref, lg_ref, *, hi):
    mix = _dot(y_ref[0], wy_ref[...], hi) + _dot(o_ref[0], wo_ref[...], hi)
    xn = x_ref[0] + mod_ref[0, 2:3, :] * mix
    xn_ref[0] = xn
    h = xn * lax.rsqrt(jnp.mean(xn * xn, axis=-1, keepdims=True) + EPS) * g_ref[...]
    h = h * (1.0 + mod_ref[0, 4:5, :]) + mod_ref[0, 3:4, :]
    h_ref[0] = h
    lg_ref[0] = _dot_split(h, wr_ref[...]) + br_ref[...]


def _out_proj(y, o, x, mod, wy, wo, g, wr, br, tm, hi):
    B, L, _ = x.shape
    tm = min(tm, L)
    blk = lambda dt_w: pl.BlockSpec((1, tm, dt_w), lambda b, i: (b, i, 0))
    full = lambda s: pl.BlockSpec(s, lambda b, i: (0, 0))
    return pl.pallas_call(
        functools.partial(_out_kernel, hi=hi),
        grid=(B, L // tm),
        in_specs=[blk(D_MODEL), blk(D_MODEL), blk(D_MODEL),
                  pl.BlockSpec((1, 6, D_MODEL), lambda b, i: (b, 0, 0)),
                  full((D_MODEL, D_MODEL)), full((D_MODEL, D_MODEL)), full((1, D_MODEL)),
                  full((D_MODEL, LANES)), full((1, LANES))],
        out_specs=[blk(D_MODEL), blk(D_MODEL), blk(LANES)],
        out_shape=[jax.ShapeDtypeStruct((B, L, D_MODEL), F32),
                   jax.ShapeDtypeStruct((B, L, D_MODEL), F32),
                   jax.ShapeDtypeStruct((B, L, LANES), F32)],
        compiler_params=_cparams(("parallel", "parallel")),
        name="out_proj",
    )(y, o, x, mod, wy, wo, g, wr, br)


def _route_kernel(lg_ref, e_ref, rank_ref, gate_ref, cnt_ref, base_sc, *, tm):
    i = pl.program_id(0)

    @pl.when(i == 0)
    def _():
        base_sc[...] = jnp.zeros_like(base_sc)

    lg = lg_ref[...]
    lane_i = lax.broadcasted_iota(I32, (tm, LANES), 1)
    lane = lane_i.astype(F32)
    vals, idxs = [], []
    cur = lg
    for _ in range(TOP_K):
        m = jnp.max(cur, axis=-1, keepdims=True)
        idx = jnp.min(jnp.where(cur == m, lane, float(LANES)), axis=-1, keepdims=True)
        vals.append(m)
        idxs.append(idx)
        cur = jnp.where(lane == idx, -jnp.inf, cur)
    ex = [jnp.exp(v - vals[0]) for v in vals]
    den = ex[0] + ex[1] + ex[2] + ex[3]
    onehot = jnp.zeros((tm, LANES), F32)
    for idx in idxs:
        onehot = onehot + jnp.where(lane == idx, 1.0, 0.0)
    ri = lax.broadcasted_iota(I32, (tm, tm), 0)
    ci = lax.broadcasted_iota(I32, (tm, tm), 1)
    before = _dot(jnp.where(ci < ri, 1.0, 0.0), onehot) + base_sc[...]
    e_out = jnp.zeros((tm, LANES), I32)
    r_out = jnp.zeros((tm, LANES), I32)
    g_out = jnp.zeros((tm, LANES), F32)
    for k in range(TOP_K):
        rk = jnp.sum(jnp.where(lane == idxs[k], before, 0.0), axis=-1, keepdims=True)
        e_out = jnp.where(lane_i == k, idxs[k].astype(I32), e_out)
        r_out = jnp.where(lane_i == k, rk.astype(I32), r_out)
        g_out = jnp.where(lane_i == k, ex[k] / den, g_out)
    e_ref[...] = e_out
    rank_ref[...] = r_out
    gate_ref[...] = g_out
    base_sc[...] = base_sc[...] + jnp.sum(onehot, axis=0, keepdims=True)
    cnt_ref[...] = base_sc[...].astype(I32)


def _route(logits, tm):
    n_tok = logits.shape[0]
    tm = min(tm, n_tok)
    blk = pl.BlockSpec((tm, LANES), lambda i: (i, 0))
    return pl.pallas_call(
        functools.partial(_route_kernel, tm=tm),
        grid=(n_tok // tm,),
        in_specs=[blk],
        out_specs=[blk, blk, blk, pl.BlockSpec((1, LANES), lambda i: (0, 0))],
        out_shape=[jax.ShapeDtypeStruct((n_tok, LANES), I32),
                   jax.ShapeDtypeStruct((n_tok, LANES), I32),
                   jax.ShapeDtypeStruct((n_tok, LANES), F32),
                   jax.ShapeDtypeStruct((1, LANES), I32)],
        scratch_shapes=[pltpu.VMEM((1, LANES), F32)],
        compiler_params=_cparams(("arbitrary",)),
        name="moe_route",
    )(logits)


def _dispatch_kernel(zs_ref, nz_ref, dest_ref, h_ref, out_hbm, zbuf, sem, zsem, *, tm, n_blocks):
    bm = zbuf.shape[0]

    def zero_copy(row0):
        return pltpu.make_async_copy(zbuf, out_hbm.at[pl.ds(row0, bm), :], zsem)

    @pl.when(pl.program_id(0) == 0)
    def _():
        zbuf[...] = jnp.zeros_like(zbuf)
        n_used = nz_ref[0]
        for e in range(N_EXPERTS):
            @pl.when(zs_ref[e] >= 0)
            def _(e=e):
                zero_copy(pl.multiple_of(zs_ref[e], bm)).start()

        def tail(j, carry):
            zero_copy(pl.multiple_of(j * bm, bm)).start()
            return carry

        lax.fori_loop(n_used, n_blocks, tail, 0)

        def drain(j, carry):
            zero_copy(0).wait()
            return carry

        lax.fori_loop(0, nz_ref[1], drain, 0)

    def issue(r, carry):
        for k in range(TOP_K):
            d = dest_ref[r * TOP_K + k]
            pltpu.make_async_copy(h_ref.at[pl.ds(r, 1), :], out_hbm.at[pl.ds(d, 1), :], sem).start()
        return carry

    lax.fori_loop(0, tm, issue, 0)
    for _ in range(TOP_K):
        pltpu.make_async_copy(h_ref, out_hbm.at[pl.ds(0, tm), :], sem).wait()


def _dispatch(h, dest_flat, zero_start, zero_counts, n_rows, tm, bm):
    n_tok = h.shape[0]
    tm = min(tm, n_tok)
    grid_spec = pltpu.PrefetchScalarGridSpec(
        num_scalar_prefetch=2,
        grid=(n_tok // tm,),
        in_specs=[pl.BlockSpec((tm * TOP_K,), lambda i, zs, nz: (i,), memory_space=pltpu.SMEM),
                  pl.BlockSpec((tm, D_MODEL), lambda i, zs, nz: (i, 0))],
        out_specs=pl.BlockSpec(memory_space=pl.ANY),
        scratch_shapes=[pltpu.VMEM((bm, D_MODEL), F32), pltpu.SemaphoreType.DMA(()),
                        pltpu.SemaphoreType.DMA(())],
    )
    return pl.pallas_call(
        functools.partial(_dispatch_kernel, tm=tm, n_blocks=n_rows // bm),
        grid_spec=grid_spec,
        out_shape=jax.ShapeDtypeStruct((n_rows, D_MODEL), F32),
        compiler_params=_cparams(("arbitrary",)),
        name="moe_dispatch",
    )(zero_start, zero_counts, dest_flat, h)


def _ffn_kernel(blk_e_ref, nused_ref, x_ref, wgu_ref, bgu_ref, wd_ref, bd_ref, o_ref, wgu_sc, wd_sc):
    i = pl.program_id(0)
    prev = blk_e_ref[jnp.maximum(i - 1, 0)]
    fresh = jnp.logical_or(i == 0, blk_e_ref[i] != prev)
    active = i < nused_ref[0]

    @pl.when(jnp.logical_and(active, fresh))
    def _():
        wgu_sc[...] = wgu_ref[0, 0].astype(BF16)
        wd_sc[...] = wd_ref[0, 0].astype(BF16)

    @pl.when(active)
    def _():
        gu = jnp.dot(x_ref[...].astype(BF16), wgu_sc[...], preferred_element_type=F32) + bgu_ref[0]
        gate = jnp.minimum(gu[:, :D_FF], SWIGLU_LIMIT)
        up = jnp.clip(gu[:, D_FF:], -SWIGLU_LIMIT, SWIGLU_LIMIT)
        act = (up + 1.0) * gate * _sigmoid(SWIGLU_ALPHA * gate)
        o_ref[...] = jnp.dot(act.astype(BF16), wd_sc[...], preferred_element_type=F32) + bd_ref[0]

    @pl.when(jnp.logical_not(active))
    def _():
        o_ref[...] = jnp.zeros_like(o_ref)


def _expert_ffn(xin, blk_e, n_used, w_gu, b_gu, w_down, b_down, layer, bm):
    n_rows = xin.shape[0]
    n_blocks = n_rows // bm

    def row_map(i, be, nu):
        return (jnp.minimum(i, nu[0] - 1), 0)

    def e_map4(i, be, nu):
        return (layer, be[jnp.minimum(i, nu[0] - 1)], 0, 0)

    def e_map3(i, be, nu):
        return (layer * N_EXPERTS + be[jnp.minimum(i, nu[0] - 1)], 0, 0)

    grid_spec = pltpu.PrefetchScalarGridSpec(
        num_scalar_prefetch=2,
        grid=(n_blocks,),
        in_specs=[pl.BlockSpec((bm, D_MODEL), row_map),
                  pl.BlockSpec((1, 1, D_MODEL, 2 * D_FF), e_map4),
                  pl.BlockSpec((1, 1, 2 * D_FF), e_map3),
                  pl.BlockSpec((1, 1, D_FF, D_MODEL), e_map4),
                  pl.BlockSpec((1, 1, D_MODEL), e_map3)],
        out_specs=pl.BlockSpec((bm, D_MODEL), lambda i, be, nu: (i, 0)),
        scratch_shapes=[pltpu.VMEM((D_MODEL, 2 * D_FF), BF16),
                        pltpu.VMEM((D_FF, D_MODEL), BF16)],
    )
    return pl.pallas_call(
        _ffn_kernel,
        grid_spec=grid_spec,
        out_shape=jax.ShapeDtypeStruct((n_rows, D_MODEL), F32),
        compiler_params=_cparams(("arbitrary",)),
        name="moe_expert_ffn",
    )(blk_e, n_used, xin, w_gu, b_gu.reshape(DEPTH * N_EXPERTS, 1, 2 * D_FF), w_down,
      b_down.reshape(DEPTH * N_EXPERTS, 1, D_MODEL))


def _combine_kernel(dest_ref, dest_next_ref, f_hbm, gate_ref, x_ref, mod_ref, g_ref, o_ref, buf, sem, *, tm, final):
    i = pl.program_id(0)
    slot = lax.rem(i, 2)

    def fetch(idx_ref, s):
        def issue(r, carry):
            for k in range(TOP_K):
                d = idx_ref[r * TOP_K + k]
                pltpu.make_async_copy(f_hbm.at[pl.ds(d, 1), :], buf.at[s, k, pl.ds(r, 1), :], sem.at[s]).start()
            return carry

        lax.fori_loop(0, tm, issue, 0)

    @pl.when(i == 0)
    def _():
        fetch(dest_ref, 0)

    @pl.when(i + 1 < pl.num_programs(0))
    def _():
        fetch(dest_next_ref, 1 - slot)

    for k in range(TOP_K):
        pltpu.make_async_copy(f_hbm.at[pl.ds(0, tm), :], buf.at[slot, k], sem.at[slot]).wait()
    gates = gate_ref[...]
    moe = gates[:, 0:1] * buf[slot, 0]
    for k in range(1, TOP_K):
        moe = moe + gates[:, k:k + 1] * buf[slot, k]
    xo = x_ref[...] + mod_ref[0, 5:6, :] * moe
    if final:
        xo = xo * lax.rsqrt(jnp.mean(xo * xo, axis=-1, keepdims=True) + EPS) * g_ref[...]
    o_ref[...] = xo


def _combine(ffn_out, dest_flat, gates, x, mod, g_final, tiles_per_batch, tm, final):
    n_tok = x.shape[0]
    n_tiles = n_tok // tm
    return pl.pallas_call(
        functools.partial(_combine_kernel, tm=tm, final=final),
        grid=(n_tiles,),
        in_specs=[pl.BlockSpec((tm * TOP_K,), lambda i: (i,), memory_space=pltpu.SMEM),
                  pl.BlockSpec((tm * TOP_K,), lambda i: (jnp.minimum(i + 1, n_tiles - 1),),
                               memory_space=pltpu.SMEM),
                  pl.BlockSpec(memory_space=pl.ANY),
                  pl.BlockSpec((tm, LANES), lambda i: (i, 0)),
                  pl.BlockSpec((tm, D_MODEL), lambda i: (i, 0)),
                  pl.BlockSpec((1, 6, D_MODEL), lambda i: (i // tiles_per_batch, 0, 0)),
                  pl.BlockSpec((1, D_MODEL), lambda i: (0, 0))],
        out_specs=pl.BlockSpec((tm, D_MODEL), lambda i: (i, 0)),
        out_shape=jax.ShapeDtypeStruct((n_tok, D_MODEL), F32),
        scratch_shapes=[pltpu.VMEM((2, TOP_K, tm, D_MODEL), F32), pltpu.SemaphoreType.DMA((2,))],
        compiler_params=_cparams(("arbitrary",)),
        name="moe_combine",
    )(dest_flat, dest_flat, ffn_out, gates, x, mod, g_final)


def _pad_cols(w, width):
    return jnp.pad(w, ((0, 0), (0, width - w.shape[1])))


def _pad_lanes(v, offset=0):
    return jnp.pad(v.astype(F32), (offset, LANES - offset - v.shape[0])).reshape(1, LANES)


def _rope_tables(pos):
    half = MLA_ROPE // 2
    inv = ROPE_THETA ** (-jnp.arange(half, dtype=F32) / half)
    ang = pos.astype(F32)[:, None] * inv[None, :]
    cos, sin = jnp.cos(ang), jnp.sin(ang)
    z = jnp.zeros_like(cos)
    pad = jnp.zeros((pos.shape[0], LANES - MLA_ROPE), F32)
    cos_t = jnp.concatenate([cos, cos, pad], axis=1)
    sin_a = jnp.concatenate([-sin, z, pad], axis=1)
    sin_b = jnp.concatenate([z, sin, pad], axis=1)
    return cos_t, sin_a, sin_b


def _pad_seq(t, lp):
    return jnp.pad(t, ((0, 0), (0, lp - t.shape[1]), (0, 0)))


def _even_weights(W, j, dt):
    w_in = W['ev_w_in'][j]
    o1 = SSD_D_INNER
    o2 = o1 + SSD_XBC
    o3 = o2 + SSD_HEADS
    o4 = o3 + MLA_Q_RANK
    w_ssd = jnp.concatenate([w_in[:, :o2], _pad_cols(w_in[:, o2:o3], LANES)], axis=1).astype(dt)
    w_mla = jnp.concatenate([w_in[:, o3:o4], _pad_cols(w_in[:, o4:], MLA_Q_RANK)], axis=1).astype(dt)
    wq = W['mla_w_q_up'][j].reshape(MLA_Q_RANK, MLA_HEADS, MLA_NOPE + MLA_ROPE)
    wq = jnp.concatenate([wq[..., :MLA_NOPE], jnp.zeros((MLA_Q_RANK, MLA_HEADS, LANES - MLA_NOPE), F32),
                          wq[..., MLA_NOPE:], jnp.zeros((MLA_Q_RANK, MLA_HEADS, LANES - MLA_ROPE), F32)], axis=-1)
    wq = jnp.transpose(wq, (1, 0, 2)).astype(dt)
    wkv = jnp.transpose(W['mla_w_kv_up'][j].reshape(MLA_KV_RANK, MLA_HEADS, MLA_NOPE + MLA_V),
                        (1, 0, 2)).astype(dt)
    expand = (jnp.arange(LANES)[:, None] == (jnp.arange(SSD_D_INNER) // SSD_HEADDIM)[None, :]).astype(F32)
    wuk = jnp.pad(jnp.transpose(wkv[:, :, :MLA_NOPE], (0, 2, 1)), ((0, 0), (0, LANES - MLA_NOPE), (0, 0)))
    wuv = jnp.pad(wkv[:, :, MLA_NOPE:], ((0, 0), (0, 0), (0, LANES - MLA_V)))
    return dict(
        w_ssd=w_ssd, w_mla=w_mla, wq=wq, wkv=wkv, wuk=wuk, wuv=wuv, expand=expand,
        wy=W['ev_w_out'][j][:SSD_D_INNER].astype(dt), wo=W['ev_w_out'][j][SSD_D_INNER:].astype(dt),
        cw=W['ssd_conv_w'][j], cb=W['ssd_conv_b'][j].reshape(1, SSD_XBC),
        dtb=_pad_lanes(W['ssd_dt_bias'][j]), alog=_pad_lanes(W['ssd_a_log'][j]),
        dsk=jnp.repeat(W['ssd_d'][j].astype(F32), SSD_HEADDIM).reshape(1, SSD_D_INNER),
        nrm=W['ssd_norm'][j].reshape(1, SSD_D_INNER),
        qn=W['mla_q_norm'][j].reshape(1, MLA_Q_RANK), kvn=W['mla_kv_norm'][j].reshape(1, MLA_KV_RANK))


def _odd_weights(W, j, dt):
    w_in = W['od_w_in'][j]
    o1 = GDN_QKV
    o2 = o1 + GDN_HEADS * GDN_DV
    o3 = o2 + 2 * GDN_HEADS
    w_gdn = jnp.concatenate([w_in[:, :o2], _pad_cols(w_in[:, o2:o3], LANES)], axis=1).astype(dt)
    w_sc = w_in[:, o3:].astype(dt)
    return dict(
        w_gdn=w_gdn, w_sc=w_sc,
        wy=W['od_w_out'][j][:GDN_HEADS * GDN_DV].astype(dt), wo=W['od_w_out'][j][GDN_HEADS * GDN_DV:].astype(dt),
        cw=W['gdn_conv_w'][j], dtb=_pad_lanes(W['gdn_dt_bias'][j], GDN_HEADS),
        alog=_pad_lanes(W['gdn_a_log'][j], GDN_HEADS), nrm=W['gdn_norm'][j].reshape(1, GDN_DV),
        scw=W['sconv_w'][j])


def _moe(h, logits, x_new, mod, W, i, g_final, L, final):
    n_tok = h.shape[0]
    bm = MOE_BM if n_tok * TOP_K >= N_EXPERTS * MOE_BM else MOE_BM_SMALL
    e_pad, rank_pad, gates, cnt = _route(logits, 256)
    counts = cnt[0, :N_EXPERTS]
    padded = (counts + bm - 1) // bm * bm
    pend = jnp.cumsum(padded)
    pstart = pend - padded
    e_sel = e_pad[:, :TOP_K]
    dest = (pstart[e_sel] + rank_pad[:, :TOP_K]).astype(I32).reshape(-1)
    n_blocks = n_tok * TOP_K // bm + N_EXPERTS
    n_rows = n_blocks * bm
    blk_start = jnp.arange(n_blocks, dtype=pend.dtype) * bm
    blk_e = jnp.minimum(jnp.sum((blk_start[:, None] >= pend[None, :]).astype(I32), axis=1), N_EXPERTS - 1)
    n_used = (pend[-1:] // bm).astype(I32)
    zero_start = jnp.where(counts > 0, pend - bm, -1).astype(I32)
    n_zero = jnp.sum((counts > 0).astype(I32)) + n_blocks - n_used[0]
    zero_counts = jnp.stack([n_used[0], n_zero]).astype(I32)
    xin = _dispatch(h, dest, zero_start, zero_counts, n_rows, DISPATCH_TM, bm)
    f_out = _expert_ffn(xin, blk_e, n_used, W['w_gu'], W['b_gu'], W['w_down'], W['b_down'], i, bm)
    tm = min(256, L)
    return _combine(f_out, dest, gates, x_new, mod, g_final, L // tm, tm, final)


def _trunk(x, c_mod, pos0, caches, W, PW, seq_t, hi):
    B, L, _ = x.shape
    new = {}
    pos = pos0 + jnp.arange(L, dtype=I32)
    cos_t, sin_a, sin_b = _rope_tables(pos)
    wr_all = W['w_router']
    for i in range(DEPTH):
        mod = c_mod[i]
        g_mix = W['norm_mix'][i].reshape(1, D_MODEL)
        j = i // 2
        if i % 2 == 0:
            P = PW[i]
            z, xbc, dtr = _in_proj(x, mod, g_mix, P['w_ssd'],
                                   ((0, SSD_D_INNER), (SSD_D_INNER, SSD_XBC), (SSD_D_INNER + SSD_XBC, LANES)),
                                   (F32, F32, F32), 0, 1, 512, hi)
            latq, latkv = _in_proj(x, mod, g_mix, P['w_mla'], ((0, MLA_Q_RANK), (MLA_Q_RANK, MLA_Q_RANK)),
                                   (F32, F32), 0, 1, 512, hi)
            T = seq_t['ssd']
            lp = -(-L // T) * T
            y, s_new, cst_new = _ssd(_pad_seq(z, lp), _pad_seq(xbc, lp), _pad_seq(dtr, lp),
                                     caches['ssd_conv'][j], caches['ssd'][j].reshape(B, -1, SSD_STATE),
                                     P['cw'], P['cb'], P['dtb'], P['alog'], P['dsk'], P['nrm'], P['expand'], T, L, hi)
            y = y[:, :L]
            new['ssd'] = s_new.reshape(1, B, SSD_HEADS, SSD_HEADDIM, SSD_STATE)
            new['ssd_conv'] = cst_new[None]
            ckv_new, kpe_new = _latkv_post(latkv, P['kvn'], cos_t, sin_a, sin_b, 512)
            new['mla_ckv'] = ckv_new[None]
            new['mla_krope'] = kpe_new[None, :, :, :MLA_ROPE]
            ckv_past, kpe_past = caches['mla_ckv'][j], caches['mla_krope'][j]
            past = ckv_past.shape[1]
            kv_len = past + L
            tk = seq_t['tk']
            lk = -(-kv_len // tk) * tk
            ckv_all = _pad_seq(jnp.concatenate([ckv_past, ckv_new], axis=1), lk)
            kpe_all = _pad_seq(jnp.concatenate(
                [jnp.pad(kpe_past, ((0, 0), (0, 0), (0, LANES - MLA_ROPE))), kpe_new], axis=1), lk)
            q = _q_proj(latq, P['qn'], P['wq'], cos_t, sin_a, sin_b, 512, hi)
            if seq_t['latent']:
                o = _attention_latent(q, ckv_all, kpe_all, P['wuk'], P['wuv'], pos0, kv_len, hi)
            else:
                kv = _kv_up(ckv_all, kpe_all, P['wkv'], min(tk, 512), hi)
                o = _attention(q, kv, min(seq_t['tq'], L), tk, seq_t['tkm'], pos0, kv_len, hi)
        else:
            P = PW[i]
            hv = GDN_HEADS * GDN_DV
            qkv, gate, ba = _in_proj(x, mod, g_mix, P['w_gdn'],
                                     ((0, GDN_QKV), (GDN_QKV, hv), (GDN_QKV + hv, LANES)),
                                     (F32, F32, F32), 0, 1, 512, hi)
            scb, scc, scv = _in_proj(x, mod, g_mix, P['w_sc'],
                                     ((0, SC_WIDTH), (SC_WIDTH, SC_WIDTH), (2 * SC_WIDTH, SC_WIDTH)),
                                     (F32, F32, F32), 0, 1, 512, hi)
            T = CHUNK
            lp = -(-L // T) * T
            y, s_new, cst_new = _gdn(_pad_seq(qkv, lp), _pad_seq(gate, lp), _pad_seq(ba, lp),
                                     caches['gdn_conv'][j], caches['gdn'][j].reshape(B, -1, GDN_DV),
                                     P['cw'], P['dtb'], P['alog'], P['nrm'], T, L, hi)
            y = y[:, :L]
            new['gdn'] = s_new.reshape(1, B, GDN_HEADS, GDN_DK, GDN_DV)
            new['gdn_conv'] = cst_new[None]
            o, sc_new = _sconv(scb, scc, scv, caches['sconv'][j], P['scw'], 512, hi)
            new['sconv'] = sc_new[None]
        wr = _pad_cols(wr_all[i], LANES)
        br = jnp.concatenate([W['b_router'][i].astype(F32), jnp.full((LANES - N_EXPERTS,), NEG_BIG, F32)]).reshape(1, LANES)
        x_new, h, logits = _out_proj(y, o, x, mod, P['wy'], P['wo'], W['norm_ffn'][i].reshape(1, D_MODEL),
                                     wr, br, 512, hi)
        final = i == DEPTH - 1
        xo = _moe(h.reshape(B * L, D_MODEL), logits.reshape(B * L, LANES), x_new.reshape(B * L, D_MODEL),
                  mod, W, i, W['norm_final'].reshape(1, D_MODEL), L, final)
        x = xo.reshape(B, L, D_MODEL)
    return x, new


def _prep_weights(W, dt):
    PW = {}
    for i in range(DEPTH):
        PW[i] = _even_weights(W, i // 2, dt) if i % 2 == 0 else _odd_weights(W, i // 2, dt)
    return PW


def kernel(x_prompt, x_sample, c_prompt, c_sample, cache_mla_ckv, cache_mla_krope, state_ssd, state_ssd_conv, state_gdn, state_gdn_conv, state_sconv, norm_mix, norm_ffn, w_ada, b_ada, w_router, b_router, w_gu, b_gu, w_down, b_down, norm_final, ev_w_in, ev_w_out, ssd_conv_w, ssd_conv_b, ssd_dt_bias, ssd_a_log, ssd_d, ssd_norm, mla_q_norm, mla_w_q_up, mla_kv_norm, mla_w_kv_up, od_w_in, od_w_out, gdn_conv_w, gdn_dt_bias, gdn_a_log, gdn_norm, sconv_w):
    W = dict(norm_mix=norm_mix, norm_ffn=norm_ffn, w_ada=w_ada, b_ada=b_ada, w_router=w_router,
             b_router=b_router, w_gu=w_gu, b_gu=b_gu, w_down=w_down, b_down=b_down, norm_final=norm_final,
             ev_w_in=ev_w_in, ev_w_out=ev_w_out, ssd_conv_w=ssd_conv_w, ssd_conv_b=ssd_conv_b,
             ssd_dt_bias=ssd_dt_bias, ssd_a_log=ssd_a_log, ssd_d=ssd_d, ssd_norm=ssd_norm,
             mla_q_norm=mla_q_norm, mla_w_q_up=mla_w_q_up, mla_kv_norm=mla_kv_norm, mla_w_kv_up=mla_w_kv_up,
             od_w_in=od_w_in, od_w_out=od_w_out, gdn_conv_w=gdn_conv_w, gdn_dt_bias=gdn_dt_bias,
             gdn_a_log=gdn_a_log, gdn_norm=gdn_norm, sconv_w=sconv_w)
    bp, bs = x_prompt.shape[0], x_sample.shape[0]
    nb = 16
    c_all = jnp.concatenate([c_prompt, c_sample, jnp.zeros((nb - bp - bs, D_MODEL), F32)], axis=0)
    mod_all = _ada_mod(c_all, w_ada, b_ada).reshape(DEPTH, nb, 6, D_MODEL)
    n_even, n_odd = (DEPTH + 1) // 2, DEPTH // 2
    zero_caches = dict(
        mla_ckv=jnp.zeros((n_even, bp, 0, MLA_KV_RANK), F32), mla_krope=jnp.zeros((n_even, bp, 0, MLA_ROPE), F32),
        ssd=jnp.zeros((n_even, bp, SSD_HEADS, SSD_HEADDIM, SSD_STATE), F32),
        ssd_conv=jnp.zeros((n_even, bp, SSD_CONV - 1, SSD_XBC), F32),
        gdn=jnp.zeros((n_odd, bp, GDN_HEADS, GDN_DK, GDN_DV), F32),
        gdn_conv=jnp.zeros((n_odd, bp, GDN_CONV - 1, GDN_QKV), F32),
        sconv=jnp.zeros((n_odd, bp, SC_CONV - 1, SC_WIDTH), F32))
    y_p, sp = _trunk(x_prompt, mod_all[:, :bp], 0, zero_caches, W, _prep_weights(W, BF16),
                     dict(ssd=256, tq=1024, tk=1024, tkm=512, latent=False), False)
    past = cache_mla_ckv.shape[2]
    caches = dict(mla_ckv=cache_mla_ckv, mla_krope=cache_mla_krope, ssd=state_ssd, ssd_conv=state_ssd_conv,
                  gdn=state_gdn, gdn_conv=state_gdn_conv, sconv=state_sconv)
    y_s, ss = _trunk(x_sample, mod_all[:, bp:bp + bs], past, caches, W, _prep_weights(W, F32),
                     dict(ssd=128, tq=32, tk=256, tkm=256, latent=True), True)
    return (y_p, y_s,
            sp['mla_ckv'], ss['mla_ckv'], sp['mla_krope'], ss['mla_krope'],
            sp['ssd'], ss['ssd'], sp['ssd_conv'], ss['ssd_conv'],
            sp['gdn'], ss['gdn'], sp['gdn_conv'], ss['gdn_conv'],
            sp['sconv'], ss['sconv'])
```

```python
import functools
import math

import jax
import jax.numpy as jnp
from jax import lax
from jax.experimental import pallas as pl
from jax.experimental.pallas import tpu as pltpu

F32 = jnp.float32
BF16 = jnp.bfloat16
I32 = jnp.int32

D_MODEL = 1024
DEPTH = 2
CHUNK = 64
CHUNK_SHIFT = 6
EPS = 1e-6
SSD_D_INNER = D_MODEL
SSD_HEADDIM = 64
SSD_HEADS = SSD_D_INNER // SSD_HEADDIM
SSD_GROUPS = 4
SSD_STATE = 128
SSD_CONV = 4
SSD_XBC = SSD_D_INNER + 2 * SSD_GROUPS * SSD_STATE
MLA_HEADS = 16
MLA_NOPE = 64
MLA_ROPE = 32
MLA_V = 64
MLA_Q_RANK = 384
MLA_KV_RANK = 256
ROPE_THETA = 10000.0
ROPE_LANE = MLA_NOPE
GDN_HEADS = 8
GDN_DK = 128
GDN_DV = 128
GDN_CONV = 4
GDN_QKV = GDN_HEADS * (2 * GDN_DK + GDN_DV)
SC_WIDTH = D_MODEL
SC_CONV = 3
N_EXPERTS = 32
TOP_K = 4
D_FF = D_MODEL
SWIGLU_LIMIT = 7.0
SWIGLU_ALPHA = 1.702

LANES = 128
SUBLANES = 8
VMEM_LIMIT = 56 * 1024 * 1024

NEG_BIG = -1e30
MOE_BM = 512
MOE_BM_SMALL = 128
DISPATCH_TM = 1024


def _cparams(sem):
    return pltpu.CompilerParams(dimension_semantics=sem, vmem_limit_bytes=VMEM_LIMIT)


def _sigmoid(x):
    return 1.0 / (1.0 + jnp.exp(-x))


def _silu(x):
    return x * _sigmoid(x)


def _softplus(x):
    return jnp.maximum(x, 0.0) + jnp.log1p(jnp.exp(-jnp.abs(x)))


_NN = (((1,), (0,)), ((), ()))
_NT = (((1,), (1,)), ((), ()))


def _dot_split(a, b, dims=_NN):
    a_h = a.astype(BF16)
    b_h = b.astype(BF16)
    a_l = (a - a_h.astype(F32)).astype(BF16)
    b_l = (b - b_h.astype(F32)).astype(BF16)
    d = functools.partial(lax.dot_general, dimension_numbers=dims, preferred_element_type=F32)
    return d(a_h, b_h) + d(a_l, b_h) + d(a_h, b_l)


def _dot(a, b, hi=False):
    if hi:
        return _dot_split(a.astype(F32), b.astype(F32))
    return jnp.dot(a.astype(BF16), b.astype(BF16), preferred_element_type=F32)


def _dot_nt(a, b, hi=False):
    if hi:
        return _dot_split(a.astype(F32), b.astype(F32), _NT)
    return lax.dot_general(a.astype(BF16), b.astype(BF16), _NT, preferred_element_type=F32)


def _split3(a):
    a1 = a.astype(BF16)
    r = a - a1.astype(F32)
    a2 = r.astype(BF16)
    a3 = (r - a2.astype(F32)).astype(BF16)
    return a1, a2, a3


def _dot_sel(sel, a, dims=_NN, sel_left=True):
    s = sel.astype(BF16)
    d = functools.partial(lax.dot_general, dimension_numbers=dims, preferred_element_type=F32)
    if sel_left:
        return sum(d(s, p) for p in _split3(a))
    return sum(d(p, s) for p in _split3(a))


def _act_dtype(hi):
    return F32 if hi else BF16


def _rope_rot(p, cos_t, sin_a, sin_b):
    return p * cos_t + pltpu.roll(p, LANES - MLA_ROPE // 2, 1) * sin_a + pltpu.roll(p, MLA_ROPE // 2, 1) * sin_b


def _ada_kernel(c_ref, w_ref, b_ref, o_ref):
    c = c_ref[...]
    o_ref[0] = _dot_split(_silu(c), w_ref[0]) + b_ref[0]


def _ada_mod(c_all, w_ada, b_ada):
    nb = c_all.shape[0]
    return pl.pallas_call(
        _ada_kernel,
        grid=(DEPTH, 6),
        in_specs=[pl.BlockSpec((nb, D_MODEL), lambda i, j: (0, 0)),
                  pl.BlockSpec((1, D_MODEL, D_MODEL), lambda i, j: (i, 0, j)),
                  pl.BlockSpec((1, 1, D_MODEL), lambda i, j: (i, 0, j))],
        out_specs=pl.BlockSpec((1, nb, D_MODEL), lambda i, j: (i, 0, j)),
        out_shape=jax.ShapeDtypeStruct((DEPTH, nb, 6 * D_MODEL), F32),
        compiler_params=_cparams(("parallel", "parallel")),
        name="ada_mod",
    )(c_all, w_ada, b_ada.reshape(DEPTH, 1, 6 * D_MODEL))


def _in_kernel(x_ref, mod_ref, g_ref, w_ref, *out_refs, segs, shift_row, scale_row, hi):
    x = x_ref[0]
    h = x * lax.rsqrt(jnp.mean(x * x, axis=-1, keepdims=True) + EPS) * g_ref[...]
    h = h * (1.0 + mod_ref[0, scale_row:scale_row + 1, :]) + mod_ref[0, shift_row:shift_row + 1, :]
    r = _dot(h, w_ref[...], hi)
    for (off, width), o_ref in zip(segs, out_refs):
        o_ref[0] = r[:, off:off + width].astype(o_ref.dtype)


def _in_proj(x, mod, g, w, segs, dtypes, shift_row, scale_row, tm, hi):
    B, L, _ = x.shape
    n_p = w.shape[1]
    tm = min(tm, L)
    return pl.pallas_call(
        functools.partial(_in_kernel, segs=segs, shift_row=shift_row, scale_row=scale_row, hi=hi),
        grid=(B, L // tm),
        in_specs=[pl.BlockSpec((1, tm, D_MODEL), lambda b, i: (b, i, 0)),
                  pl.BlockSpec((1, 6, D_MODEL), lambda b, i: (b, 0, 0)),
                  pl.BlockSpec((1, D_MODEL), lambda b, i: (0, 0)),
                  pl.BlockSpec((D_MODEL, n_p), lambda b, i: (0, 0))],
        out_specs=[pl.BlockSpec((1, tm, wd), lambda b, i: (b, i, 0)) for _, wd in segs],
        out_shape=[jax.ShapeDtypeStruct((B, L, wd), dt) for (_, wd), dt in zip(segs, dtypes)],
        compiler_params=_cparams(("parallel", "parallel")),
        name="in_proj",
    )(x, mod, g, w)


def _ssd_kernel(z_ref, xbc_ref, dtr_ref, cst_ref, s0_ref, cw_ref, cb_ref, dtb_ref, alog_ref, dsk_ref,
                nrm_ref, e_ref, y_ref, sout_ref, cout_ref, state_sc, cbuf, *, T, n_valid, hi):
    c = pl.program_id(1)
    nc = pl.num_programs(1)
    halo = SSD_CONV - 1
    base = SUBLANES - halo

    @pl.when(c == 0)
    def _():
        state_sc[...] = s0_ref[0]
        cbuf[base:SUBLANES, :] = cst_ref[0]

    cbuf[SUBLANES:SUBLANES + T, :] = xbc_ref[0]
    conv = cb_ref[...] + cw_ref[0:1, :] * cbuf[base:base + T, :]
    for k in range(1, SSD_CONV):
        conv = conv + cw_ref[k:k + 1, :] * cbuf[base + k:base + k + T, :]
    xc = _silu(conv)

    c_last = (n_valid - 1) // T
    nv_last = n_valid - c_last * T

    @pl.when(c == c_last)
    def _():
        cout_ref[0] = cbuf[base + nv_last:base + nv_last + halo, :]

    cbuf[base:SUBLANES, :] = cbuf[base + T:SUBLANES + T, :]

    xs = xc[:, :SSD_D_INNER]
    gn = SSD_GROUPS * SSD_STATE
    bm = xc[:, SSD_D_INNER:SSD_D_INNER + gn]
    cm = xc[:, SSD_D_INNER + gn:]

    tok = lax.broadcasted_iota(I32, (T, 1), 0) + c * T
    dt = jnp.where(tok < n_valid, _softplus(dtr_ref[0] + dtb_ref[...]), 0.0)
    a = dt * (-jnp.exp(alog_ref[...]))
    ri = lax.broadcasted_iota(I32, (T, T), 0)
    ci = lax.broadcasted_iota(I32, (T, T), 1)
    causal = ci <= ri
    tril = jnp.where(causal, 1.0, 0.0).astype(F32)
    acum = _dot_sel(tril, a)
    eye = jnp.where(lax.broadcasted_iota(I32, (LANES, LANES), 0) == lax.broadcasted_iota(I32, (LANES, LANES), 1),
                    1.0, 0.0).astype(F32)
    acum_t = _dot_sel(eye, acum, _NT)
    a_last = acum[T - 1:T, :]
    e = e_ref[...]
    xdt = xs * _dot_sel(e, dt, sel_left=False)
    eacum_x = jnp.exp(_dot_sel(e, acum, sel_left=False))
    xdend = xdt * jnp.exp(_dot_sel(e, a_last - acum, sel_left=False))

    r = SSD_HEADS // SSD_GROUPS
    gw = r * SSD_HEADDIM
    y_groups = []
    for g in range(SSD_GROUPS):
        bg = bm[:, g * SSD_STATE:(g + 1) * SSD_STATE]
        cg = cm[:, g * SSD_STATE:(g + 1) * SSD_STATE]
        cb_mat = _dot_nt(cg, bg, hi)
        ys = []
        for j in range(r):
            h = g * r + j
            seg = acum[:, h:h + 1] - acum_t[h:h + 1, :]
            lm = jnp.where(causal, jnp.exp(jnp.minimum(seg, 0.0)), 0.0)
            ys.append(_dot(cb_mat * lm, xdt[:, h * SSD_HEADDIM:(h + 1) * SSD_HEADDIM], hi))
        y_diag = jnp.concatenate(ys, axis=-1)
        s_g = state_sc[g * gw:(g + 1) * gw, :]
        y_off = _dot_nt(cg, s_g, hi) * eacum_x[:, g * gw:(g + 1) * gw]
        y_groups.append(y_diag + y_off)
        cs = _dot(jnp.transpose(xdend[:, g * gw:(g + 1) * gw]), bg, hi)
        dec = jnp.concatenate(
            [jnp.broadcast_to(jnp.exp(acum_t[g * r + j:g * r + j + 1, T - 1:T]), (SSD_HEADDIM, SSD_STATE))
             for j in range(r)], axis=0)
        state_sc[g * gw:(g + 1) * gw, :] = s_g * dec + cs
    y = jnp.concatenate(y_groups, axis=-1)
    y = y + xs * dsk_ref[...]
    y = y * _silu(z_ref[0])
    outs = []
    for g in range(SSD_GROUPS):
        yg = y[:, g * gw:(g + 1) * gw]
        outs.append(yg * lax.rsqrt(jnp.mean(yg * yg, axis=-1, keepdims=True) + EPS))
    y_ref[0] = (jnp.concatenate(outs, axis=-1) * nrm_ref[...]).astype(y_ref.dtype)

    @pl.when(c == nc - 1)
    def _():
        sout_ref[0] = state_sc[...]


def _ssd(z, xbc, dtr, cst, s0, cw, cb, dtb, alog, dsk, nrm, e, T, n_valid, hi):
    B, Lp, _ = z.shape
    hp = SSD_HEADS * SSD_HEADDIM
    row = lambda b, c: (0, 0)
    return pl.pallas_call(
        functools.partial(_ssd_kernel, T=T, n_valid=n_valid, hi=hi),
        grid=(B, Lp // T),
        in_specs=[pl.BlockSpec((1, T, SSD_D_INNER), lambda b, c: (b, c, 0)),
                  pl.BlockSpec((1, T, SSD_XBC), lambda b, c: (b, c, 0)),
                  pl.BlockSpec((1, T, LANES), lambda b, c: (b, c, 0)),
                  pl.BlockSpec((1, SSD_CONV - 1, SSD_XBC), lambda b, c: (b, 0, 0)),
                  pl.BlockSpec((1, hp, SSD_STATE), lambda b, c: (b, 0, 0)),
                  pl.BlockSpec((SSD_CONV, SSD_XBC), row),
                  pl.BlockSpec((1, SSD_XBC), row),
                  pl.BlockSpec((1, LANES), row),
                  pl.BlockSpec((1, LANES), row),
                  pl.BlockSpec((1, SSD_D_INNER), row),
                  pl.BlockSpec((1, SSD_D_INNER), row),
                  pl.BlockSpec((LANES, SSD_D_INNER), row)],
        out_specs=[pl.BlockSpec((1, T, SSD_D_INNER), lambda b, c: (b, c, 0)),
                   pl.BlockSpec((1, hp, SSD_STATE), lambda b, c: (b, 0, 0)),
                   pl.BlockSpec((1, SSD_CONV - 1, SSD_XBC), lambda b, c: (b, 0, 0))],
        out_shape=[jax.ShapeDtypeStruct((B, Lp, SSD_D_INNER), _act_dtype(hi)),
                   jax.ShapeDtypeStruct((B, hp, SSD_STATE), F32),
                   jax.ShapeDtypeStruct((B, SSD_CONV - 1, SSD_XBC), F32)],
        scratch_shapes=[pltpu.VMEM((hp, SSD_STATE), F32),
                        pltpu.VMEM((T + SUBLANES, SSD_XBC), F32)],
        compiler_params=_cparams(("parallel", "arbitrary")),
        name="ssd_scan",
    )(z, xbc, dtr, cst, s0, cw, cb, dtb, alog, dsk, nrm, e)


def _latkv_kernel(lat_ref, g_ref, cos_ref, sa_ref, sb_ref, ckv_ref, kpe_ref):
    lat = lat_ref[0]
    cr = lat[:, :MLA_KV_RANK]
    ckv_ref[0] = cr * lax.rsqrt(jnp.mean(cr * cr, axis=-1, keepdims=True) + EPS) * g_ref[...]
    kpe_ref[0] = _rope_rot(lat[:, MLA_KV_RANK:], cos_ref[...], sa_ref[...], sb_ref[...])


def _latkv_post(latkv, g, cos_t, sin_a, sin_b, tm):
    B, L, wp = latkv.shape
    tm = min(tm, L)
    tab = pl.BlockSpec((tm, LANES), lambda b, i: (i, 0))
    return pl.pallas_call(
        _latkv_kernel,
        grid=(B, L // tm),
        in_specs=[pl.BlockSpec((1, tm, wp), lambda b, i: (b, i, 0)),
                  pl.BlockSpec((1, MLA_KV_RANK), lambda b, i: (0, 0)), tab, tab, tab],
        out_specs=[pl.BlockSpec((1, tm, MLA_KV_RANK), lambda b, i: (b, i, 0)),
                   pl.BlockSpec((1, tm, LANES), lambda b, i: (b, i, 0))],
        out_shape=[jax.ShapeDtypeStruct((B, L, MLA_KV_RANK), F32),
                   jax.ShapeDtypeStruct((B, L, LANES), F32)],
        compiler_params=_cparams(("parallel", "parallel")),
        name="mla_latent_kv",
    )(latkv, g, cos_t, sin_a, sin_b)


def _q_kernel(lat_ref, g_ref, w_ref, cos_ref, sa_ref, sb_ref, q_ref, *, scale, hi):
    lat = lat_ref[0]
    n = lat * lax.rsqrt(jnp.mean(lat * lat, axis=-1, keepdims=True) + EPS) * g_ref[...]
    cos_t, sin_a, sin_b = cos_ref[...], sa_ref[...], sb_ref[...]
    nope = lax.broadcasted_iota(I32, (lat.shape[0], LANES), 1) < ROPE_LANE
    for h in range(MLA_HEADS):
        q = _dot(n, w_ref[h], hi)
        p = _rope_rot(jnp.where(nope, 0.0, q), cos_t, sin_a, sin_b)
        q_ref[0, h] = (jnp.where(nope, q, p) * scale).astype(q_ref.dtype)


def _q_proj(latq, g, wq, cos_t, sin_a, sin_b, tm, hi):
    B, L, _ = latq.shape
    tm = min(tm, L)
    tab = pl.BlockSpec((tm, LANES), lambda b, i: (i, 0))
    scale = (MLA_NOPE + MLA_ROPE) ** -0.5 * math.log2(math.e)
    return pl.pallas_call(
        functools.partial(_q_kernel, scale=scale, hi=hi),
        grid=(B, L // tm),
        in_specs=[pl.BlockSpec((1, tm, MLA_Q_RANK), lambda b, i: (b, i, 0)),
                  pl.BlockSpec((1, MLA_Q_RANK), lambda b, i: (0, 0)),
                  pl.BlockSpec((MLA_HEADS, MLA_Q_RANK, LANES), lambda b, i: (0, 0, 0)), tab, tab, tab],
        out_specs=pl.BlockSpec((1, MLA_HEADS, tm, LANES), lambda b, i: (b, 0, i, 0)),
        out_shape=jax.ShapeDtypeStruct((B, MLA_HEADS, L, LANES), _act_dtype(hi)),
        compiler_params=_cparams(("parallel", "parallel")),
        name="mla_q_proj",
    )(latq, g, wq, cos_t, sin_a, sin_b)


def _kvup_kernel(ckv_ref, kpe_ref, w_ref, kv_ref, *, hi):
    ckv = ckv_ref[0]
    kpe = kpe_ref[0]
    nope = lax.broadcasted_iota(I32, kpe.shape, 1) < ROPE_LANE
    for h in range(MLA_HEADS):
        kv = _dot(ckv, w_ref[h], hi)
        key = jnp.where(nope, kv, kpe)
        val = jnp.where(nope, pltpu.roll(kv, MLA_V, 1), 0.0)
        kv_ref[0, h] = jnp.concatenate([key, val], axis=-1).astype(kv_ref.dtype)


def _kv_up(ckv, kpe, wkv, tm, hi):
    B, Lk, _ = ckv.shape
    tm = min(tm, Lk)
    return pl.pallas_call(
        functools.partial(_kvup_kernel, hi=hi),
        grid=(B, Lk // tm),
        in_specs=[pl.BlockSpec((1, tm, MLA_KV_RANK), lambda b, i: (b, i, 0)),
                  pl.BlockSpec((1, tm, LANES), lambda b, i: (b, i, 0)),
                  pl.BlockSpec((MLA_HEADS, MLA_KV_RANK, LANES), lambda b, i: (0, 0, 0))],
        out_specs=pl.BlockSpec((1, MLA_HEADS, tm, 2 * LANES), lambda b, i: (b, 0, i, 0)),
        out_shape=jax.ShapeDtypeStruct((B, MLA_HEADS, Lk, 2 * LANES), _act_dtype(hi)),
        compiler_params=_cparams(("parallel", "parallel")),
        name="mla_kv_up",
    )(ckv, kpe, wkv)


def _attn_kernel(q_ref, kv_ref, o_ref, m_sc, l_sc, acc_sc, *, tq, tk, tkm, n_sub, q_off, kv_len, hi):
    q0 = pl.program_id(2) * tq
    first = q_off + q0

    def seen_by_all(f):
        return jnp.minimum(kv_len, (f // CHUNK + 1) * CHUNK)

    n_full = seen_by_all(first) // tk
    m_lo = n_full * (tk // tkm)

    def block(r0, rows, k0, width, masked):
        hds = range(2)
        rs = slice(r0, r0 + rows)
        k = [kv_ref[0, hd, pl.ds(k0, width), :] for hd in hds]
        s = [_dot_nt(q_ref[0, hd, rs, :], k[hd][:, :LANES], hi) for hd in hds]
        if masked:
            qchunk = jnp.right_shift(first + r0 + lax.broadcasted_iota(I32, (rows, 1), 0), CHUNK_SHIFT)
            kpos = k0 + lax.broadcasted_iota(I32, (1, width), 1)
            vis = jnp.logical_and(jnp.right_shift(kpos, CHUNK_SHIFT) <= qchunk, kpos < kv_len)
            s = [jnp.where(vis, s_, NEG_BIG) for s_ in s]
        m_prev = [m_sc[hd, rs, :] for hd in hds]
        m_new = [jnp.maximum(m_prev[hd], jnp.max(s[hd], axis=-1, keepdims=True)) for hd in hds]
        alpha = [jnp.exp2(m_prev[hd] - m_new[hd]) for hd in hds]
        p = [jnp.exp2(s[hd] - jnp.tile(m_new[hd], (1, width // LANES))) for hd in hds]
        pv = [_dot(p[hd], k[hd][:, LANES:], hi) for hd in hds]
        for hd in hds:
            l_sc[hd, rs, :] = alpha[hd] * l_sc[hd, rs, :] + jnp.sum(p[hd], axis=-1, keepdims=True)
            acc_sc[hd, rs, :] = alpha[hd] * acc_sc[hd, rs, :] + pv[hd]
            m_sc[hd, rs, :] = m_new[hd]

    m_sc[...] = jnp.full((2, tq, LANES), NEG_BIG, F32)
    l_sc[...] = jnp.zeros((2, tq, LANES), F32)
    acc_sc[...] = jnp.zeros((2, tq, LANES), F32)

    def full_body(j, carry):
        block(0, tq, pl.multiple_of(j * tk, tk), tk, False)
        return carry

    lax.fori_loop(0, n_full, full_body, 0)
    rows = tq // n_sub
    for r in range(n_sub):
        f = first + r * rows
        u_hi = jnp.maximum(m_lo, seen_by_all(f) // tkm)
        n_any = jnp.minimum(kv_len, ((f + rows - 1) // CHUNK + 1) * CHUNK)
        m_hi = (n_any + tkm - 1) // tkm

        def open_body(j, carry, r=r):
            block(r * rows, rows, pl.multiple_of(j * tkm, tkm), tkm, False)
            return carry

        def masked_body(j, carry, r=r):
            block(r * rows, rows, pl.multiple_of(j * tkm, tkm), tkm, True)
            return carry

        lax.fori_loop(m_lo, u_hi, open_body, 0)
        lax.fori_loop(u_hi, m_hi, masked_body, 0)
    outs = [acc_sc[hd] / l_sc[hd] for hd in range(2)]
    lane = lax.broadcasted_iota(I32, (tq, LANES), 1)
    o = jnp.where(lane < MLA_V, outs[0], pltpu.roll(outs[1], MLA_V, 1))
    o_ref[0] = o.astype(o_ref.dtype)


def _attention(q, kv, tq, tk, tkm, q_off, kv_len, hi):
    B, H, L, _ = q.shape
    Lk = kv.shape[2]
    n_sub = 2 if tq >= 2 * tkm else 1
    return pl.pallas_call(
        functools.partial(_attn_kernel, tq=tq, tk=tk, tkm=tkm, n_sub=n_sub, q_off=q_off, kv_len=kv_len, hi=hi),
        grid=(B, H // 2, L // tq),
        in_specs=[pl.BlockSpec((1, 2, tq, LANES), lambda b, h, i: (b, h, i, 0)),
                  pl.BlockSpec((1, 2, Lk, 2 * LANES), lambda b, h, i: (b, h, 0, 0))],
        out_specs=pl.BlockSpec((1, tq, LANES), lambda b, h, i: (b, i, h)),
        out_shape=jax.ShapeDtypeStruct((B, L, H * MLA_V), _act_dtype(hi)),
        scratch_shapes=[pltpu.VMEM((2, tq, LANES), F32)] * 3,
        compiler_params=_cparams(("parallel", "parallel", "parallel")),
        name="mla_attention",
    )(q, kv)


def _attn_latent_kernel(q_ref, ckv_ref, kpe_ref, wuk_ref, wuv_ref, o_ref, *, L, q_off, kv_len, hi):
    H = MLA_HEADS
    lk = ckv_ref.shape[1]
    ckv = ckv_ref[0]
    kpe = kpe_ref[0]
    qa = jnp.concatenate([_dot(q_ref[0, h], wuk_ref[h], hi) for h in range(H)], axis=0)
    qp = jnp.concatenate([q_ref[0, h] for h in range(H)], axis=0)
    s = _dot_nt(qa, ckv, hi) + _dot_nt(qp, kpe, hi)
    row = lax.broadcasted_iota(I32, (H * L, 1), 0)
    qchunk = jnp.right_shift(q_off + jnp.bitwise_and(row, L - 1), CHUNK_SHIFT)
    kpos = lax.broadcasted_iota(I32, (1, lk), 1)
    vis = jnp.logical_and(jnp.right_shift(kpos, CHUNK_SHIFT) <= qchunk, kpos < kv_len)
    s = jnp.where(vis, s, NEG_BIG)
    p = jnp.exp2(s - jnp.max(s, axis=-1, keepdims=True))
    lat = _dot(p, ckv, hi) / jnp.sum(p, axis=-1, keepdims=True)
    outs = [_dot(lat[h * L:(h + 1) * L], wuv_ref[h], hi)[:, :MLA_V] for h in range(H)]
    o_ref[0] = jnp.concatenate(outs, axis=-1).astype(o_ref.dtype)


def _attention_latent(q, ckv, kpe, wuk, wuv, q_off, kv_len, hi):
    B, H, L, _ = q.shape
    lk = ckv.shape[1]
    return pl.pallas_call(
        functools.partial(_attn_latent_kernel, L=L, q_off=q_off, kv_len=kv_len, hi=hi),
        grid=(B,),
        in_specs=[pl.BlockSpec((1, H, L, LANES), lambda b: (b, 0, 0, 0)),
                  pl.BlockSpec((1, lk, MLA_KV_RANK), lambda b: (b, 0, 0)),
                  pl.BlockSpec((1, lk, LANES), lambda b: (b, 0, 0)),
                  pl.BlockSpec((H, LANES, MLA_KV_RANK), lambda b: (0, 0, 0)),
                  pl.BlockSpec((H, MLA_KV_RANK, LANES), lambda b: (0, 0, 0))],
        out_specs=pl.BlockSpec((1, L, H * MLA_V), lambda b: (b, 0, 0)),
        out_shape=jax.ShapeDtypeStruct((B, L, H * MLA_V), _act_dtype(hi)),
        compiler_params=_cparams(("parallel",)),
        name="mla_attention_latent",
    )(q, ckv, kpe, wuk, wuv)


def _gdn_kernel(qkv_ref, gate_ref, ba_ref, cst_ref, s0_ref, cw_ref, dtb_ref, alog_ref, nrm_ref,
                o_ref, sout_ref, cout_ref, state_sc, cbuf, *, bb_n, **kw):
    for b in range(bb_n):
        _gdn_stream(qkv_ref.at[b], gate_ref.at[b], ba_ref.at[b], cst_ref.at[b], s0_ref.at[b], cw_ref, dtb_ref,
                    alog_ref, nrm_ref, o_ref.at[b], sout_ref.at[b], cout_ref.at[b], state_sc.at[b], cbuf.at[b], **kw)


def _gdn_stream(qkv_ref, gate_ref, ba_ref, cst_ref, s0_ref, cw_ref, dtb_ref, alog_ref, nrm_ref,
                o_ref, sout_ref, cout_ref, state_sc, cbuf, *, T, n_valid, hi):
    c = pl.program_id(1)
    nc = pl.num_programs(1)
    halo = GDN_CONV - 1
    base = SUBLANES - halo
    H = GDN_HEADS
    DK = GDN_DK

    @pl.when(c == 0)
    def _():
        state_sc[...] = s0_ref[...]
        cbuf[base:SUBLANES, :] = cst_ref[...]

    cbuf[SUBLANES:SUBLANES + T, :] = qkv_ref[...]
    conv = cw_ref[0:1, :] * cbuf[base:base + T, :]
    for k in range(1, GDN_CONV):
        conv = conv + cw_ref[k:k + 1, :] * cbuf[base + k:base + k + T, :]
    qkv = _silu(conv)

    c_last = (n_valid - 1) // T
    nv_last = n_valid - c_last * T

    @pl.when(c == c_last)
    def _():
        cout_ref[...] = cbuf[base + nv_last:base + nv_last + halo, :]

    cbuf[base:SUBLANES, :] = cbuf[base + T:SUBLANES + T, :]

    tok = lax.broadcasted_iota(I32, (T, 1), 0) + c * T
    valid = tok < n_valid
    ba = ba_ref[...]
    beta = jnp.where(valid, _sigmoid(ba), 0.0)
    g = jnp.where(valid, -jnp.exp(alog_ref[...]) * _softplus(ba + dtb_ref[...]), 0.0)
    ri = lax.broadcasted_iota(I32, (T, T), 0)
    ci = lax.broadcasted_iota(I32, (T, T), 1)
    tril = jnp.where(ci <= ri, 1.0, 0.0).astype(F32)
    gc = _dot_sel(tril, g)

    def l2n(x):
        return x * lax.rsqrt(jnp.sum(x * x, axis=-1, keepdims=True) + EPS)

    qn = [l2n(qkv[:, h * DK:(h + 1) * DK]) * (DK ** -0.5) for h in range(H)]
    kn = [l2n(qkv[:, (H + h) * DK:(H + h + 1) * DK]) for h in range(H)]
    vv = [qkv[:, (2 * H + h) * DK:(2 * H + h + 1) * DK] for h in range(H)]

    G = 2
    W = G * T
    bi = lax.broadcasted_iota(I32, (W, W), 0)
    bj = lax.broadcasted_iota(I32, (W, W), 1)
    t_shift = T.bit_length() - 1
    same = jnp.right_shift(bi, t_shift) == jnp.right_shift(bj, t_shift)
    strict = jnp.logical_and(same, bj < bi)
    incl = jnp.logical_and(same, bj <= bi)
    eye_w = jnp.where(bi == bj, 1.0, 0.0).astype(F32)
    rowhead = jnp.right_shift(lax.broadcasted_iota(I32, (W, DK), 0), t_shift)

    groups = [[grp * G + j for j in range(G)] for grp in range(H // G)]
    n_g = len(groups)
    kst = [jnp.concatenate([kn[h] for h in hs], axis=0) for hs in groups]
    qst = [jnp.concatenate([qn[h] for h in hs], axis=0) for hs in groups]
    vst = [jnp.concatenate([vv[h] for h in hs], axis=0) for hs in groups]
    cb = [jnp.concatenate([jnp.broadcast_to(gc[:, H + h:H + h + 1], (T, W)) for h in hs], axis=0)
          for hs in groups]
    bb = [jnp.concatenate([jnp.broadcast_to(beta[:, h:h + 1], (T, DK)) for h in hs], axis=0) for hs in groups]
    glast = [jnp.concatenate([jnp.broadcast_to(gc[T - 1:T, H + h:H + h + 1], (T, DK)) for h in hs], axis=0)
             for hs in groups]
    dec = [jnp.exp(jnp.minimum(c_ - jnp.transpose(c_), 0.0)) for c_ in cb]
    kk = [_dot_nt(k_, k_, hi) for k_ in kst]
    qk = [_dot_nt(q_, k_, hi) for q_, k_ in zip(qst, kst)]
    a_mat = [jnp.where(strict, jnp.tile(b_, (1, W // DK)) * kk_ * d_, 0.0) for b_, kk_, d_ in zip(bb, kk, dec)]
    qkm = [jnp.where(incl, qk_ * d_, 0.0) for qk_, d_ in zip(qk, dec)]
    p_mat = [eye_w - a_ for a_ in a_mat]
    a_pow = a_mat
    for _ in range(T.bit_length() - 2):
        a_pow = [_dot_split(a_, a_) for a_ in a_pow]
        p_mat = [p_ + _dot_split(p_, a_) for p_, a_ in zip(p_mat, a_pow)]
    cbk = [c_[:, :DK] for c_ in cb]
    eg = [jnp.exp(c_) for c_ in cbk]
    rhs = [jnp.concatenate([v_ * b_, k_ * b_ * e_], axis=-1) for v_, b_, k_, e_ in zip(vst, bb, kst, eg)]
    sol = [_dot_split(p_, r_) for p_, r_ in zip(p_mat, rhs)]
    qdec = [q_ * e_ for q_, e_ in zip(qst, eg)]
    kdec_t = [jnp.transpose(k_ * jnp.exp(g_ - c_)) for k_, g_, c_ in zip(kst, glast, cbk)]
    s_old = [state_sc[h * DK:(h + 1) * DK, :] for h in range(H)]
    ws = [[_dot(jnp.concatenate([sol[g][j * T:(j + 1) * T, DK:], qdec[g][j * T:(j + 1) * T]], axis=0),
                s_old[h], hi) for j, h in enumerate(groups[g])] for g in range(n_g)]
    vnew_st = [jnp.concatenate([sol[g][j * T:(j + 1) * T, :DK] - ws[g][j][:T] for j in range(G)], axis=0)
               for g in range(n_g)]
    o_st = [jnp.concatenate([ws[g][j][T:] for j in range(G)], axis=0) + _dot(qkm[g], vnew_st[g], hi)
            for g in range(n_g)]
    o_heads = [None] * H
    for g in range(n_g):
        for j, h in enumerate(groups[g]):
            vm = jnp.where(rowhead == j, vnew_st[g], 0.0)
            last = jnp.exp(jnp.broadcast_to(gc[T - 1:T, H + h:H + h + 1], (DK, GDN_DV)))
            state_sc[h * DK:(h + 1) * DK, :] = s_old[h] * last + _dot(kdec_t[g], vm, hi)
            o_heads[h] = o_st[g][j * T:(j + 1) * T]
    gate = gate_ref[...]
    outs = []
    for h in range(H):
        oh = o_heads[h]
        oh = oh * lax.rsqrt(jnp.mean(oh * oh, axis=-1, keepdims=True) + EPS) * nrm_ref[...]
        outs.append(oh * _silu(gate[:, h * GDN_DV:(h + 1) * GDN_DV]))
    o_ref[...] = jnp.concatenate(outs, axis=-1).astype(o_ref.dtype)

    @pl.when(c == nc - 1)
    def _():
        sout_ref[...] = state_sc[...]


def _gdn(qkv, gate, ba, cst, s0, cw, dtb, alog, nrm, T, n_valid, hi):
    B, Lp, _ = qkv.shape
    hk = GDN_HEADS * GDN_DK
    bb_n = 1
    row = lambda b, c: (0, 0)
    return pl.pallas_call(
        functools.partial(_gdn_kernel, bb_n=bb_n, T=T, n_valid=n_valid, hi=hi),
        grid=(B // bb_n, Lp // T),
        in_specs=[pl.BlockSpec((bb_n, T, GDN_QKV), lambda b, c: (b, c, 0)),
                  pl.BlockSpec((bb_n, T, GDN_HEADS * GDN_DV), lambda b, c: (b, c, 0)),
                  pl.BlockSpec((bb_n, T, LANES), lambda b, c: (b, c, 0)),
                  pl.BlockSpec((bb_n, GDN_CONV - 1, GDN_QKV), lambda b, c: (b, 0, 0)),
                  pl.BlockSpec((bb_n, hk, GDN_DV), lambda b, c: (b, 0, 0)),
                  pl.BlockSpec((GDN_CONV, GDN_QKV), row),
                  pl.BlockSpec((1, LANES), row),
                  pl.BlockSpec((1, LANES), row),
                  pl.BlockSpec((1, GDN_DV), row)],
        out_specs=[pl.BlockSpec((bb_n, T, GDN_HEADS * GDN_DV), lambda b, c: (b, c, 0)),
                   pl.BlockSpec((bb_n, hk, GDN_DV), lambda b, c: (b, 0, 0)),
                   pl.BlockSpec((bb_n, GDN_CONV - 1, GDN_QKV), lambda b, c: (b, 0, 0))],
        out_shape=[jax.ShapeDtypeStruct((B, Lp, GDN_HEADS * GDN_DV), _act_dtype(hi)),
                   jax.ShapeDtypeStruct((B, hk, GDN_DV), F32),
                   jax.ShapeDtypeStruct((B, GDN_CONV - 1, GDN_QKV), F32)],
        scratch_shapes=[pltpu.VMEM((bb_n, hk, GDN_DV), F32),
                        pltpu.VMEM((bb_n, T + SUBLANES, GDN_QKV), F32)],
        compiler_params=_cparams(("parallel", "arbitrary")),
        name="gdn_scan",
    )(qkv, gate, ba, cst, s0, cw, dtb, alog, nrm)


def _sconv_kernel(b_ref, c_ref, v_ref, cst_ref, w_ref, o_ref, cout_ref, cbuf, *, T):
    i = pl.program_id(1)
    halo = SC_CONV - 1
    base = SUBLANES - halo

    @pl.when(i == 0)
    def _():
        cbuf[base:SUBLANES, :] = cst_ref[0]

    cbuf[SUBLANES:SUBLANES + T, :] = c_ref[0] * v_ref[0]
    conv = w_ref[0:1, :] * cbuf[base:base + T, :]
    for k in range(1, SC_CONV):
        conv = conv + w_ref[k:k + 1, :] * cbuf[base + k:base + k + T, :]
    o_ref[0] = (b_ref[0] * conv).astype(o_ref.dtype)
    cout_ref[0] = cbuf[base + T:SUBLANES + T, :]
    cbuf[base:SUBLANES, :] = cbuf[base + T:SUBLANES + T, :]


def _sconv(scb, scc, scv, cst, w, T, hi):
    B, L, _ = scb.shape
    T = min(T, L)
    blk = pl.BlockSpec((1, T, SC_WIDTH), lambda b, i: (b, i, 0))
    st = pl.BlockSpec((1, SC_CONV - 1, SC_WIDTH), lambda b, i: (b, 0, 0))
    return pl.pallas_call(
        functools.partial(_sconv_kernel, T=T),
        grid=(B, L // T),
        in_specs=[blk, blk, blk, st, pl.BlockSpec((SC_CONV, SC_WIDTH), lambda b, i: (0, 0))],
        out_specs=[blk, st],
        out_shape=[jax.ShapeDtypeStruct((B, L, SC_WIDTH), _act_dtype(hi)),
                   jax.ShapeDtypeStruct((B, SC_CONV - 1, SC_WIDTH), F32)],
        scratch_shapes=[pltpu.VMEM((T + SUBLANES, SC_WIDTH), F32)],
        compiler_params=_cparams(("parallel", "arbitrary")),
        name="short_conv",
    )(scb, scc, scv, cst, w)


def _out_kernel(y_ref, o_ref, x_ref, mod_ref, wy_ref, wo_ref, g_ref, wr_ref, br_ref,
                xn_ref, h_ref, lg_ref, *, hi):
    mix = _dot(y_ref[0], wy_ref[...], hi) + _dot(o_ref[0], wo_ref[...], hi)
    xn = x_ref[0] + mod_ref[0, 2:3, :] * mix
    xn_ref[0] = xn
    h = xn * lax.rsqrt(jnp.mean(xn * xn, axis=-1, keepdims=True) + EPS) * g_ref[...]
    h = h * (1.0 + mod_ref[0, 4:5, :]) + mod_ref[0, 3:4, :]
    h_ref[0] = h
    lg_ref[0] = _dot_split(h, wr_ref[...]) + br_ref[...]


def _out_proj(y, o, x, mod, wy, wo, g, wr, br, tm, hi):
    B, L, _ = x.shape
    tm = min(tm, L)
    blk = lambda dt_w: pl.BlockSpec((1, tm, dt_w), lambda b, i: (b, i, 0))
    full = lambda s: pl.BlockSpec(s, lambda b, i: (0, 0))
    return pl.pallas_call(
        functools.partial(_out_kernel, hi=hi),
        grid=(B, L // tm),
        in_specs=[blk(D_MODEL), blk(D_MODEL), blk(D_MODEL),
                  pl.BlockSpec((1, 6, D_MODEL), lambda b, i: (b, 0, 0)),
                  full((D_MODEL, D_MODEL)), full((D_MODEL, D_MODEL)), full((1, D_MODEL)),
                  full((D_MODEL, LANES)), full((1, LANES))],
        out_specs=[blk(D_MODEL), blk(D_MODEL), blk(LANES)],
        out_shape=[jax.ShapeDtypeStruct((B, L, D_MODEL), F32),
                   jax.ShapeDtypeStruct((B, L, D_MODEL), F32),
                   jax.ShapeDtypeStruct((B, L, LANES), F32)],
        compiler_params=_cparams(("parallel", "parallel")),
        name="out_proj",
    )(y, o, x, mod, wy, wo, g, wr, br)


def _route_kernel(lg_ref, e_ref, rank_ref, gate_ref, cnt_ref, base_sc, *, tm):
    i = pl.program_id(0)

    @pl.when(i == 0)
    def _():
        base_sc[...] = jnp.zeros_like(base_sc)

    lg = lg_ref[...]
    lane_i = lax.broadcasted_iota(I32, (tm, LANES), 1)
    lane = lane_i.astype(F32)
    vals, idxs = [], []
    cur = lg
    for _ in range(TOP_K):
        m = jnp.max(cur, axis=-1, keepdims=True)
        idx = jnp.min(jnp.where(cur == m, lane, float(LANES)), axis=-1, keepdims=True)
        vals.append(m)
        idxs.append(idx)
        cur = jnp.where(lane == idx, -jnp.inf, cur)
    ex = [jnp.exp(v - vals[0]) for v in vals]
    den = ex[0] + ex[1] + ex[2] + ex[3]
    onehot = jnp.zeros((tm, LANES), F32)
    for idx in idxs:
        onehot = onehot + jnp.where(lane == idx, 1.0, 0.0)
    ri = lax.broadcasted_iota(I32, (tm, tm), 0)
    ci = lax.broadcasted_iota(I32, (tm, tm), 1)
    before = _dot(jnp.where(ci < ri, 1.0, 0.0), onehot) + base_sc[...]
    e_out = jnp.zeros((tm, LANES), I32)
    r_out = jnp.zeros((tm, LANES), I32)
    g_out = jnp.zeros((tm, LANES), F32)
    for k in range(TOP_K):
        rk = jnp.sum(jnp.where(lane == idxs[k], before, 0.0), axis=-1, keepdims=True)
        e_out = jnp.where(lane_i == k, idxs[k].astype(I32), e_out)
        r_out = jnp.where(lane_i == k, rk.astype(I32), r_out)
        g_out = jnp.where(lane_i == k, ex[k] / den, g_out)
    e_ref[...] = e_out
    rank_ref[...] = r_out
    gate_ref[...] = g_out
    base_sc[...] = base_sc[...] + jnp.sum(onehot, axis=0, keepdims=True)
    cnt_ref[...] = base_sc[...].astype(I32)


def _route(logits, tm):
    n_tok = logits.shape[0]
    tm = min(tm, n_tok)
    blk = pl.BlockSpec((tm, LANES), lambda i: (i, 0))
    return pl.pallas_call(
        functools.partial(_route_kernel, tm=tm),
        grid=(n_tok // tm,),
        in_specs=[blk],
        out_specs=[blk, blk, blk, pl.BlockSpec((1, LANES), lambda i: (0, 0))],
        out_shape=[jax.ShapeDtypeStruct((n_tok, LANES), I32),
                   jax.ShapeDtypeStruct((n_tok, LANES), I32),
                   jax.ShapeDtypeStruct((n_tok, LANES), F32),
                   jax.ShapeDtypeStruct((1, LANES), I32)],
        scratch_shapes=[pltpu.VMEM((1, LANES), F32)],
        compiler_params=_cparams(("arbitrary",)),
        name="moe_route",
    )(logits)


def _dispatch_kernel(zs_ref, nz_ref, dest_ref, h_ref, out_hbm, zbuf, sem, zsem, *, tm, n_blocks):
    bm = zbuf.shape[0]

    def zero_copy(row0):
        return pltpu.make_async_copy(zbuf, out_hbm.at[pl.ds(row0, bm), :], zsem)

    @pl.when(pl.program_id(0) == 0)
    def _():
        zbuf[...] = jnp.zeros_like(zbuf)
        n_used = nz_ref[0]
        for e in range(N_EXPERTS):
            @pl.when(zs_ref[e] >= 0)
            def _(e=e):
                zero_copy(pl.multiple_of(zs_ref[e], bm)).start()

        def tail(j, carry):
            zero_copy(pl.multiple_of(j * bm, bm)).start()
            return carry

        lax.fori_loop(n_used, n_blocks, tail, 0)

        def drain(j, carry):
            zero_copy(0).wait()
            return carry

        lax.fori_loop(0, nz_ref[1], drain, 0)

    def issue(r, carry):
        for k in range(TOP_K):
            d = dest_ref[r * TOP_K + k]
            pltpu.make_async_copy(h_ref.at[pl.ds(r, 1), :], out_hbm.at[pl.ds(d, 1), :], sem).start()
        return carry

    lax.fori_loop(0, tm, issue, 0)
    for _ in range(TOP_K):
        pltpu.make_async_copy(h_ref, out_hbm.at[pl.ds(0, tm), :], sem).wait()


def _dispatch(h, dest_flat, zero_start, zero_counts, n_rows, tm, bm):
    n_tok = h.shape[0]
    tm = min(tm, n_tok)
    grid_spec = pltpu.PrefetchScalarGridSpec(
        num_scalar_prefetch=2,
        grid=(n_tok // tm,),
        in_specs=[pl.BlockSpec((tm * TOP_K,), lambda i, zs, nz: (i,), memory_space=pltpu.SMEM),
                  pl.BlockSpec((tm, D_MODEL), lambda i, zs, nz: (i, 0))],
        out_specs=pl.BlockSpec(memory_space=pl.ANY),
        scratch_shapes=[pltpu.VMEM((bm, D_MODEL), F32), pltpu.SemaphoreType.DMA(()),
                        pltpu.SemaphoreType.DMA(())],
    )
    return pl.pallas_call(
        functools.partial(_dispatch_kernel, tm=tm, n_blocks=n_rows // bm),
        grid_spec=grid_spec,
        out_shape=jax.ShapeDtypeStruct((n_rows, D_MODEL), F32),
        compiler_params=_cparams(("arbitrary",)),
        name="moe_dispatch",
    )(zero_start, zero_counts, dest_flat, h)


def _ffn_kernel(blk_e_ref, nused_ref, x_ref, wgu_ref, bgu_ref, wd_ref, bd_ref, o_ref, wgu_sc, wd_sc):
    i = pl.program_id(0)
    prev = blk_e_ref[jnp.maximum(i - 1, 0)]
    fresh = jnp.logical_or(i == 0, blk_e_ref[i] != prev)
    active = i < nused_ref[0]

    @pl.when(jnp.logical_and(active, fresh))
    def _():
        wgu_sc[...] = wgu_ref[0, 0].astype(BF16)
        wd_sc[...] = wd_ref[0, 0].astype(BF16)

    @pl.when(active)
    def _():
        gu = jnp.dot(x_ref[...].astype(BF16), wgu_sc[...], preferred_element_type=F32) + bgu_ref[0]
        gate = jnp.minimum(gu[:, :D_FF], SWIGLU_LIMIT)
        up = jnp.clip(gu[:, D_FF:], -SWIGLU_LIMIT, SWIGLU_LIMIT)
        act = (up + 1.0) * gate * _sigmoid(SWIGLU_ALPHA * gate)
        o_ref[...] = jnp.dot(act.astype(BF16), wd_sc[...], preferred_element_type=F32) + bd_ref[0]

    @pl.when(jnp.logical_not(active))
    def _():
        o_ref[...] = jnp.zeros_like(o_ref)


def _expert_ffn(xin, blk_e, n_used, w_gu, b_gu, w_down, b_down, layer, bm):
    n_rows = xin.shape[0]
    n_blocks = n_rows // bm

    def row_map(i, be, nu):
        return (jnp.minimum(i, nu[0] - 1), 0)

    def e_map4(i, be, nu):
        return (layer, be[jnp.minimum(i, nu[0] - 1)], 0, 0)

    def e_map3(i, be, nu):
        return (layer * N_EXPERTS + be[jnp.minimum(i, nu[0] - 1)], 0, 0)

    grid_spec = pltpu.PrefetchScalarGridSpec(
        num_scalar_prefetch=2,
        grid=(n_blocks,),
        in_specs=[pl.BlockSpec((bm, D_MODEL), row_map),
                  pl.BlockSpec((1, 1, D_MODEL, 2 * D_FF), e_map4),
                  pl.BlockSpec((1, 1, 2 * D_FF), e_map3),
                  pl.BlockSpec((1, 1, D_FF, D_MODEL), e_map4),
                  pl.BlockSpec((1, 1, D_MODEL), e_map3)],
        out_specs=pl.BlockSpec((bm, D_MODEL), lambda i, be, nu: (i, 0)),
        scratch_shapes=[pltpu.VMEM((D_MODEL, 2 * D_FF), BF16),
                        pltpu.VMEM((D_FF, D_MODEL), BF16)],
    )
    return pl.pallas_call(
        _ffn_kernel,
        grid_spec=grid_spec,
        out_shape=jax.ShapeDtypeStruct((n_rows, D_MODEL), F32),
        compiler_params=_cparams(("arbitrary",)),
        name="moe_expert_ffn",
    )(blk_e, n_used, xin, w_gu, b_gu.reshape(DEPTH * N_EXPERTS, 1, 2 * D_FF), w_down,
      b_down.reshape(DEPTH * N_EXPERTS, 1, D_MODEL))


def _combine_kernel(dest_ref, dest_next_ref, f_hbm, gate_ref, x_ref, mod_ref, g_ref, o_ref, buf, sem, *, tm, final):
    i = pl.program_id(0)
    slot = lax.rem(i, 2)

    def fetch(idx_ref, s):
        def issue(r, carry):
            for k in range(TOP_K):
                d = idx_ref[r * TOP_K + k]
                pltpu.make_async_copy(f_hbm.at[pl.ds(d, 1), :], buf.at[s, k, pl.ds(r, 1), :], sem.at[s]).start()
            return carry

        lax.fori_loop(0, tm, issue, 0)

    @pl.when(i == 0)
    def _():
        fetch(dest_ref, 0)

    @pl.when(i + 1 < pl.num_programs(0))
    def _():
        fetch(dest_next_ref, 1 - slot)

    for k in range(TOP_K):
        pltpu.make_async_copy(f_hbm.at[pl.ds(0, tm), :], buf.at[slot, k], sem.at[slot]).wait()
    gates = gate_ref[...]
    moe = gates[:, 0:1] * buf[slot, 0]
    for k in range(1, TOP_K):
        moe = moe + gates[:, k:k + 1] * buf[slot, k]
    xo = x_ref[...] + mod_ref[0, 5:6, :] * moe
    if final:
        xo = xo * lax.rsqrt(jnp.mean(xo * xo, axis=-1, keepdims=True) + EPS) * g_ref[...]
    o_ref[...] = xo


def _combine(ffn_out, dest_flat, gates, x, mod, g_final, tiles_per_batch, tm, final):
    n_tok = x.shape[0]
    n_tiles = n_tok // tm
    return pl.pallas_call(
        functools.partial(_combine_kernel, tm=tm, final=final),
        grid=(n_tiles,),
        in_specs=[pl.BlockSpec((tm * TOP_K,), lambda i: (i,), memory_space=pltpu.SMEM),
                  pl.BlockSpec((tm * TOP_K,), lambda i: (jnp.minimum(i + 1, n_tiles - 1),),
                               memory_space=pltpu.SMEM),
                  pl.BlockSpec(memory_space=pl.ANY),
                  pl.BlockSpec((tm, LANES), lambda i: (i, 0)),
                  pl.BlockSpec((tm, D_MODEL), lambda i: (i, 0)),
                  pl.BlockSpec((1, 6, D_MODEL), lambda i: (i // tiles_per_batch, 0, 0)),
                  pl.BlockSpec((1, D_MODEL), lambda i: (0, 0))],
        out_specs=pl.BlockSpec((tm, D_MODEL), lambda i: (i, 0)),
        out_shape=jax.ShapeDtypeStruct((n_tok, D_MODEL), F32),
        scratch_shapes=[pltpu.VMEM((2, TOP_K, tm, D_MODEL), F32), pltpu.SemaphoreType.DMA((2,))],
        compiler_params=_cparams(("arbitrary",)),
        name="moe_combine",
    )(dest_flat, dest_flat, ffn_out, gates, x, mod, g_final)


def _pad_cols(w, width):
    return jnp.pad(w, ((0, 0), (0, width - w.shape[1])))


def _pad_lanes(v, offset=0):
    return jnp.pad(v.astype(F32), (offset, LANES - offset - v.shape[0])).reshape(1, LANES)


def _rope_tables(pos):
    half = MLA_ROPE // 2
    inv = ROPE_THETA ** (-jnp.arange(half, dtype=F32) / half)
    ang = pos.astype(F32)[:, None] * inv[None, :]
    cos, sin = jnp.cos(ang), jnp.sin(ang)
    z = jnp.zeros_like(cos)
    lead = jnp.zeros((pos.shape[0], ROPE_LANE), F32)
    pad = jnp.zeros((pos.shape[0], LANES - ROPE_LANE - MLA_ROPE), F32)
    cos_t = jnp.concatenate([lead, cos, cos, pad], axis=1)
    sin_a = jnp.concatenate([lead, -sin, z, pad], axis=1)
    sin_b = jnp.concatenate([lead, z, sin, pad], axis=1)
    return cos_t, sin_a, sin_b


def _pad_seq(t, lp):
    return jnp.pad(t, ((0, 0), (0, lp - t.shape[1]), (0, 0)))


def _even_weights(W, j, dt):
    w_in = W['ev_w_in'][j]
    o1 = SSD_D_INNER
    o2 = o1 + SSD_XBC
    o3 = o2 + SSD_HEADS
    o4 = o3 + MLA_Q_RANK
    w_ssd = jnp.concatenate([w_in[:, :o2], _pad_cols(w_in[:, o2:o3], LANES)], axis=1).astype(dt)
    o5 = o4 + MLA_KV_RANK
    zc = lambda n: jnp.zeros((D_MODEL, n), F32)
    w_mla = jnp.concatenate([w_in[:, o3:o4], w_in[:, o4:o5], zc(ROPE_LANE), w_in[:, o5:],
                             zc(LANES - ROPE_LANE - MLA_ROPE)], axis=1).astype(dt)
    wq = W['mla_w_q_up'][j].reshape(MLA_Q_RANK, MLA_HEADS, MLA_NOPE + MLA_ROPE)
    wq = jnp.pad(wq, ((0, 0), (0, 0), (0, LANES - MLA_NOPE - MLA_ROPE)))
    wq = jnp.transpose(wq, (1, 0, 2)).astype(dt)
    wkv = jnp.transpose(W['mla_w_kv_up'][j].reshape(MLA_KV_RANK, MLA_HEADS, MLA_NOPE + MLA_V),
                        (1, 0, 2)).astype(dt)
    expand = (jnp.arange(LANES)[:, None] == (jnp.arange(SSD_D_INNER) // SSD_HEADDIM)[None, :]).astype(F32)
    wuk = jnp.pad(jnp.transpose(wkv[:, :, :MLA_NOPE], (0, 2, 1)), ((0, 0), (0, LANES - MLA_NOPE), (0, 0)))
    wuv = jnp.pad(wkv[:, :, MLA_NOPE:], ((0, 0), (0, 0), (0, LANES - MLA_V)))
    return dict(
        w_ssd=w_ssd, w_mla=w_mla, wq=wq, wkv=wkv, wuk=wuk, wuv=wuv, expand=expand,
        wy=W['ev_w_out'][j][:SSD_D_INNER].astype(dt), wo=W['ev_w_out'][j][SSD_D_INNER:].astype(dt),
        cw=W['ssd_conv_w'][j], cb=W['ssd_conv_b'][j].reshape(1, SSD_XBC),
        dtb=_pad_lanes(W['ssd_dt_bias'][j]), alog=_pad_lanes(W['ssd_a_log'][j]),
        dsk=jnp.repeat(W['ssd_d'][j].astype(F32), SSD_HEADDIM).reshape(1, SSD_D_INNER),
        nrm=W['ssd_norm'][j].reshape(1, SSD_D_INNER),
        qn=W['mla_q_norm'][j].reshape(1, MLA_Q_RANK), kvn=W['mla_kv_norm'][j].reshape(1, MLA_KV_RANK))


def _odd_weights(W, j, dt):
    w_in = W['od_w_in'][j]
    o1 = GDN_QKV
    o2 = o1 + GDN_HEADS * GDN_DV
    o3 = o2 + 2 * GDN_HEADS
    w_gdn = jnp.concatenate([w_in[:, :o2], _pad_cols(w_in[:, o2:o3], LANES)], axis=1).astype(dt)
    w_sc = w_in[:, o3:].astype(dt)
    return dict(
        w_gdn=w_gdn, w_sc=w_sc,
        wy=W['od_w_out'][j][:GDN_HEADS * GDN_DV].astype(dt), wo=W['od_w_out'][j][GDN_HEADS * GDN_DV:].astype(dt),
        cw=W['gdn_conv_w'][j], dtb=_pad_lanes(W['gdn_dt_bias'][j], GDN_HEADS),
        alog=_pad_lanes(W['gdn_a_log'][j], GDN_HEADS), nrm=W['gdn_norm'][j].reshape(1, GDN_DV),
        scw=W['sconv_w'][j])


def _moe(h, logits, x_new, mod, W, i, g_final, L, final):
    n_tok = h.shape[0]
    bm = MOE_BM if n_tok * TOP_K >= N_EXPERTS * MOE_BM else MOE_BM_SMALL
    e_pad, rank_pad, gates, cnt = _route(logits, 256)
    counts = cnt[0, :N_EXPERTS]
    padded = (counts + bm - 1) // bm * bm
    pend = jnp.cumsum(padded)
    pstart = pend - padded
    e_sel = e_pad[:, :TOP_K]
    dest = (pstart[e_sel] + rank_pad[:, :TOP_K]).astype(I32).reshape(-1)
    n_blocks = n_tok * TOP_K // bm + N_EXPERTS
    n_rows = n_blocks * bm
    blk_start = jnp.arange(n_blocks, dtype=pend.dtype) * bm
    blk_e = jnp.minimum(jnp.sum((blk_start[:, None] >= pend[None, :]).astype(I32), axis=1), N_EXPERTS - 1)
    n_used = (pend[-1:] // bm).astype(I32)
    zero_start = jnp.where(counts > 0, pend - bm, -1).astype(I32)
    n_zero = jnp.sum((counts > 0).astype(I32)) + n_blocks - n_used[0]
    zero_counts = jnp.stack([n_used[0], n_zero]).astype(I32)
    xin = _dispatch(h, dest, zero_start, zero_counts, n_rows, DISPATCH_TM, bm)
    f_out = _expert_ffn(xin, blk_e, n_used, W['w_gu'], W['b_gu'], W['w_down'], W['b_down'], i, bm)
    tm = min(256, L)
    return _combine(f_out, dest, gates, x_new, mod, g_final, L // tm, tm, final)


def _trunk(x, c_mod, pos0, caches, W, PW, seq_t, hi):
    B, L, _ = x.shape
    new = {}
    pos = pos0 + jnp.arange(L, dtype=I32)
    cos_t, sin_a, sin_b = _rope_tables(pos)
    wr_all = W['w_router']
    for i in range(DEPTH):
        mod = c_mod[i]
        g_mix = W['norm_mix'][i].reshape(1, D_MODEL)
        j = i // 2
        if i % 2 == 0:
            P = PW[i]
            z, xbc, dtr = _in_proj(x, mod, g_mix, P['w_ssd'],
                                   ((0, SSD_D_INNER), (SSD_D_INNER, SSD_XBC), (SSD_D_INNER + SSD_XBC, LANES)),
                                   (F32, F32, F32), 0, 1, 512, hi)
            latq, latkv = _in_proj(x, mod, g_mix, P['w_mla'], ((0, MLA_Q_RANK), (MLA_Q_RANK, MLA_Q_RANK)),
                                   (F32, F32), 0, 1, 512, hi)
            T = seq_t['ssd']
            lp = -(-L // T) * T
            y, s_new, cst_new = _ssd(_pad_seq(z, lp), _pad_seq(xbc, lp), _pad_seq(dtr, lp),
                                     caches['ssd_conv'][j], caches['ssd'][j].reshape(B, -1, SSD_STATE),
                                     P['cw'], P['cb'], P['dtb'], P['alog'], P['dsk'], P['nrm'], P['expand'], T, L, hi)
            y = y[:, :L]
            new['ssd'] = s_new.reshape(1, B, SSD_HEADS, SSD_HEADDIM, SSD_STATE)
            new['ssd_conv'] = cst_new[None]
            ckv_new, kpe_new = _latkv_post(latkv, P['kvn'], cos_t, sin_a, sin_b, 512)
            new['mla_ckv'] = ckv_new[None]
            new['mla_krope'] = kpe_new[None, :, :, ROPE_LANE:ROPE_LANE + MLA_ROPE]
            ckv_past, kpe_past = caches['mla_ckv'][j], caches['mla_krope'][j]
            past = ckv_past.shape[1]
            kv_len = past + L
            tk = seq_t['tk']
            lk = -(-kv_len // tk) * tk
            ckv_all = _pad_seq(jnp.concatenate([ckv_past, ckv_new], axis=1), lk)
            kpe_all = _pad_seq(jnp.concatenate(
                [jnp.pad(kpe_past, ((0, 0), (0, 0), (ROPE_LANE, LANES - ROPE_LANE - MLA_ROPE))), kpe_new],
                axis=1), lk)
            q = _q_proj(latq, P['qn'], P['wq'], cos_t, sin_a, sin_b, 512, hi)
            if seq_t['latent']:
                o = _attention_latent(q, ckv_all, kpe_all, P['wuk'], P['wuv'], pos0, kv_len, hi)
            else:
                kv = _kv_up(ckv_all, kpe_all, P['wkv'], min(tk, 512), hi)
                o = _attention(q, kv, min(seq_t['tq'], L), tk, seq_t['tkm'], pos0, kv_len, hi)
        else:
            P = PW[i]
            hv = GDN_HEADS * GDN_DV
            qkv, gate, ba = _in_proj(x, mod, g_mix, P['w_gdn'],
                                     ((0, GDN_QKV), (GDN_QKV, hv), (GDN_QKV + hv, LANES)),
                                     (F32, F32, F32), 0, 1, 512, hi)
            scb, scc, scv = _in_proj(x, mod, g_mix, P['w_sc'],
                                     ((0, SC_WIDTH), (SC_WIDTH, SC_WIDTH), (2 * SC_WIDTH, SC_WIDTH)),
                                     (F32, F32, F32), 0, 1, 512, hi)
            T = CHUNK
            lp = -(-L // T) * T
            y, s_new, cst_new = _gdn(_pad_seq(qkv, lp), _pad_seq(gate, lp), _pad_seq(ba, lp),
                                     caches['gdn_conv'][j], caches['gdn'][j].reshape(B, -1, GDN_DV),
                                     P['cw'], P['dtb'], P['alog'], P['nrm'], T, L, hi)
            y = y[:, :L]
            new['gdn'] = s_new.reshape(1, B, GDN_HEADS, GDN_DK, GDN_DV)
            new['gdn_conv'] = cst_new[None]
            o, sc_new = _sconv(scb, scc, scv, caches['sconv'][j], P['scw'], 512, hi)
            new['sconv'] = sc_new[None]
        wr = _pad_cols(wr_all[i], LANES)
        br = jnp.concatenate([W['b_router'][i].astype(F32), jnp.full((LANES - N_EXPERTS,), NEG_BIG, F32)]).reshape(1, LANES)
        x_new, h, logits = _out_proj(y, o, x, mod, P['wy'], P['wo'], W['norm_ffn'][i].reshape(1, D_MODEL),
                                     wr, br, 512, hi)
        final = i == DEPTH - 1
        xo = _moe(h.reshape(B * L, D_MODEL), logits.reshape(B * L, LANES), x_new.reshape(B * L, D_MODEL),
                  mod, W, i, W['norm_final'].reshape(1, D_MODEL), L, final)
        x = xo.reshape(B, L, D_MODEL)
    return x, new


def _prep_weights(W, dt):
    PW = {}
    for i in range(DEPTH):
        PW[i] = _even_weights(W, i // 2, dt) if i % 2 == 0 else _odd_weights(W, i // 2, dt)
    return PW


def kernel(x_prompt, x_sample, c_prompt, c_sample, cache_mla_ckv, cache_mla_krope, state_ssd, state_ssd_conv, state_gdn, state_gdn_conv, state_sconv, norm_mix, norm_ffn, w_ada, b_ada, w_router, b_router, w_gu, b_gu, w_down, b_down, norm_final, ev_w_in, ev_w_out, ssd_conv_w, ssd_conv_b, ssd_dt_bias, ssd_a_log, ssd_d, ssd_norm, mla_q_norm, mla_w_q_up, mla_kv_norm, mla_w_kv_up, od_w_in, od_w_out, gdn_conv_w, gdn_dt_bias, gdn_a_log, gdn_norm, sconv_w):
    W = dict(norm_mix=norm_mix, norm_ffn=norm_ffn, w_ada=w_ada, b_ada=b_ada, w_router=w_router,
             b_router=b_router, w_gu=w_gu, b_gu=b_gu, w_down=w_down, b_down=b_down, norm_final=norm_final,
             ev_w_in=ev_w_in, ev_w_out=ev_w_out, ssd_conv_w=ssd_conv_w, ssd_conv_b=ssd_conv_b,
             ssd_dt_bias=ssd_dt_bias, ssd_a_log=ssd_a_log, ssd_d=ssd_d, ssd_norm=ssd_norm,
             mla_q_norm=mla_q_norm, mla_w_q_up=mla_w_q_up, mla_kv_norm=mla_kv_norm, mla_w_kv_up=mla_w_kv_up,
             od_w_in=od_w_in, od_w_out=od_w_out, gdn_conv_w=gdn_conv_w, gdn_dt_bias=gdn_dt_bias,
             gdn_a_log=gdn_a_log, gdn_norm=gdn_norm, sconv_w=sconv_w)
    bp, bs = x_prompt.shape[0], x_sample.shape[0]
    nb = 16
    c_all = jnp.concatenate([c_prompt, c_sample, jnp.zeros((nb - bp - bs, D_MODEL), F32)], axis=0)
    mod_all = _ada_mod(c_all, w_ada, b_ada).reshape(DEPTH, nb, 6, D_MODEL)
    n_even, n_odd = (DEPTH + 1) // 2, DEPTH // 2
    zero_caches = dict(
        mla_ckv=jnp.zeros((n_even, bp, 0, MLA_KV_RANK), F32), mla_krope=jnp.zeros((n_even, bp, 0, MLA_ROPE), F32),
        ssd=jnp.zeros((n_even, bp, SSD_HEADS, SSD_HEADDIM, SSD_STATE), F32),
        ssd_conv=jnp.zeros((n_even, bp, SSD_CONV - 1, SSD_XBC), F32),
        gdn=jnp.zeros((n_odd, bp, GDN_HEADS, GDN_DK, GDN_DV), F32),
        gdn_conv=jnp.zeros((n_odd, bp, GDN_CONV - 1, GDN_QKV), F32),
        sconv=jnp.zeros((n_odd, bp, SC_CONV - 1, SC_WIDTH), F32))
    y_p, sp = _trunk(x_prompt, mod_all[:, :bp], 0, zero_caches, W, _prep_weights(W, BF16),
                     dict(ssd=256, tq=1024, tk=1024, tkm=512, latent=False), False)
    past = cache_mla_ckv.shape[2]
    caches = dict(mla_ckv=cache_mla_ckv, mla_krope=cache_mla_krope, ssd=state_ssd, ssd_conv=state_ssd_conv,
                  gdn=state_gdn, gdn_conv=state_gdn_conv, sconv=state_sconv)
    y_s, ss = _trunk(x_sample, mod_all[:, bp:bp + bs], past, caches, W, _prep_weights(W, F32),
                     dict(ssd=128, tq=32, tk=256, tkm=256, latent=True), True)
    return (y_p, y_s,
            sp['mla_ckv'], ss['mla_ckv'], sp['mla_krope'], ss['mla_krope'],
            sp['ssd'], ss['ssd'], sp['ssd_conv'], ss['ssd_conv'],
            sp['gdn'], ss['gdn'], sp['gdn_conv'], ss['gdn_conv'],
            sp['sconv'], ss['sconv'])
```

```python
import functools
import math

import jax
import jax.numpy as jnp
from jax import lax
from jax.experimental import pallas as pl
from jax.experimental.pallas import tpu as pltpu

F32 = jnp.float32
BF16 = jnp.bfloat16
I32 = jnp.int32

D_MODEL = 1024
DEPTH = 2
CHUNK = 64
CHUNK_SHIFT = 6
EPS = 1e-6
SSD_D_INNER = D_MODEL
SSD_HEADDIM = 64
SSD_HEADS = SSD_D_INNER // SSD_HEADDIM
SSD_GROUPS = 4
SSD_STATE = 128
SSD_CONV = 4
SSD_XBC = SSD_D_INNER + 2 * SSD_GROUPS * SSD_STATE
MLA_HEADS = 16
MLA_NOPE = 64
MLA_ROPE = 32
MLA_V = 64
MLA_Q_RANK = 384
MLA_KV_RANK = 256
ROPE_THETA = 10000.0
ROPE_LANE = MLA_NOPE
GDN_HEADS = 8
GDN_DK = 128
GDN_DV = 128
GDN_CONV = 4
GDN_QKV = GDN_HEADS * (2 * GDN_DK + GDN_DV)
SC_WIDTH = D_MODEL
SC_CONV = 3
N_EXPERTS = 32
TOP_K = 4
D_FF = D_MODEL
SWIGLU_LIMIT = 7.0
SWIGLU_ALPHA = 1.702

LANES = 128
SUBLANES = 8
VMEM_LIMIT = 56 * 1024 * 1024

NEG_BIG = -1e30
MOE_BM = 512
MOE_BM_SMALL = 128
DISPATCH_TM = 2048


def _cparams(sem):
    return pltpu.CompilerParams(dimension_semantics=sem, vmem_limit_bytes=VMEM_LIMIT)


def _sigmoid(x):
    return 1.0 / (1.0 + jnp.exp(-x))


def _silu(x):
    return x * _sigmoid(x)


def _softplus(x):
    return jnp.maximum(x, 0.0) + jnp.log1p(jnp.exp(-jnp.abs(x)))


_NN = (((1,), (0,)), ((), ()))
_NT = (((1,), (1,)), ((), ()))


def _dot_split(a, b, dims=_NN):
    a_h = a.astype(BF16)
    b_h = b.astype(BF16)
    a_l = (a - a_h.astype(F32)).astype(BF16)
    b_l = (b - b_h.astype(F32)).astype(BF16)
    d = functools.partial(lax.dot_general, dimension_numbers=dims, preferred_element_type=F32)
    return d(a_h, b_h) + d(a_l, b_h) + d(a_h, b_l)


def _dot(a, b, hi=False):
    if hi:
        return _dot_split(a.astype(F32), b.astype(F32))
    return jnp.dot(a.astype(BF16), b.astype(BF16), preferred_element_type=F32)


def _dot_nt(a, b, hi=False):
    if hi:
        return _dot_split(a.astype(F32), b.astype(F32), _NT)
    return lax.dot_general(a.astype(BF16), b.astype(BF16), _NT, preferred_element_type=F32)


def _split3(a):
    a1 = a.astype(BF16)
    r = a - a1.astype(F32)
    a2 = r.astype(BF16)
    a3 = (r - a2.astype(F32)).astype(BF16)
    return a1, a2, a3


def _dot_sel(sel, a, dims=_NN, sel_left=True):
    s = sel.astype(BF16)
    d = functools.partial(lax.dot_general, dimension_numbers=dims, preferred_element_type=F32)
    if sel_left:
        return sum(d(s, p) for p in _split3(a))
    return sum(d(p, s) for p in _split3(a))


def _act_dtype(hi):
    return F32 if hi else BF16


def _rope_rot(p, cos_t, sin_a, sin_b):
    return p * cos_t + pltpu.roll(p, LANES - MLA_ROPE // 2, 1) * sin_a + pltpu.roll(p, MLA_ROPE // 2, 1) * sin_b


def _ada_kernel(c_ref, w_ref, b_ref, o_ref):
    c = c_ref[...]
    o_ref[0] = _dot_split(_silu(c), w_ref[0]) + b_ref[0]


def _ada_mod(c_all, w_ada, b_ada):
    nb = c_all.shape[0]
    return pl.pallas_call(
        _ada_kernel,
        grid=(DEPTH, 6),
        in_specs=[pl.BlockSpec((nb, D_MODEL), lambda i, j: (0, 0)),
                  pl.BlockSpec((1, D_MODEL, D_MODEL), lambda i, j: (i, 0, j)),
                  pl.BlockSpec((1, 1, D_MODEL), lambda i, j: (i, 0, j))],
        out_specs=pl.BlockSpec((1, nb, D_MODEL), lambda i, j: (i, 0, j)),
        out_shape=jax.ShapeDtypeStruct((DEPTH, nb, 6 * D_MODEL), F32),
        compiler_params=_cparams(("parallel", "parallel")),
        name="ada_mod",
    )(c_all, w_ada, b_ada.reshape(DEPTH, 1, 6 * D_MODEL))


def _in_kernel(x_ref, mod_ref, g_ref, w_ref, *out_refs, segs, shift_row, scale_row, hi):
    x = x_ref[0]
    h = x * lax.rsqrt(jnp.mean(x * x, axis=-1, keepdims=True) + EPS) * g_ref[...]
    h = h * (1.0 + mod_ref[0, scale_row:scale_row + 1, :]) + mod_ref[0, shift_row:shift_row + 1, :]
    r = _dot(h, w_ref[...], hi)
    for (off, width), o_ref in zip(segs, out_refs):
        o_ref[0] = r[:, off:off + width].astype(o_ref.dtype)


def _in_proj(x, mod, g, w, segs, dtypes, shift_row, scale_row, tm, hi):
    B, L, _ = x.shape
    n_p = w.shape[1]
    tm = min(tm, L)
    return pl.pallas_call(
        functools.partial(_in_kernel, segs=segs, shift_row=shift_row, scale_row=scale_row, hi=hi),
        grid=(B, L // tm),
        in_specs=[pl.BlockSpec((1, tm, D_MODEL), lambda b, i: (b, i, 0)),
                  pl.BlockSpec((1, 6, D_MODEL), lambda b, i: (b, 0, 0)),
                  pl.BlockSpec((1, D_MODEL), lambda b, i: (0, 0)),
                  pl.BlockSpec((D_MODEL, n_p), lambda b, i: (0, 0))],
        out_specs=[pl.BlockSpec((1, tm, wd), lambda b, i: (b, i, 0)) for _, wd in segs],
        out_shape=[jax.ShapeDtypeStruct((B, L, wd), dt) for (_, wd), dt in zip(segs, dtypes)],
        compiler_params=_cparams(("parallel", "parallel")),
        name="in_proj",
    )(x, mod, g, w)


def _ssd_kernel(z_ref, xbc_ref, dtr_ref, cst_ref, s0_ref, cw_ref, cb_ref, dtb_ref, alog_ref, dsk_ref,
                nrm_ref, e_ref, y_ref, sout_ref, cout_ref, state_sc, cbuf, *, T, n_valid, hi):
    c = pl.program_id(1)
    nc = pl.num_programs(1)
    halo = SSD_CONV - 1
    base = SUBLANES - halo

    @pl.when(c == 0)
    def _():
        state_sc[...] = s0_ref[0]
        cbuf[base:SUBLANES, :] = cst_ref[0]

    cbuf[SUBLANES:SUBLANES + T, :] = xbc_ref[0]
    conv = cb_ref[...] + cw_ref[0:1, :] * cbuf[base:base + T, :]
    for k in range(1, SSD_CONV):
        conv = conv + cw_ref[k:k + 1, :] * cbuf[base + k:base + k + T, :]
    xc = _silu(conv)

    c_last = (n_valid - 1) // T
    nv_last = n_valid - c_last * T

    @pl.when(c == c_last)
    def _():
        cout_ref[0] = cbuf[base + nv_last:base + nv_last + halo, :]

    cbuf[base:SUBLANES, :] = cbuf[base + T:SUBLANES + T, :]

    xs = xc[:, :SSD_D_INNER]
    gn = SSD_GROUPS * SSD_STATE
    bm = xc[:, SSD_D_INNER:SSD_D_INNER + gn]
    cm = xc[:, SSD_D_INNER + gn:]

    tok = lax.broadcasted_iota(I32, (T, 1), 0) + c * T
    dt = jnp.where(tok < n_valid, _softplus(dtr_ref[0] + dtb_ref[...]), 0.0)
    a = dt * (-jnp.exp(alog_ref[...]))
    ri = lax.broadcasted_iota(I32, (T, T), 0)
    ci = lax.broadcasted_iota(I32, (T, T), 1)
    causal = ci <= ri
    tril = jnp.where(causal, 1.0, 0.0).astype(F32)
    acum = _dot_sel(tril, a)
    eye = jnp.where(lax.broadcasted_iota(I32, (LANES, LANES), 0) == lax.broadcasted_iota(I32, (LANES, LANES), 1),
                    1.0, 0.0).astype(F32)
    acum_t = _dot_sel(eye, acum, _NT)
    a_last = acum[T - 1:T, :]
    e = e_ref[...]
    xdt = xs * _dot_sel(e, dt, sel_left=False)
    eacum_x = jnp.exp(_dot_sel(e, acum, sel_left=False))
    xdend = xdt * jnp.exp(_dot_sel(e, a_last - acum, sel_left=False))

    r = SSD_HEADS // SSD_GROUPS
    gw = r * SSD_HEADDIM
    y_groups = []
    for g in range(SSD_GROUPS):
        bg = bm[:, g * SSD_STATE:(g + 1) * SSD_STATE]
        cg = cm[:, g * SSD_STATE:(g + 1) * SSD_STATE]
        cb_mat = _dot_nt(cg, bg, hi)
        ys = []
        for j in range(r):
            h = g * r + j
            seg = acum[:, h:h + 1] - acum_t[h:h + 1, :]
            lm = jnp.where(causal, jnp.exp(jnp.minimum(seg, 0.0)), 0.0)
            ys.append(_dot(cb_mat * lm, xdt[:, h * SSD_HEADDIM:(h + 1) * SSD_HEADDIM], hi))
        y_diag = jnp.concatenate(ys, axis=-1)
        s_g = state_sc[g * gw:(g + 1) * gw, :]
        y_off = _dot_nt(cg, s_g, hi) * eacum_x[:, g * gw:(g + 1) * gw]
        y_groups.append(y_diag + y_off)
        cs = _dot(jnp.transpose(xdend[:, g * gw:(g + 1) * gw]), bg, hi)
        dec = jnp.concatenate(
            [jnp.broadcast_to(jnp.exp(acum_t[g * r + j:g * r + j + 1, T - 1:T]), (SSD_HEADDIM, SSD_STATE))
             for j in range(r)], axis=0)
        state_sc[g * gw:(g + 1) * gw, :] = s_g * dec + cs
    y = jnp.concatenate(y_groups, axis=-1)
    y = y + xs * dsk_ref[...]
    y = y * _silu(z_ref[0])
    outs = []
    for g in range(SSD_GROUPS):
        yg = y[:, g * gw:(g + 1) * gw]
        outs.append(yg * lax.rsqrt(jnp.mean(yg * yg, axis=-1, keepdims=True) + EPS))
    y_ref[0] = (jnp.concatenate(outs, axis=-1) * nrm_ref[...]).astype(y_ref.dtype)

    @pl.when(c == nc - 1)
    def _():
        sout_ref[0] = state_sc[...]


def _ssd(z, xbc, dtr, cst, s0, cw, cb, dtb, alog, dsk, nrm, e, T, n_valid, hi):
    B, Lp, _ = z.shape
    hp = SSD_HEADS * SSD_HEADDIM
    row = lambda b, c: (0, 0)
    return pl.pallas_call(
        functools.partial(_ssd_kernel, T=T, n_valid=n_valid, hi=hi),
        grid=(B, Lp // T),
        in_specs=[pl.BlockSpec((1, T, SSD_D_INNER), lambda b, c: (b, c, 0)),
                  pl.BlockSpec((1, T, SSD_XBC), lambda b, c: (b, c, 0)),
                  pl.BlockSpec((1, T, LANES), lambda b, c: (b, c, 0)),
                  pl.BlockSpec((1, SSD_CONV - 1, SSD_XBC), lambda b, c: (b, 0, 0)),
                  pl.BlockSpec((1, hp, SSD_STATE), lambda b, c: (b, 0, 0)),
                  pl.BlockSpec((SSD_CONV, SSD_XBC), row),
                  pl.BlockSpec((1, SSD_XBC), row),
                  pl.BlockSpec((1, LANES), row),
                  pl.BlockSpec((1, LANES), row),
                  pl.BlockSpec((1, SSD_D_INNER), row),
                  pl.BlockSpec((1, SSD_D_INNER), row),
                  pl.BlockSpec((LANES, SSD_D_INNER), row)],
        out_specs=[pl.BlockSpec((1, T, SSD_D_INNER), lambda b, c: (b, c, 0)),
                   pl.BlockSpec((1, hp, SSD_STATE), lambda b, c: (b, 0, 0)),
                   pl.BlockSpec((1, SSD_CONV - 1, SSD_XBC), lambda b, c: (b, 0, 0))],
        out_shape=[jax.ShapeDtypeStruct((B, Lp, SSD_D_INNER), _act_dtype(hi)),
                   jax.ShapeDtypeStruct((B, hp, SSD_STATE), F32),
                   jax.ShapeDtypeStruct((B, SSD_CONV - 1, SSD_XBC), F32)],
        scratch_shapes=[pltpu.VMEM((hp, SSD_STATE), F32),
                        pltpu.VMEM((T + SUBLANES, SSD_XBC), F32)],
        compiler_params=_cparams(("parallel", "arbitrary")),
        name="ssd_scan",
    )(z, xbc, dtr, cst, s0, cw, cb, dtb, alog, dsk, nrm, e)


def _latkv_kernel(lat_ref, g_ref, cos_ref, sa_ref, sb_ref, ckv_ref, kpe_ref):
    lat = lat_ref[0]
    cr = lat[:, :MLA_KV_RANK]
    ckv_ref[0] = cr * lax.rsqrt(jnp.mean(cr * cr, axis=-1, keepdims=True) + EPS) * g_ref[...]
    kpe_ref[0] = _rope_rot(lat[:, MLA_KV_RANK:], cos_ref[...], sa_ref[...], sb_ref[...])


def _latkv_post(latkv, g, cos_t, sin_a, sin_b, tm):
    B, L, wp = latkv.shape
    tm = min(tm, L)
    tab = pl.BlockSpec((tm, LANES), lambda b, i: (i, 0))
    return pl.pallas_call(
        _latkv_kernel,
        grid=(B, L // tm),
        in_specs=[pl.BlockSpec((1, tm, wp), lambda b, i: (b, i, 0)),
                  pl.BlockSpec((1, MLA_KV_RANK), lambda b, i: (0, 0)), tab, tab, tab],
        out_specs=[pl.BlockSpec((1, tm, MLA_KV_RANK), lambda b, i: (b, i, 0)),
                   pl.BlockSpec((1, tm, LANES), lambda b, i: (b, i, 0))],
        out_shape=[jax.ShapeDtypeStruct((B, L, MLA_KV_RANK), F32),
                   jax.ShapeDtypeStruct((B, L, LANES), F32)],
        compiler_params=_cparams(("parallel", "parallel")),
        name="mla_latent_kv",
    )(latkv, g, cos_t, sin_a, sin_b)


def _q_kernel(lat_ref, g_ref, w_ref, cos_ref, sa_ref, sb_ref, q_ref, *, scale, hi):
    lat = lat_ref[0]
    n = lat * lax.rsqrt(jnp.mean(lat * lat, axis=-1, keepdims=True) + EPS) * g_ref[...]
    cos_t, sin_a, sin_b = cos_ref[...], sa_ref[...], sb_ref[...]
    nope = lax.broadcasted_iota(I32, (lat.shape[0], LANES), 1) < ROPE_LANE
    for h in range(MLA_HEADS):
        q = _dot(n, w_ref[h], hi)
        p = _rope_rot(jnp.where(nope, 0.0, q), cos_t, sin_a, sin_b)
        q_ref[0, h] = (jnp.where(nope, q, p) * scale).astype(q_ref.dtype)


def _q_proj(latq, g, wq, cos_t, sin_a, sin_b, tm, hi):
    B, L, _ = latq.shape
    tm = min(tm, L)
    tab = pl.BlockSpec((tm, LANES), lambda b, i: (i, 0))
    scale = (MLA_NOPE + MLA_ROPE) ** -0.5 * math.log2(math.e)
    return pl.pallas_call(
        functools.partial(_q_kernel, scale=scale, hi=hi),
        grid=(B, L // tm),
        in_specs=[pl.BlockSpec((1, tm, MLA_Q_RANK), lambda b, i: (b, i, 0)),
                  pl.BlockSpec((1, MLA_Q_RANK), lambda b, i: (0, 0)),
                  pl.BlockSpec((MLA_HEADS, MLA_Q_RANK, LANES), lambda b, i: (0, 0, 0)), tab, tab, tab],
        out_specs=pl.BlockSpec((1, MLA_HEADS, tm, LANES), lambda b, i: (b, 0, i, 0)),
        out_shape=jax.ShapeDtypeStruct((B, MLA_HEADS, L, LANES), _act_dtype(hi)),
        compiler_params=_cparams(("parallel", "parallel")),
        name="mla_q_proj",
    )(latq, g, wq, cos_t, sin_a, sin_b)


def _kvup_kernel(ckv_ref, kpe_ref, w_ref, kv_ref, *, hi):
    ckv = ckv_ref[0]
    kpe = kpe_ref[0]
    nope = lax.broadcasted_iota(I32, kpe.shape, 1) < ROPE_LANE
    for h in range(MLA_HEADS):
        kv = _dot(ckv, w_ref[h], hi)
        key = jnp.where(nope, kv, kpe)
        val = jnp.where(nope, pltpu.roll(kv, MLA_V, 1), 0.0)
        kv_ref[0, h] = jnp.concatenate([key, val], axis=-1).astype(kv_ref.dtype)


def _kv_up(ckv, kpe, wkv, tm, hi):
    B, Lk, _ = ckv.shape
    tm = min(tm, Lk)
    return pl.pallas_call(
        functools.partial(_kvup_kernel, hi=hi),
        grid=(B, Lk // tm),
        in_specs=[pl.BlockSpec((1, tm, MLA_KV_RANK), lambda b, i: (b, i, 0)),
                  pl.BlockSpec((1, tm, LANES), lambda b, i: (b, i, 0)),
                  pl.BlockSpec((MLA_HEADS, MLA_KV_RANK, LANES), lambda b, i: (0, 0, 0))],
        out_specs=pl.BlockSpec((1, MLA_HEADS, tm, 2 * LANES), lambda b, i: (b, 0, i, 0)),
        out_shape=jax.ShapeDtypeStruct((B, MLA_HEADS, Lk, 2 * LANES), _act_dtype(hi)),
        compiler_params=_cparams(("parallel", "parallel")),
        name="mla_kv_up",
    )(ckv, kpe, wkv)


def _attn_kernel(q_ref, kv_ref, o_ref, m_sc, l_sc, acc_sc, *, tq, tk, tkm, n_sub, q_off, kv_len, hi):
    q0 = pl.program_id(2) * tq
    first = q_off + q0

    def seen_by_all(f):
        return jnp.minimum(kv_len, (f // CHUNK + 1) * CHUNK)

    n_full = seen_by_all(first) // tk
    m_lo = n_full * (tk // tkm)

    def block(r0, rows, k0, width, masked):
        hds = range(2)
        rs = slice(r0, r0 + rows)
        k = [kv_ref[0, hd, pl.ds(k0, width), :] for hd in hds]
        s = [_dot_nt(q_ref[0, hd, rs, :], k[hd][:, :LANES], hi) for hd in hds]
        if masked:
            qchunk = jnp.right_shift(first + r0 + lax.broadcasted_iota(I32, (rows, 1), 0), CHUNK_SHIFT)
            kpos = k0 + lax.broadcasted_iota(I32, (1, width), 1)
            vis = jnp.logical_and(jnp.right_shift(kpos, CHUNK_SHIFT) <= qchunk, kpos < kv_len)
            s = [jnp.where(vis, s_, NEG_BIG) for s_ in s]
        m_prev = [m_sc[hd, rs, :] for hd in hds]
        m_new = [jnp.maximum(m_prev[hd], jnp.max(s[hd], axis=-1, keepdims=True)) for hd in hds]
        alpha = [jnp.exp2(m_prev[hd] - m_new[hd]) for hd in hds]
        p = [jnp.exp2(s[hd] - jnp.tile(m_new[hd], (1, width // LANES))) for hd in hds]
        pv = [_dot(p[hd], k[hd][:, LANES:], hi) for hd in hds]
        for hd in hds:
            l_sc[hd, rs, :] = alpha[hd] * l_sc[hd, rs, :] + jnp.sum(p[hd], axis=-1, keepdims=True)
            acc_sc[hd, rs, :] = alpha[hd] * acc_sc[hd, rs, :] + pv[hd]
            m_sc[hd, rs, :] = m_new[hd]

    m_sc[...] = jnp.full((2, tq, LANES), NEG_BIG, F32)
    l_sc[...] = jnp.zeros((2, tq, LANES), F32)
    acc_sc[...] = jnp.zeros((2, tq, LANES), F32)

    def full_body(j, carry):
        block(0, tq, pl.multiple_of(j * tk, tk), tk, False)
        return carry

    lax.fori_loop(0, n_full, full_body, 0)
    rows = tq // n_sub
    for r in range(n_sub):
        f = first + r * rows
        u_hi = jnp.maximum(m_lo, seen_by_all(f) // tkm)
        n_any = jnp.minimum(kv_len, ((f + rows - 1) // CHUNK + 1) * CHUNK)
        m_hi = (n_any + tkm - 1) // tkm

        def open_body(j, carry, r=r):
            block(r * rows, rows, pl.multiple_of(j * tkm, tkm), tkm, False)
            return carry

        def masked_body(j, carry, r=r):
            block(r * rows, rows, pl.multiple_of(j * tkm, tkm), tkm, True)
            return carry

        lax.fori_loop(m_lo, u_hi, open_body, 0)
        lax.fori_loop(u_hi, m_hi, masked_body, 0)
    outs = [acc_sc[hd] / l_sc[hd] for hd in range(2)]
    lane = lax.broadcasted_iota(I32, (tq, LANES), 1)
    o = jnp.where(lane < MLA_V, outs[0], pltpu.roll(outs[1], MLA_V, 1))
    o_ref[0] = o.astype(o_ref.dtype)


def _attention(q, kv, tq, tk, tkm, q_off, kv_len, hi):
    B, H, L, _ = q.shape
    Lk = kv.shape[2]
    n_sub = 2 if tq >= 2 * tkm else 1
    return pl.pallas_call(
        functools.partial(_attn_kernel, tq=tq, tk=tk, tkm=tkm, n_sub=n_sub, q_off=q_off, kv_len=kv_len, hi=hi),
        grid=(B, H // 2, L // tq),
        in_specs=[pl.BlockSpec((1, 2, tq, LANES), lambda b, h, i: (b, h, i, 0)),
                  pl.BlockSpec((1, 2, Lk, 2 * LANES), lambda b, h, i: (b, h, 0, 0))],
        out_specs=pl.BlockSpec((1, tq, LANES), lambda b, h, i: (b, i, h)),
        out_shape=jax.ShapeDtypeStruct((B, L, H * MLA_V), _act_dtype(hi)),
        scratch_shapes=[pltpu.VMEM((2, tq, LANES), F32)] * 3,
        compiler_params=_cparams(("parallel", "parallel", "parallel")),
        name="mla_attention",
    )(q, kv)


def _attn_latent_kernel(q_ref, ckv_ref, kpe_ref, wuk_ref, wuv_ref, o_ref, *, L, q_off, kv_len, hi):
    H = MLA_HEADS
    lk = ckv_ref.shape[1]
    ckv = ckv_ref[0]
    kpe = kpe_ref[0]
    qa = jnp.concatenate([_dot(q_ref[0, h], wuk_ref[h], hi) for h in range(H)], axis=0)
    qp = jnp.concatenate([q_ref[0, h] for h in range(H)], axis=0)
    s = _dot_nt(qa, ckv, hi) + _dot_nt(qp, kpe, hi)
    row = lax.broadcasted_iota(I32, (H * L, 1), 0)
    qchunk = jnp.right_shift(q_off + jnp.bitwise_and(row, L - 1), CHUNK_SHIFT)
    kpos = lax.broadcasted_iota(I32, (1, lk), 1)
    vis = jnp.logical_and(jnp.right_shift(kpos, CHUNK_SHIFT) <= qchunk, kpos < kv_len)
    s = jnp.where(vis, s, NEG_BIG)
    p = jnp.exp2(s - jnp.max(s, axis=-1, keepdims=True))
    lat = _dot(p, ckv, hi) / jnp.sum(p, axis=-1, keepdims=True)
    outs = [_dot(lat[h * L:(h + 1) * L], wuv_ref[h], hi)[:, :MLA_V] for h in range(H)]
    o_ref[0] = jnp.concatenate(outs, axis=-1).astype(o_ref.dtype)


def _attention_latent(q, ckv, kpe, wuk, wuv, q_off, kv_len, hi):
    B, H, L, _ = q.shape
    lk = ckv.shape[1]
    return pl.pallas_call(
        functools.partial(_attn_latent_kernel, L=L, q_off=q_off, kv_len=kv_len, hi=hi),
        grid=(B,),
        in_specs=[pl.BlockSpec((1, H, L, LANES), lambda b: (b, 0, 0, 0)),
                  pl.BlockSpec((1, lk, MLA_KV_RANK), lambda b: (b, 0, 0)),
                  pl.BlockSpec((1, lk, LANES), lambda b: (b, 0, 0)),
                  pl.BlockSpec((H, LANES, MLA_KV_RANK), lambda b: (0, 0, 0)),
                  pl.BlockSpec((H, MLA_KV_RANK, LANES), lambda b: (0, 0, 0))],
        out_specs=pl.BlockSpec((1, L, H * MLA_V), lambda b: (b, 0, 0)),
        out_shape=jax.ShapeDtypeStruct((B, L, H * MLA_V), _act_dtype(hi)),
        compiler_params=_cparams(("parallel",)),
        name="mla_attention_latent",
    )(q, ckv, kpe, wuk, wuv)


def _gdn_kernel(qkv_ref, gate_ref, ba_ref, cst_ref, s0_ref, cw_ref, dtb_ref, alog_ref, nrm_ref,
                o_ref, sout_ref, cout_ref, state_sc, cbuf, *, bb_n, **kw):
    for b in range(bb_n):
        _gdn_stream(qkv_ref.at[b], gate_ref.at[b], ba_ref.at[b], cst_ref.at[b], s0_ref.at[b], cw_ref, dtb_ref,
                    alog_ref, nrm_ref, o_ref.at[b], sout_ref.at[b], cout_ref.at[b], state_sc.at[b], cbuf.at[b], **kw)


def _gdn_stream(qkv_ref, gate_ref, ba_ref, cst_ref, s0_ref, cw_ref, dtb_ref, alog_ref, nrm_ref,
                o_ref, sout_ref, cout_ref, state_sc, cbuf, *, T, n_valid, hi):
    c = pl.program_id(1)
    nc = pl.num_programs(1)
    halo = GDN_CONV - 1
    base = SUBLANES - halo
    H = GDN_HEADS
    DK = GDN_DK

    @pl.when(c == 0)
    def _():
        state_sc[...] = s0_ref[...]
        cbuf[base:SUBLANES, :] = cst_ref[...]

    cbuf[SUBLANES:SUBLANES + T, :] = qkv_ref[...]
    conv = cw_ref[0:1, :] * cbuf[base:base + T, :]
    for k in range(1, GDN_CONV):
        conv = conv + cw_ref[k:k + 1, :] * cbuf[base + k:base + k + T, :]
    qkv = _silu(conv)

    c_last = (n_valid - 1) // T
    nv_last = n_valid - c_last * T

    @pl.when(c == c_last)
    def _():
        cout_ref[...] = cbuf[base + nv_last:base + nv_last + halo, :]

    cbuf[base:SUBLANES, :] = cbuf[base + T:SUBLANES + T, :]

    tok = lax.broadcasted_iota(I32, (T, 1), 0) + c * T
    valid = tok < n_valid
    ba = ba_ref[...]
    beta = jnp.where(valid, _sigmoid(ba), 0.0)
    g = jnp.where(valid, -jnp.exp(alog_ref[...]) * _softplus(ba + dtb_ref[...]), 0.0)
    ri = lax.broadcasted_iota(I32, (T, T), 0)
    ci = lax.broadcasted_iota(I32, (T, T), 1)
    tril = jnp.where(ci <= ri, 1.0, 0.0).astype(F32)
    gc = _dot_sel(tril, g)

    def l2n(x):
        return x * lax.rsqrt(jnp.sum(x * x, axis=-1, keepdims=True) + EPS)

    qn = [l2n(qkv[:, h * DK:(h + 1) * DK]) * (DK ** -0.5) for h in range(H)]
    kn = [l2n(qkv[:, (H + h) * DK:(H + h + 1) * DK]) for h in range(H)]
    vv = [qkv[:, (2 * H + h) * DK:(2 * H + h + 1) * DK] for h in range(H)]

    G = 2
    W = G * T
    bi = lax.broadcasted_iota(I32, (W, W), 0)
    bj = lax.broadcasted_iota(I32, (W, W), 1)
    t_shift = T.bit_length() - 1
    same = jnp.right_shift(bi, t_shift) == jnp.right_shift(bj, t_shift)
    strict = jnp.logical_and(same, bj < bi)
    incl = jnp.logical_and(same, bj <= bi)
    eye_w = jnp.where(bi == bj, 1.0, 0.0).astype(F32)
    rowhead = jnp.right_shift(lax.broadcasted_iota(I32, (W, DK), 0), t_shift)

    groups = [[grp * G + j for j in range(G)] for grp in range(H // G)]
    n_g = len(groups)
    kst = [jnp.concatenate([kn[h] for h in hs], axis=0) for hs in groups]
    qst = [jnp.concatenate([qn[h] for h in hs], axis=0) for hs in groups]
    vst = [jnp.concatenate([vv[h] for h in hs], axis=0) for hs in groups]
    cb = [jnp.concatenate([jnp.broadcast_to(gc[:, H + h:H + h + 1], (T, W)) for h in hs], axis=0)
          for hs in groups]
    bb = [jnp.concatenate([jnp.broadcast_to(beta[:, h:h + 1], (T, DK)) for h in hs], axis=0) for hs in groups]
    glast = [jnp.concatenate([jnp.broadcast_to(gc[T - 1:T, H + h:H + h + 1], (T, DK)) for h in hs], axis=0)
             for hs in groups]
    dec = [jnp.exp(jnp.minimum(c_ - jnp.transpose(c_), 0.0)) for c_ in cb]
    kk = [_dot_nt(k_, k_, hi) for k_ in kst]
    qk = [_dot_nt(q_, k_, hi) for q_, k_ in zip(qst, kst)]
    a_mat = [jnp.where(strict, jnp.tile(b_, (1, W // DK)) * kk_ * d_, 0.0) for b_, kk_, d_ in zip(bb, kk, dec)]
    qkm = [jnp.where(incl, qk_ * d_, 0.0) for qk_, d_ in zip(qk, dec)]
    p_mat = [eye_w - a_ for a_ in a_mat]
    a_pow = a_mat
    for _ in range(T.bit_length() - 2):
        a_pow = [_dot_split(a_, a_) for a_ in a_pow]
        p_mat = [p_ + _dot_split(p_, a_) for p_, a_ in zip(p_mat, a_pow)]
    cbk = [c_[:, :DK] for c_ in cb]
    eg = [jnp.exp(c_) for c_ in cbk]
    rhs = [jnp.concatenate([v_ * b_, k_ * b_ * e_], axis=-1) for v_, b_, k_, e_ in zip(vst, bb, kst, eg)]
    sol = [_dot_split(p_, r_) for p_, r_ in zip(p_mat, rhs)]
    qdec = [q_ * e_ for q_, e_ in zip(qst, eg)]
    kdec_t = [jnp.transpose(k_ * jnp.exp(g_ - c_)) for k_, g_, c_ in zip(kst, glast, cbk)]
    s_old = [state_sc[h * DK:(h + 1) * DK, :] for h in range(H)]
    ws = [[_dot(jnp.concatenate([sol[g][j * T:(j + 1) * T, DK:], qdec[g][j * T:(j + 1) * T]], axis=0),
                s_old[h], hi) for j, h in enumerate(groups[g])] for g in range(n_g)]
    vnew_st = [jnp.concatenate([sol[g][j * T:(j + 1) * T, :DK] - ws[g][j][:T] for j in range(G)], axis=0)
               for g in range(n_g)]
    o_st = [jnp.concatenate([ws[g][j][T:] for j in range(G)], axis=0) + _dot(qkm[g], vnew_st[g], hi)
            for g in range(n_g)]
    o_heads = [None] * H
    for g in range(n_g):
        for j, h in enumerate(groups[g]):
            vm = jnp.where(rowhead == j, vnew_st[g], 0.0)
            last = jnp.exp(jnp.broadcast_to(gc[T - 1:T, H + h:H + h + 1], (DK, GDN_DV)))
            state_sc[h * DK:(h + 1) * DK, :] = s_old[h] * last + _dot(kdec_t[g], vm, hi)
            o_heads[h] = o_st[g][j * T:(j + 1) * T]
    gate = gate_ref[...]
    outs = []
    for h in range(H):
        oh = o_heads[h]
        oh = oh * lax.rsqrt(jnp.mean(oh * oh, axis=-1, keepdims=True) + EPS) * nrm_ref[...]
        outs.append(oh * _silu(gate[:, h * GDN_DV:(h + 1) * GDN_DV]))
    o_ref[...] = jnp.concatenate(outs, axis=-1).astype(o_ref.dtype)

    @pl.when(c == nc - 1)
    def _():
        sout_ref[...] = state_sc[...]


def _gdn(qkv, gate, ba, cst, s0, cw, dtb, alog, nrm, T, n_valid, hi):
    B, Lp, _ = qkv.shape
    hk = GDN_HEADS * GDN_DK
    bb_n = 1
    row = lambda b, c: (0, 0)
    return pl.pallas_call(
        functools.partial(_gdn_kernel, bb_n=bb_n, T=T, n_valid=n_valid, hi=hi),
        grid=(B // bb_n, Lp // T),
        in_specs=[pl.BlockSpec((bb_n, T, GDN_QKV), lambda b, c: (b, c, 0)),
                  pl.BlockSpec((bb_n, T, GDN_HEADS * GDN_DV), lambda b, c: (b, c, 0)),
                  pl.BlockSpec((bb_n, T, LANES), lambda b, c: (b, c, 0)),
                  pl.BlockSpec((bb_n, GDN_CONV - 1, GDN_QKV), lambda b, c: (b, 0, 0)),
                  pl.BlockSpec((bb_n, hk, GDN_DV), lambda b, c: (b, 0, 0)),
                  pl.BlockSpec((GDN_CONV, GDN_QKV), row),
                  pl.BlockSpec((1, LANES), row),
                  pl.BlockSpec((1, LANES), row),
                  pl.BlockSpec((1, GDN_DV), row)],
        out_specs=[pl.BlockSpec((bb_n, T, GDN_HEADS * GDN_DV), lambda b, c: (b, c, 0)),
                   pl.BlockSpec((bb_n, hk, GDN_DV), lambda b, c: (b, 0, 0)),
                   pl.BlockSpec((bb_n, GDN_CONV - 1, GDN_QKV), lambda b, c: (b, 0, 0))],
        out_shape=[jax.ShapeDtypeStruct((B, Lp, GDN_HEADS * GDN_DV), _act_dtype(hi)),
                   jax.ShapeDtypeStruct((B, hk, GDN_DV), F32),
                   jax.ShapeDtypeStruct((B, GDN_CONV - 1, GDN_QKV), F32)],
        scratch_shapes=[pltpu.VMEM((bb_n, hk, GDN_DV), F32),
                        pltpu.VMEM((bb_n, T + SUBLANES, GDN_QKV), F32)],
        compiler_params=_cparams(("parallel", "arbitrary")),
        name="gdn_scan",
    )(qkv, gate, ba, cst, s0, cw, dtb, alog, nrm)


def _sconv_kernel(b_ref, c_ref, v_ref, cst_ref, w_ref, o_ref, cout_ref, cbuf, *, T):
    i = pl.program_id(1)
    halo = SC_CONV - 1
    base = SUBLANES - halo

    @pl.when(i == 0)
    def _():
        cbuf[base:SUBLANES, :] = cst_ref[0]

    cbuf[SUBLANES:SUBLANES + T, :] = c_ref[0] * v_ref[0]
    conv = w_ref[0:1, :] * cbuf[base:base + T, :]
    for k in range(1, SC_CONV):
        conv = conv + w_ref[k:k + 1, :] * cbuf[base + k:base + k + T, :]
    o_ref[0] = (b_ref[0] * conv).astype(o_ref.dtype)
    cout_ref[0] = cbuf[base + T:SUBLANES + T, :]
    cbuf[base:SUBLANES, :] = cbuf[base + T:SUBLANES + T, :]


def _sconv(scb, scc, scv, cst, w, T, hi):
    B, L, _ = scb.shape
    T = min(T, L)
    blk = pl.BlockSpec((1, T, SC_WIDTH), lambda b, i: (b, i, 0))
    st = pl.BlockSpec((1, SC_CONV - 1, SC_WIDTH), lambda b, i: (b, 0, 0))
    return pl.pallas_call(
        functools.partial(_sconv_kernel, T=T),
        grid=(B, L // T),
        in_specs=[blk, blk, blk, st, pl.BlockSpec((SC_CONV, SC_WIDTH), lambda b, i: (0, 0))],
        out_specs=[blk, st],
        out_shape=[jax.ShapeDtypeStruct((B, L, SC_WIDTH), _act_dtype(hi)),
                   jax.ShapeDtypeStruct((B, SC_CONV - 1, SC_WIDTH), F32)],
        scratch_shapes=[pltpu.VMEM((T + SUBLANES, SC_WIDTH), F32)],
        compiler_params=_cparams(("parallel", "arbitrary")),
        name="short_conv",
    )(scb, scc, scv, cst, w)


def _out_kernel(y_ref, o_ref, x_ref, mod_ref, wy_ref, wo_ref, g_ref, wr_ref, br_ref,
                xn_ref, h_ref, lg_ref, *, hi):
    mix = _dot(y_ref[0], wy_ref[...], hi) + _dot(o_ref[0], wo_ref[...], hi)
    xn = x_ref[0] + mod_ref[0, 2:3, :] * mix
    xn_ref[0] = xn
    h = xn * lax.rsqrt(jnp.mean(xn * xn, axis=-1, keepdims=True) + EPS) * g_ref[...]
    h = h * (1.0 + mod_ref[0, 4:5, :]) + mod_ref[0, 3:4, :]
    h_ref[0] = h
    lg_ref[0] = _dot_split(h, wr_ref[...]) + br_ref[...]


def _out_proj(y, o, x, mod, wy, wo, g, wr, br, tm, hi):
    B, L, _ = x.shape
    tm = min(tm, L)
    blk = lambda dt_w: pl.BlockSpec((1, tm, dt_w), lambda b, i: (b, i, 0))
    full = lambda s: pl.BlockSpec(s, lambda b, i: (0, 0))
    return pl.pallas_call(
        functools.partial(_out_kernel, hi=hi),
        grid=(B, L // tm),
        in_specs=[blk(D_MODEL), blk(D_MODEL), blk(D_MODEL),
                  pl.BlockSpec((1, 6, D_MODEL), lambda b, i: (b, 0, 0)),
                  full((D_MODEL, D_MODEL)), full((D_MODEL, D_MODEL)), full((1, D_MODEL)),
                  full((D_MODEL, LANES)), full((1, LANES))],
        out_specs=[blk(D_MODEL), blk(D_MODEL), blk(LANES)],
        out_shape=[jax.ShapeDtypeStruct((B, L, D_MODEL), F32),
                   jax.ShapeDtypeStruct((B, L, D_MODEL), F32),
                   jax.ShapeDtypeStruct((B, L, LANES), F32)],
        compiler_params=_cparams(("parallel", "parallel")),
        name="out_proj",
    )(y, o, x, mod, wy, wo, g, wr, br)


def _route_kernel(lg_ref, e_ref, rank_ref, gate_ref, cnt_ref, base_sc, *, tm):
    i = pl.program_id(0)

    @pl.when(i == 0)
    def _():
        base_sc[...] = jnp.zeros_like(base_sc)

    lg = lg_ref[...]
    lane_i = lax.broadcasted_iota(I32, (tm, LANES), 1)
    lane = lane_i.astype(F32)
    vals, idxs = [], []
    cur = lg
    for _ in range(TOP_K):
        m = jnp.max(cur, axis=-1, keepdims=True)
        idx = jnp.min(jnp.where(cur == m, lane, float(LANES)), axis=-1, keepdims=True)
        vals.append(m)
        idxs.append(idx)
        cur = jnp.where(lane == idx, -jnp.inf, cur)
    ex = [jnp.exp(v - vals[0]) for v in vals]
    den = ex[0] + ex[1] + ex[2] + ex[3]
    onehot = jnp.zeros((tm, LANES), F32)
    for idx in idxs:
        onehot = onehot + jnp.where(lane == idx, 1.0, 0.0)
    ri = lax.broadcasted_iota(I32, (tm, tm), 0)
    ci = lax.broadcasted_iota(I32, (tm, tm), 1)
    before = _dot(jnp.where(ci < ri, 1.0, 0.0), onehot) + base_sc[...]
    e_out = jnp.zeros((tm, LANES), I32)
    r_out = jnp.zeros((tm, LANES), I32)
    g_out = jnp.zeros((tm, LANES), F32)
    for k in range(TOP_K):
        rk = jnp.sum(jnp.where(lane == idxs[k], before, 0.0), axis=-1, keepdims=True)
        e_out = jnp.where(lane_i == k, idxs[k].astype(I32), e_out)
        r_out = jnp.where(lane_i == k, rk.astype(I32), r_out)
        g_out = jnp.where(lane_i == k, ex[k] / den, g_out)
    e_ref[...] = e_out
    rank_ref[...] = r_out
    gate_ref[...] = g_out
    base_sc[...] = base_sc[...] + jnp.sum(onehot, axis=0, keepdims=True)
    cnt_ref[...] = base_sc[...].astype(I32)


def _route(logits, tm):
    n_tok = logits.shape[0]
    tm = min(tm, n_tok)
    blk = pl.BlockSpec((tm, LANES), lambda i: (i, 0))
    return pl.pallas_call(
        functools.partial(_route_kernel, tm=tm),
        grid=(n_tok // tm,),
        in_specs=[blk],
        out_specs=[blk, blk, blk, pl.BlockSpec((1, LANES), lambda i: (0, 0))],
        out_shape=[jax.ShapeDtypeStruct((n_tok, LANES), I32),
                   jax.ShapeDtypeStruct((n_tok, LANES), I32),
                   jax.ShapeDtypeStruct((n_tok, LANES), F32),
                   jax.ShapeDtypeStruct((1, LANES), I32)],
        scratch_shapes=[pltpu.VMEM((1, LANES), F32)],
        compiler_params=_cparams(("arbitrary",)),
        name="moe_route",
    )(logits)


def _dispatch_kernel(zs_ref, nz_ref, dest_ref, h_ref, out_hbm, zbuf, sem, zsem, *, tm, n_blocks):
    bm = zbuf.shape[0]

    def zero_copy(row0):
        return pltpu.make_async_copy(zbuf, out_hbm.at[pl.ds(row0, bm), :], zsem)

    @pl.when(pl.program_id(0) == 0)
    def _():
        zbuf[...] = jnp.zeros_like(zbuf)
        n_used = nz_ref[0]
        for e in range(N_EXPERTS):
            @pl.when(zs_ref[e] >= 0)
            def _(e=e):
                zero_copy(pl.multiple_of(zs_ref[e], bm)).start()

        def tail(j, carry):
            zero_copy(pl.multiple_of(j * bm, bm)).start()
            return carry

        lax.fori_loop(n_used, n_blocks, tail, 0)

        def drain(j, carry):
            zero_copy(0).wait()
            return carry

        lax.fori_loop(0, nz_ref[1], drain, 0)

    def issue(r, carry):
        for k in range(TOP_K):
            d = dest_ref[r * TOP_K + k]
            pltpu.make_async_copy(h_ref.at[pl.ds(r, 1), :], out_hbm.at[pl.ds(d, 1), :], sem).start()
        return carry

    lax.fori_loop(0, tm, issue, 0)
    for _ in range(TOP_K):
        pltpu.make_async_copy(h_ref, out_hbm.at[pl.ds(0, tm), :], sem).wait()


def _dispatch(h, dest_flat, zero_start, zero_counts, n_rows, tm, bm):
    n_tok = h.shape[0]
    tm = min(tm, n_tok)
    grid_spec = pltpu.PrefetchScalarGridSpec(
        num_scalar_prefetch=2,
        grid=(n_tok // tm,),
        in_specs=[pl.BlockSpec((tm * TOP_K,), lambda i, zs, nz: (i,), memory_space=pltpu.SMEM),
                  pl.BlockSpec((tm, D_MODEL), lambda i, zs, nz: (i, 0))],
        out_specs=pl.BlockSpec(memory_space=pl.ANY),
        scratch_shapes=[pltpu.VMEM((bm, D_MODEL), F32), pltpu.SemaphoreType.DMA(()),
                        pltpu.SemaphoreType.DMA(())],
    )
    return pl.pallas_call(
        functools.partial(_dispatch_kernel, tm=tm, n_blocks=n_rows // bm),
        grid_spec=grid_spec,
        out_shape=jax.ShapeDtypeStruct((n_rows, D_MODEL), F32),
        compiler_params=_cparams(("arbitrary",)),
        name="moe_dispatch",
    )(zero_start, zero_counts, dest_flat, h)


def _ffn_kernel(blk_e_ref, nused_ref, x_ref, wgu_ref, bgu_ref, wd_ref, bd_ref, o_ref, wgu_sc, wd_sc):
    i = pl.program_id(0)
    prev = blk_e_ref[jnp.maximum(i - 1, 0)]
    fresh = jnp.logical_or(i == 0, blk_e_ref[i] != prev)
    active = i < nused_ref[0]

    @pl.when(jnp.logical_and(active, fresh))
    def _():
        wgu_sc[...] = wgu_ref[0, 0].astype(BF16)
        wd_sc[...] = wd_ref[0, 0].astype(BF16)

    @pl.when(active)
    def _():
        gu = jnp.dot(x_ref[...].astype(BF16), wgu_sc[...], preferred_element_type=F32) + bgu_ref[0]
        gate = jnp.minimum(gu[:, :D_FF], SWIGLU_LIMIT)
        up = jnp.clip(gu[:, D_FF:], -SWIGLU_LIMIT, SWIGLU_LIMIT)
        act = (up + 1.0) * gate * _sigmoid(SWIGLU_ALPHA * gate)
        o_ref[...] = jnp.dot(act.astype(BF16), wd_sc[...], preferred_element_type=F32) + bd_ref[0]

    @pl.when(jnp.logical_not(active))
    def _():
        o_ref[...] = jnp.zeros_like(o_ref)


def _expert_ffn(xin, blk_e, n_used, w_gu, b_gu, w_down, b_down, layer, bm):
    n_rows = xin.shape[0]
    n_blocks = n_rows // bm

    def row_map(i, be, nu):
        return (jnp.minimum(i, nu[0] - 1), 0)

    def e_map4(i, be, nu):
        return (layer, be[jnp.minimum(i, nu[0] - 1)], 0, 0)

    def e_map3(i, be, nu):
        return (layer * N_EXPERTS + be[jnp.minimum(i, nu[0] - 1)], 0, 0)

    grid_spec = pltpu.PrefetchScalarGridSpec(
        num_scalar_prefetch=2,
        grid=(n_blocks,),
        in_specs=[pl.BlockSpec((bm, D_MODEL), row_map),
                  pl.BlockSpec((1, 1, D_MODEL, 2 * D_FF), e_map4),
                  pl.BlockSpec((1, 1, 2 * D_FF), e_map3),
                  pl.BlockSpec((1, 1, D_FF, D_MODEL), e_map4),
                  pl.BlockSpec((1, 1, D_MODEL), e_map3)],
        out_specs=pl.BlockSpec((bm, D_MODEL), lambda i, be, nu: (i, 0)),
        scratch_shapes=[pltpu.VMEM((D_MODEL, 2 * D_FF), BF16),
                        pltpu.VMEM((D_FF, D_MODEL), BF16)],
    )
    return pl.pallas_call(
        _ffn_kernel,
        grid_spec=grid_spec,
        out_shape=jax.ShapeDtypeStruct((n_rows, D_MODEL), F32),
        compiler_params=_cparams(("arbitrary",)),
        name="moe_expert_ffn",
    )(blk_e, n_used, xin, w_gu, b_gu.reshape(DEPTH * N_EXPERTS, 1, 2 * D_FF), w_down,
      b_down.reshape(DEPTH * N_EXPERTS, 1, D_MODEL))


def _combine_kernel(dest_ref, dest_next_ref, f_hbm, gate_ref, x_ref, mod_ref, g_ref, o_ref, buf, sem, *, tm, final):
    i = pl.program_id(0)
    slot = lax.rem(i, 2)

    def fetch(idx_ref, s):
        def issue(r, carry):
            for k in range(TOP_K):
                d = idx_ref[r * TOP_K + k]
                pltpu.make_async_copy(f_hbm.at[pl.ds(d, 1), :], buf.at[s, k, pl.ds(r, 1), :], sem.at[s]).start()
            return carry

        lax.fori_loop(0, tm, issue, 0)

    @pl.when(i == 0)
    def _():
        fetch(dest_ref, 0)

    @pl.when(i + 1 < pl.num_programs(0))
    def _():
        fetch(dest_next_ref, 1 - slot)

    for k in range(TOP_K):
        pltpu.make_async_copy(f_hbm.at[pl.ds(0, tm), :], buf.at[slot, k], sem.at[slot]).wait()
    gates = gate_ref[...]
    moe = gates[:, 0:1] * buf[slot, 0]
    for k in range(1, TOP_K):
        moe = moe + gates[:, k:k + 1] * buf[slot, k]
    xo = x_ref[...] + mod_ref[0, 5:6, :] * moe
    if final:
        xo = xo * lax.rsqrt(jnp.mean(xo * xo, axis=-1, keepdims=True) + EPS) * g_ref[...]
    o_ref[...] = xo


def _combine(ffn_out, dest_flat, gates, x, mod, g_final, tiles_per_batch, tm, final):
    n_tok = x.shape[0]
    n_tiles = n_tok // tm
    return pl.pallas_call(
        functools.partial(_combine_kernel, tm=tm, final=final),
        grid=(n_tiles,),
        in_specs=[pl.BlockSpec((tm * TOP_K,), lambda i: (i,), memory_space=pltpu.SMEM),
                  pl.BlockSpec((tm * TOP_K,), lambda i: (jnp.minimum(i + 1, n_tiles - 1),),
                               memory_space=pltpu.SMEM),
                  pl.BlockSpec(memory_space=pl.ANY),
                  pl.BlockSpec((tm, LANES), lambda i: (i, 0)),
                  pl.BlockSpec((tm, D_MODEL), lambda i: (i, 0)),
                  pl.BlockSpec((1, 6, D_MODEL), lambda i: (i // tiles_per_batch, 0, 0)),
                  pl.BlockSpec((1, D_MODEL), lambda i: (0, 0))],
        out_specs=pl.BlockSpec((tm, D_MODEL), lambda i: (i, 0)),
        out_shape=jax.ShapeDtypeStruct((n_tok, D_MODEL), F32),
        scratch_shapes=[pltpu.VMEM((2, TOP_K, tm, D_MODEL), F32), pltpu.SemaphoreType.DMA((2,))],
        compiler_params=_cparams(("arbitrary",)),
        name="moe_combine",
    )(dest_flat, dest_flat, ffn_out, gates, x, mod, g_final)


def _pad_cols(w, width):
    return jnp.pad(w, ((0, 0), (0, width - w.shape[1])))


def _pad_lanes(v, offset=0):
    return jnp.pad(v.astype(F32), (offset, LANES - offset - v.shape[0])).reshape(1, LANES)


def _rope_tables(pos):
    half = MLA_ROPE // 2
    inv = ROPE_THETA ** (-jnp.arange(half, dtype=F32) / half)
    ang = pos.astype(F32)[:, None] * inv[None, :]
    cos, sin = jnp.cos(ang), jnp.sin(ang)
    z = jnp.zeros_like(cos)
    lead = jnp.zeros((pos.shape[0], ROPE_LANE), F32)
    pad = jnp.zeros((pos.shape[0], LANES - ROPE_LANE - MLA_ROPE), F32)
    cos_t = jnp.concatenate([lead, cos, cos, pad], axis=1)
    sin_a = jnp.concatenate([lead, -sin, z, pad], axis=1)
    sin_b = jnp.concatenate([lead, z, sin, pad], axis=1)
    return cos_t, sin_a, sin_b


def _pad_seq(t, lp):
    return jnp.pad(t, ((0, 0), (0, lp - t.shape[1]), (0, 0)))


def _even_weights(W, j, dt):
    w_in = W['ev_w_in'][j]
    o1 = SSD_D_INNER
    o2 = o1 + SSD_XBC
    o3 = o2 + SSD_HEADS
    o4 = o3 + MLA_Q_RANK
    w_ssd = jnp.concatenate([w_in[:, :o2], _pad_cols(w_in[:, o2:o3], LANES)], axis=1).astype(dt)
    o5 = o4 + MLA_KV_RANK
    zc = lambda n: jnp.zeros((D_MODEL, n), F32)
    w_mla = jnp.concatenate([w_in[:, o3:o4], w_in[:, o4:o5], zc(ROPE_LANE), w_in[:, o5:],
                             zc(LANES - ROPE_LANE - MLA_ROPE)], axis=1).astype(dt)
    wq = W['mla_w_q_up'][j].reshape(MLA_Q_RANK, MLA_HEADS, MLA_NOPE + MLA_ROPE)
    wq = jnp.pad(wq, ((0, 0), (0, 0), (0, LANES - MLA_NOPE - MLA_ROPE)))
    wq = jnp.transpose(wq, (1, 0, 2)).astype(dt)
    wkv = jnp.transpose(W['mla_w_kv_up'][j].reshape(MLA_KV_RANK, MLA_HEADS, MLA_NOPE + MLA_V),
                        (1, 0, 2)).astype(dt)
    expand = (jnp.arange(LANES)[:, None] == (jnp.arange(SSD_D_INNER) // SSD_HEADDIM)[None, :]).astype(F32)
    wuk = jnp.pad(jnp.transpose(wkv[:, :, :MLA_NOPE], (0, 2, 1)), ((0, 0), (0, LANES - MLA_NOPE), (0, 0)))
    wuv = jnp.pad(wkv[:, :, MLA_NOPE:], ((0, 0), (0, 0), (0, LANES - MLA_V)))
    return dict(
        w_ssd=w_ssd, w_mla=w_mla, wq=wq, wkv=wkv, wuk=wuk, wuv=wuv, expand=expand,
        wy=W['ev_w_out'][j][:SSD_D_INNER].astype(dt), wo=W['ev_w_out'][j][SSD_D_INNER:].astype(dt),
        cw=W['ssd_conv_w'][j], cb=W['ssd_conv_b'][j].reshape(1, SSD_XBC),
        dtb=_pad_lanes(W['ssd_dt_bias'][j]), alog=_pad_lanes(W['ssd_a_log'][j]),
        dsk=jnp.repeat(W['ssd_d'][j].astype(F32), SSD_HEADDIM).reshape(1, SSD_D_INNER),
        nrm=W['ssd_norm'][j].reshape(1, SSD_D_INNER),
        qn=W['mla_q_norm'][j].reshape(1, MLA_Q_RANK), kvn=W['mla_kv_norm'][j].reshape(1, MLA_KV_RANK))


def _odd_weights(W, j, dt):
    w_in = W['od_w_in'][j]
    o1 = GDN_QKV
    o2 = o1 + GDN_HEADS * GDN_DV
    o3 = o2 + 2 * GDN_HEADS
    w_gdn = jnp.concatenate([w_in[:, :o2], _pad_cols(w_in[:, o2:o3], LANES)], axis=1).astype(dt)
    w_sc = w_in[:, o3:].astype(dt)
    return dict(
        w_gdn=w_gdn, w_sc=w_sc,
        wy=W['od_w_out'][j][:GDN_HEADS * GDN_DV].astype(dt), wo=W['od_w_out'][j][GDN_HEADS * GDN_DV:].astype(dt),
        cw=W['gdn_conv_w'][j], dtb=_pad_lanes(W['gdn_dt_bias'][j], GDN_HEADS),
        alog=_pad_lanes(W['gdn_a_log'][j], GDN_HEADS), nrm=W['gdn_norm'][j].reshape(1, GDN_DV),
        scw=W['sconv_w'][j])


def _moe(h, logits, x_new, mod, W, i, g_final, L, final):
    n_tok = h.shape[0]
    bm = MOE_BM if n_tok * TOP_K >= N_EXPERTS * MOE_BM else MOE_BM_SMALL
    e_pad, rank_pad, gates, cnt = _route(logits, 512)
    counts = cnt[0, :N_EXPERTS]
    padded = (counts + bm - 1) // bm * bm
    pend = jnp.cumsum(padded)
    pstart = pend - padded
    e_sel = e_pad[:, :TOP_K]
    dest = (pstart[e_sel] + rank_pad[:, :TOP_K]).astype(I32).reshape(-1)
    n_blocks = n_tok * TOP_K // bm + N_EXPERTS
    n_rows = n_blocks * bm
    blk_start = jnp.arange(n_blocks, dtype=pend.dtype) * bm
    blk_e = jnp.minimum(jnp.sum((blk_start[:, None] >= pend[None, :]).astype(I32), axis=1), N_EXPERTS - 1)
    n_used = (pend[-1:] // bm).astype(I32)
    zero_start = jnp.where(counts > 0, pend - bm, -1).astype(I32)
    n_zero = jnp.sum((counts > 0).astype(I32)) + n_blocks - n_used[0]
    zero_counts = jnp.stack([n_used[0], n_zero]).astype(I32)
    xin = _dispatch(h, dest, zero_start, zero_counts, n_rows, DISPATCH_TM, bm)
    f_out = _expert_ffn(xin, blk_e, n_used, W['w_gu'], W['b_gu'], W['w_down'], W['b_down'], i, bm)
    tm = min(256, L)
    return _combine(f_out, dest, gates, x_new, mod, g_final, L // tm, tm, final)


def _trunk(x, c_mod, pos0, caches, W, PW, seq_t, hi):
    B, L, _ = x.shape
    new = {}
    pos = pos0 + jnp.arange(L, dtype=I32)
    cos_t, sin_a, sin_b = _rope_tables(pos)
    wr_all = W['w_router']
    for i in range(DEPTH):
        mod = c_mod[i]
        g_mix = W['norm_mix'][i].reshape(1, D_MODEL)
        j = i // 2
        if i % 2 == 0:
            P = PW[i]
            z, xbc, dtr = _in_proj(x, mod, g_mix, P['w_ssd'],
                                   ((0, SSD_D_INNER), (SSD_D_INNER, SSD_XBC), (SSD_D_INNER + SSD_XBC, LANES)),
                                   (F32, F32, F32), 0, 1, 512, hi)
            latq, latkv = _in_proj(x, mod, g_mix, P['w_mla'], ((0, MLA_Q_RANK), (MLA_Q_RANK, MLA_Q_RANK)),
                                   (F32, F32), 0, 1, 1024, hi)
            T = seq_t['ssd']
            lp = -(-L // T) * T
            y, s_new, cst_new = _ssd(_pad_seq(z, lp), _pad_seq(xbc, lp), _pad_seq(dtr, lp),
                                     caches['ssd_conv'][j], caches['ssd'][j].reshape(B, -1, SSD_STATE),
                                     P['cw'], P['cb'], P['dtb'], P['alog'], P['dsk'], P['nrm'], P['expand'], T, L, hi)
            y = y[:, :L]
            new['ssd'] = s_new.reshape(1, B, SSD_HEADS, SSD_HEADDIM, SSD_STATE)
            new['ssd_conv'] = cst_new[None]
            ckv_new, kpe_new = _latkv_post(latkv, P['kvn'], cos_t, sin_a, sin_b, 1024)
            new['mla_ckv'] = ckv_new[None]
            new['mla_krope'] = kpe_new[None, :, :, ROPE_LANE:ROPE_LANE + MLA_ROPE]
            ckv_past, kpe_past = caches['mla_ckv'][j], caches['mla_krope'][j]
            past = ckv_past.shape[1]
            kv_len = past + L
            tk = seq_t['tk']
            lk = -(-kv_len // tk) * tk
            ckv_all = _pad_seq(jnp.concatenate([ckv_past, ckv_new], axis=1), lk)
            kpe_all = _pad_seq(jnp.concatenate(
                [jnp.pad(kpe_past, ((0, 0), (0, 0), (ROPE_LANE, LANES - ROPE_LANE - MLA_ROPE))), kpe_new],
                axis=1), lk)
            q = _q_proj(latq, P['qn'], P['wq'], cos_t, sin_a, sin_b, 1024, hi)
            if seq_t['latent']:
                o = _attention_latent(q, ckv_all, kpe_all, P['wuk'], P['wuv'], pos0, kv_len, hi)
            else:
                kv = _kv_up(ckv_all, kpe_all, P['wkv'], min(tk, 512), hi)
                o = _attention(q, kv, min(seq_t['tq'], L), tk, seq_t['tkm'], pos0, kv_len, hi)
        else:
            P = PW[i]
            hv = GDN_HEADS * GDN_DV
            qkv, gate, ba = _in_proj(x, mod, g_mix, P['w_gdn'],
                                     ((0, GDN_QKV), (GDN_QKV, hv), (GDN_QKV + hv, LANES)),
                                     (F32, F32, F32), 0, 1, 512, hi)
            scb, scc, scv = _in_proj(x, mod, g_mix, P['w_sc'],
                                     ((0, SC_WIDTH), (SC_WIDTH, SC_WIDTH), (2 * SC_WIDTH, SC_WIDTH)),
                                     (F32, F32, F32), 0, 1, 512, hi)
            T = CHUNK
            lp = -(-L // T) * T
            y, s_new, cst_new = _gdn(_pad_seq(qkv, lp), _pad_seq(gate, lp), _pad_seq(ba, lp),
                                     caches['gdn_conv'][j], caches['gdn'][j].reshape(B, -1, GDN_DV),
                                     P['cw'], P['dtb'], P['alog'], P['nrm'], T, L, hi)
            y = y[:, :L]
            new['gdn'] = s_new.reshape(1, B, GDN_HEADS, GDN_DK, GDN_DV)
            new['gdn_conv'] = cst_new[None]
            o, sc_new = _sconv(scb, scc, scv, caches['sconv'][j], P['scw'], 512, hi)
            new['sconv'] = sc_new[None]
        wr = _pad_cols(wr_all[i], LANES)
        br = jnp.concatenate([W['b_router'][i].astype(F32), jnp.full((LANES - N_EXPERTS,), NEG_BIG, F32)]).reshape(1, LANES)
        x_new, h, logits = _out_proj(y, o, x, mod, P['wy'], P['wo'], W['norm_ffn'][i].reshape(1, D_MODEL),
                                     wr, br, 512, hi)
        final = i == DEPTH - 1
        xo = _moe(h.reshape(B * L, D_MODEL), logits.reshape(B * L, LANES), x_new.reshape(B * L, D_MODEL),
                  mod, W, i, W['norm_final'].reshape(1, D_MODEL), L, final)
        x = xo.reshape(B, L, D_MODEL)
    return x, new


def _prep_weights(W, dt):
    PW = {}
    for i in range(DEPTH):
        PW[i] = _even_weights(W, i // 2, dt) if i % 2 == 0 else _odd_weights(W, i // 2, dt)
    return PW


def kernel(x_prompt, x_sample, c_prompt, c_sample, cache_mla_ckv, cache_mla_krope, state_ssd, state_ssd_conv, state_gdn, state_gdn_conv, state_sconv, norm_mix, norm_ffn, w_ada, b_ada, w_router, b_router, w_gu, b_gu, w_down, b_down, norm_final, ev_w_in, ev_w_out, ssd_conv_w, ssd_conv_b, ssd_dt_bias, ssd_a_log, ssd_d, ssd_norm, mla_q_norm, mla_w_q_up, mla_kv_norm, mla_w_kv_up, od_w_in, od_w_out, gdn_conv_w, gdn_dt_bias, gdn_a_log, gdn_norm, sconv_w):
    W = dict(norm_mix=norm_mix, norm_ffn=norm_ffn, w_ada=w_ada, b_ada=b_ada, w_router=w_router,
             b_router=b_router, w_gu=w_gu, b_gu=b_gu, w_down=w_down, b_down=b_down, norm_final=norm_final,
             ev_w_in=ev_w_in, ev_w_out=ev_w_out, ssd_conv_w=ssd_conv_w, ssd_conv_b=ssd_conv_b,
             ssd_dt_bias=ssd_dt_bias, ssd_a_log=ssd_a_log, ssd_d=ssd_d, ssd_norm=ssd_norm,
             mla_q_norm=mla_q_norm, mla_w_q_up=mla_w_q_up, mla_kv_norm=mla_kv_norm, mla_w_kv_up=mla_w_kv_up,
             od_w_in=od_w_in, od_w_out=od_w_out, gdn_conv_w=gdn_conv_w, gdn_dt_bias=gdn_dt_bias,
             gdn_a_log=gdn_a_log, gdn_norm=gdn_norm, sconv_w=sconv_w)
    bp, bs = x_prompt.shape[0], x_sample.shape[0]
    nb = 16
    c_all = jnp.concatenate([c_prompt, c_sample, jnp.zeros((nb - bp - bs, D_MODEL), F32)], axis=0)
    mod_all = _ada_mod(c_all, w_ada, b_ada).reshape(DEPTH, nb, 6, D_MODEL)
    n_even, n_odd = (DEPTH + 1) // 2, DEPTH // 2
    zero_caches = dict(
        mla_ckv=jnp.zeros((n_even, bp, 0, MLA_KV_RANK), F32), mla_krope=jnp.zeros((n_even, bp, 0, MLA_ROPE), F32),
        ssd=jnp.zeros((n_even, bp, SSD_HEADS, SSD_HEADDIM, SSD_STATE), F32),
        ssd_conv=jnp.zeros((n_even, bp, SSD_CONV - 1, SSD_XBC), F32),
        gdn=jnp.zeros((n_odd, bp, GDN_HEADS, GDN_DK, GDN_DV), F32),
        gdn_conv=jnp.zeros((n_odd, bp, GDN_CONV - 1, GDN_QKV), F32),
        sconv=jnp.zeros((n_odd, bp, SC_CONV - 1, SC_WIDTH), F32))
    y_p, sp = _trunk(x_prompt, mod_all[:, :bp], 0, zero_caches, W, _prep_weights(W, BF16),
                     dict(ssd=256, tq=1024, tk=1024, tkm=512, latent=False), False)
    past = cache_mla_ckv.shape[2]
    caches = dict(mla_ckv=cache_mla_ckv, mla_krope=cache_mla_krope, ssd=state_ssd, ssd_conv=state_ssd_conv,
                  gdn=state_gdn, gdn_conv=state_gdn_conv, sconv=state_sconv)
    y_s, ss = _trunk(x_sample, mod_all[:, bp:bp + bs], past, caches, W, _prep_weights(W, F32),
                     dict(ssd=128, tq=32, tk=256, tkm=256, latent=True), True)
    return (y_p, y_s,
            sp['mla_ckv'], ss['mla_ckv'], sp['mla_krope'], ss['mla_krope'],
            sp['ssd'], ss['ssd'], sp['ssd_conv'], ss['ssd_conv'],
            sp['gdn'], ss['gdn'], sp['gdn_conv'], ss['gdn_conv'],
            sp['sconv'], ss['sconv'])
```

```python
import functools
import math

import jax
import jax.numpy as jnp
from jax import lax
from jax.experimental import pallas as pl
from jax.experimental.pallas import tpu as pltpu

F32 = jnp.float32
BF16 = jnp.bfloat16
I32 = jnp.int32

D_MODEL = 1024
DEPTH = 2
CHUNK = 64
CHUNK_SHIFT = 6
EPS = 1e-6
SSD_D_INNER = D_MODEL
SSD_HEADDIM = 64
SSD_HEADS = SSD_D_INNER // SSD_HEADDIM
SSD_GROUPS = 4
SSD_STATE = 128
SSD_CONV = 4
SSD_XBC = SSD_D_INNER + 2 * SSD_GROUPS * SSD_STATE
MLA_HEADS = 16
MLA_NOPE = 64
MLA_ROPE = 32
MLA_V = 64
MLA_Q_RANK = 384
MLA_KV_RANK = 256
ROPE_THETA = 10000.0
ROPE_LANE = MLA_NOPE
GDN_HEADS = 8
GDN_DK = 128
GDN_DV = 128
GDN_CONV = 4
GDN_QKV = GDN_HEADS * (2 * GDN_DK + GDN_DV)
SC_WIDTH = D_MODEL
SC_CONV = 3
N_EXPERTS = 32
TOP_K = 4
D_FF = D_MODEL
SWIGLU_LIMIT = 7.0
SWIGLU_ALPHA = 1.702

LANES = 128
SUBLANES = 8
VMEM_LIMIT = 56 * 1024 * 1024

NEG_BIG = -1e30
MOE_BM = 512
MOE_BM_SMALL = 128
DISPATCH_TM = 2048


def _cparams(sem):
    return pltpu.CompilerParams(dimension_semantics=sem, vmem_limit_bytes=VMEM_LIMIT)


def _sigmoid(x):
    return 1.0 / (1.0 + jnp.exp(-x))


def _silu(x):
    return x * _sigmoid(x)


def _softplus(x):
    return jnp.maximum(x, 0.0) + jnp.log1p(jnp.exp(-jnp.abs(x)))


_NN = (((1,), (0,)), ((), ()))
_NT = (((1,), (1,)), ((), ()))


def _dot_split(a, b, dims=_NN):
    a_h = a.astype(BF16)
    b_h = b.astype(BF16)
    a_l = (a - a_h.astype(F32)).astype(BF16)
    b_l = (b - b_h.astype(F32)).astype(BF16)
    d = functools.partial(lax.dot_general, dimension_numbers=dims, preferred_element_type=F32)
    return d(a_h, b_h) + d(a_l, b_h) + d(a_h, b_l)


def _dot(a, b, hi=False):
    if hi:
        return _dot_split(a.astype(F32), b.astype(F32))
    return jnp.dot(a.astype(BF16), b.astype(BF16), preferred_element_type=F32)


def _dot_nt(a, b, hi=False):
    if hi:
        return _dot_split(a.astype(F32), b.astype(F32), _NT)
    return lax.dot_general(a.astype(BF16), b.astype(BF16), _NT, preferred_element_type=F32)


def _split3(a):
    a1 = a.astype(BF16)
    r = a - a1.astype(F32)
    a2 = r.astype(BF16)
    a3 = (r - a2.astype(F32)).astype(BF16)
    return a1, a2, a3


def _dot_sel(sel, a, dims=_NN, sel_left=True):
    s = sel.astype(BF16)
    d = functools.partial(lax.dot_general, dimension_numbers=dims, preferred_element_type=F32)
    if sel_left:
        return sum(d(s, p) for p in _split3(a))
    return sum(d(p, s) for p in _split3(a))


def _act_dtype(hi):
    return F32 if hi else BF16


def _rope_rot(p, cos_t, sin_a, sin_b):
    return p * cos_t + pltpu.roll(p, LANES - MLA_ROPE // 2, 1) * sin_a + pltpu.roll(p, MLA_ROPE // 2, 1) * sin_b


def _ada_kernel(c_ref, w_ref, b_ref, o_ref):
    c = c_ref[...]
    o_ref[0] = _dot_split(_silu(c), w_ref[0]) + b_ref[0]


def _ada_mod(c_all, w_ada, b_ada):
    nb = c_all.shape[0]
    return pl.pallas_call(
        _ada_kernel,
        grid=(DEPTH, 6),
        in_specs=[pl.BlockSpec((nb, D_MODEL), lambda i, j: (0, 0)),
                  pl.BlockSpec((1, D_MODEL, D_MODEL), lambda i, j: (i, 0, j)),
                  pl.BlockSpec((1, 1, D_MODEL), lambda i, j: (i, 0, j))],
        out_specs=pl.BlockSpec((1, nb, D_MODEL), lambda i, j: (i, 0, j)),
        out_shape=jax.ShapeDtypeStruct((DEPTH, nb, 6 * D_MODEL), F32),
        compiler_params=_cparams(("parallel", "parallel")),
        name="ada_mod",
    )(c_all, w_ada, b_ada.reshape(DEPTH, 1, 6 * D_MODEL))


def _in_kernel(x_ref, mod_ref, g_ref, w_ref, *out_refs, segs, shift_row, scale_row, hi):
    x = x_ref[0]
    h = x * lax.rsqrt(jnp.mean(x * x, axis=-1, keepdims=True) + EPS) * g_ref[...]
    h = h * (1.0 + mod_ref[0, scale_row:scale_row + 1, :]) + mod_ref[0, shift_row:shift_row + 1, :]
    r = _dot(h, w_ref[...], hi)
    for (off, width), o_ref in zip(segs, out_refs):
        o_ref[0] = r[:, off:off + width].astype(o_ref.dtype)


def _in_proj(x, mod, g, w, segs, dtypes, shift_row, scale_row, tm, hi):
    B, L, _ = x.shape
    n_p = w.shape[1]
    tm = min(tm, L)
    return pl.pallas_call(
        functools.partial(_in_kernel, segs=segs, shift_row=shift_row, scale_row=scale_row, hi=hi),
        grid=(B, L // tm),
        in_specs=[pl.BlockSpec((1, tm, D_MODEL), lambda b, i: (b, i, 0)),
                  pl.BlockSpec((1, 6, D_MODEL), lambda b, i: (b, 0, 0)),
                  pl.BlockSpec((1, D_MODEL), lambda b, i: (0, 0)),
                  pl.BlockSpec((D_MODEL, n_p), lambda b, i: (0, 0))],
        out_specs=[pl.BlockSpec((1, tm, wd), lambda b, i: (b, i, 0)) for _, wd in segs],
        out_shape=[jax.ShapeDtypeStruct((B, L, wd), dt) for (_, wd), dt in zip(segs, dtypes)],
        compiler_params=_cparams(("parallel", "parallel")),
        name="in_proj",
    )(x, mod, g, w)


def _ssd_kernel(z_ref, xbc_ref, dtr_ref, cst_ref, s0_ref, cw_ref, cb_ref, dtb_ref, alog_ref, dsk_ref,
                nrm_ref, e_ref, y_ref, sout_ref, cout_ref, state_sc, cbuf, *, T, n_valid, hi):
    c = pl.program_id(1)
    nc = pl.num_programs(1)
    halo = SSD_CONV - 1
    base = SUBLANES - halo

    @pl.when(c == 0)
    def _():
        state_sc[...] = s0_ref[0]
        cbuf[base:SUBLANES, :] = cst_ref[0]

    cbuf[SUBLANES:SUBLANES + T, :] = xbc_ref[0]
    conv = cb_ref[...] + cw_ref[0:1, :] * cbuf[base:base + T, :]
    for k in range(1, SSD_CONV):
        conv = conv + cw_ref[k:k + 1, :] * cbuf[base + k:base + k + T, :]
    xc = _silu(conv)

    c_last = (n_valid - 1) // T
    nv_last = n_valid - c_last * T

    @pl.when(c == c_last)
    def _():
        cout_ref[0] = cbuf[base + nv_last:base + nv_last + halo, :]

    cbuf[base:SUBLANES, :] = cbuf[base + T:SUBLANES + T, :]

    xs = xc[:, :SSD_D_INNER]
    gn = SSD_GROUPS * SSD_STATE
    bm = xc[:, SSD_D_INNER:SSD_D_INNER + gn]
    cm = xc[:, SSD_D_INNER + gn:]

    tok = lax.broadcasted_iota(I32, (T, 1), 0) + c * T
    dt = jnp.where(tok < n_valid, _softplus(dtr_ref[0] + dtb_ref[...]), 0.0)
    a = dt * (-jnp.exp(alog_ref[...]))
    ri = lax.broadcasted_iota(I32, (T, T), 0)
    ci = lax.broadcasted_iota(I32, (T, T), 1)
    causal = ci <= ri
    tril = jnp.where(causal, 1.0, 0.0).astype(F32)
    acum = _dot_sel(tril, a)
    eye = jnp.where(lax.broadcasted_iota(I32, (LANES, LANES), 0) == lax.broadcasted_iota(I32, (LANES, LANES), 1),
                    1.0, 0.0).astype(F32)
    acum_t = _dot_sel(eye, acum, _NT)
    a_last = acum[T - 1:T, :]
    e = e_ref[...]
    xdt = xs * _dot_sel(e, dt, sel_left=False)
    eacum_x = jnp.exp(_dot_sel(e, acum, sel_left=False))
    xdend = xdt * jnp.exp(_dot_sel(e, a_last - acum, sel_left=False))

    r = SSD_HEADS // SSD_GROUPS
    gw = r * SSD_HEADDIM
    y_groups = []
    for g in range(SSD_GROUPS):
        bg = bm[:, g * SSD_STATE:(g + 1) * SSD_STATE]
        cg = cm[:, g * SSD_STATE:(g + 1) * SSD_STATE]
        cb_mat = _dot_nt(cg, bg, hi)
        ys = []
        for j in range(r):
            h = g * r + j
            seg = acum[:, h:h + 1] - acum_t[h:h + 1, :]
            lm = jnp.where(causal, jnp.exp(jnp.minimum(seg, 0.0)), 0.0)
            ys.append(_dot(cb_mat * lm, xdt[:, h * SSD_HEADDIM:(h + 1) * SSD_HEADDIM], hi))
        y_diag = jnp.concatenate(ys, axis=-1)
        s_g = state_sc[g * gw:(g + 1) * gw, :]
        y_off = _dot_nt(cg, s_g, hi) * eacum_x[:, g * gw:(g + 1) * gw]
        y_groups.append(y_diag + y_off)
        cs = _dot(jnp.transpose(xdend[:, g * gw:(g + 1) * gw]), bg, hi)
        dec = jnp.concatenate(
            [jnp.broadcast_to(jnp.exp(acum_t[g * r + j:g * r + j + 1, T - 1:T]), (SSD_HEADDIM, SSD_STATE))
             for j in range(r)], axis=0)
        state_sc[g * gw:(g + 1) * gw, :] = s_g * dec + cs
    y = jnp.concatenate(y_groups, axis=-1)
    y = y + xs * dsk_ref[...]
    y = y * _silu(z_ref[0])
    outs = []
    for g in range(SSD_GROUPS):
        yg = y[:, g * gw:(g + 1) * gw]
        outs.append(yg * lax.rsqrt(jnp.mean(yg * yg, axis=-1, keepdims=True) + EPS))
    y_ref[0] = (jnp.concatenate(outs, axis=-1) * nrm_ref[...]).astype(y_ref.dtype)

    @pl.when(c == nc - 1)
    def _():
        sout_ref[0] = state_sc[...]


def _ssd(z, xbc, dtr, cst, s0, cw, cb, dtb, alog, dsk, nrm, e, T, n_valid, hi):
    B, Lp, _ = z.shape
    hp = SSD_HEADS * SSD_HEADDIM
    row = lambda b, c: (0, 0)
    return pl.pallas_call(
        functools.partial(_ssd_kernel, T=T, n_valid=n_valid, hi=hi),
        grid=(B, Lp // T),
        in_specs=[pl.BlockSpec((1, T, SSD_D_INNER), lambda b, c: (b, c, 0)),
                  pl.BlockSpec((1, T, SSD_XBC), lambda b, c: (b, c, 0)),
                  pl.BlockSpec((1, T, LANES), lambda b, c: (b, c, 0)),
                  pl.BlockSpec((1, SSD_CONV - 1, SSD_XBC), lambda b, c: (b, 0, 0)),
                  pl.BlockSpec((1, hp, SSD_STATE), lambda b, c: (b, 0, 0)),
                  pl.BlockSpec((SSD_CONV, SSD_XBC), row),
                  pl.BlockSpec((1, SSD_XBC), row),
                  pl.BlockSpec((1, LANES), row),
                  pl.BlockSpec((1, LANES), row),
                  pl.BlockSpec((1, SSD_D_INNER), row),
                  pl.BlockSpec((1, SSD_D_INNER), row),
                  pl.BlockSpec((LANES, SSD_D_INNER), row)],
        out_specs=[pl.BlockSpec((1, T, SSD_D_INNER), lambda b, c: (b, c, 0)),
                   pl.BlockSpec((1, hp, SSD_STATE), lambda b, c: (b, 0, 0)),
                   pl.BlockSpec((1, SSD_CONV - 1, SSD_XBC), lambda b, c: (b, 0, 0))],
        out_shape=[jax.ShapeDtypeStruct((B, Lp, SSD_D_INNER), _act_dtype(hi)),
                   jax.ShapeDtypeStruct((B, hp, SSD_STATE), F32),
                   jax.ShapeDtypeStruct((B, SSD_CONV - 1, SSD_XBC), F32)],
        scratch_shapes=[pltpu.VMEM((hp, SSD_STATE), F32),
                        pltpu.VMEM((T + SUBLANES, SSD_XBC), F32)],
        compiler_params=_cparams(("parallel", "arbitrary")),
        name="ssd_scan",
    )(z, xbc, dtr, cst, s0, cw, cb, dtb, alog, dsk, nrm, e)


def _latkv_kernel(lat_ref, g_ref, cos_ref, sa_ref, sb_ref, ckv_ref, kpe_ref):
    lat = lat_ref[0]
    cr = lat[:, :MLA_KV_RANK]
    ckv_ref[0] = cr * lax.rsqrt(jnp.mean(cr * cr, axis=-1, keepdims=True) + EPS) * g_ref[...]
    kpe_ref[0] = _rope_rot(lat[:, MLA_KV_RANK:], cos_ref[...], sa_ref[...], sb_ref[...])


def _latkv_post(latkv, g, cos_t, sin_a, sin_b, tm):
    B, L, wp = latkv.shape
    tm = min(tm, L)
    tab = pl.BlockSpec((tm, LANES), lambda b, i: (i, 0))
    return pl.pallas_call(
        _latkv_kernel,
        grid=(B, L // tm),
        in_specs=[pl.BlockSpec((1, tm, wp), lambda b, i: (b, i, 0)),
                  pl.BlockSpec((1, MLA_KV_RANK), lambda b, i: (0, 0)), tab, tab, tab],
        out_specs=[pl.BlockSpec((1, tm, MLA_KV_RANK), lambda b, i: (b, i, 0)),
                   pl.BlockSpec((1, tm, LANES), lambda b, i: (b, i, 0))],
        out_shape=[jax.ShapeDtypeStruct((B, L, MLA_KV_RANK), F32),
                   jax.ShapeDtypeStruct((B, L, LANES), F32)],
        compiler_params=_cparams(("parallel", "parallel")),
        name="mla_latent_kv",
    )(latkv, g, cos_t, sin_a, sin_b)


def _q_kernel(lat_ref, g_ref, w_ref, cos_ref, sa_ref, sb_ref, q_ref, *, scale, hi):
    lat = lat_ref[0]
    n = lat * lax.rsqrt(jnp.mean(lat * lat, axis=-1, keepdims=True) + EPS) * g_ref[...]
    cos_t, sin_a, sin_b = cos_ref[...], sa_ref[...], sb_ref[...]
    nope = lax.broadcasted_iota(I32, (lat.shape[0], LANES), 1) < ROPE_LANE
    for h in range(MLA_HEADS):
        q = _dot(n, w_ref[h], hi)
        p = _rope_rot(jnp.where(nope, 0.0, q), cos_t, sin_a, sin_b)
        q_ref[0, h] = (jnp.where(nope, q, p) * scale).astype(q_ref.dtype)


def _q_proj(latq, g, wq, cos_t, sin_a, sin_b, tm, hi):
    B, L, _ = latq.shape
    tm = min(tm, L)
    tab = pl.BlockSpec((tm, LANES), lambda b, i: (i, 0))
    scale = (MLA_NOPE + MLA_ROPE) ** -0.5 * math.log2(math.e)
    return pl.pallas_call(
        functools.partial(_q_kernel, scale=scale, hi=hi),
        grid=(B, L // tm),
        in_specs=[pl.BlockSpec((1, tm, MLA_Q_RANK), lambda b, i: (b, i, 0)),
                  pl.BlockSpec((1, MLA_Q_RANK), lambda b, i: (0, 0)),
                  pl.BlockSpec((MLA_HEADS, MLA_Q_RANK, LANES), lambda b, i: (0, 0, 0)), tab, tab, tab],
        out_specs=pl.BlockSpec((1, MLA_HEADS, tm, LANES), lambda b, i: (b, 0, i, 0)),
        out_shape=jax.ShapeDtypeStruct((B, MLA_HEADS, L, LANES), _act_dtype(hi)),
        compiler_params=_cparams(("parallel", "parallel")),
        name="mla_q_proj",
    )(latq, g, wq, cos_t, sin_a, sin_b)


def _kvup_kernel(ckv_ref, kpe_ref, w_ref, kv_ref, *, hi):
    ckv = ckv_ref[0]
    kpe = kpe_ref[0]
    nope = lax.broadcasted_iota(I32, kpe.shape, 1) < ROPE_LANE
    for h in range(MLA_HEADS):
        kv = _dot(ckv, w_ref[h], hi)
        key = jnp.where(nope, kv, kpe)
        val = jnp.where(nope, pltpu.roll(kv, MLA_V, 1), 0.0)
        kv_ref[0, h] = jnp.concatenate([key, val], axis=-1).astype(kv_ref.dtype)


def _kv_up(ckv, kpe, wkv, tm, hi):
    B, Lk, _ = ckv.shape
    tm = min(tm, Lk)
    return pl.pallas_call(
        functools.partial(_kvup_kernel, hi=hi),
        grid=(B, Lk // tm),
        in_specs=[pl.BlockSpec((1, tm, MLA_KV_RANK), lambda b, i: (b, i, 0)),
                  pl.BlockSpec((1, tm, LANES), lambda b, i: (b, i, 0)),
                  pl.BlockSpec((MLA_HEADS, MLA_KV_RANK, LANES), lambda b, i: (0, 0, 0))],
        out_specs=pl.BlockSpec((1, MLA_HEADS, tm, 2 * LANES), lambda b, i: (b, 0, i, 0)),
        out_shape=jax.ShapeDtypeStruct((B, MLA_HEADS, Lk, 2 * LANES), _act_dtype(hi)),
        compiler_params=_cparams(("parallel", "parallel")),
        name="mla_kv_up",
    )(ckv, kpe, wkv)


def _attn_kernel(q_ref, kv_ref, o_ref, m_sc, l_sc, acc_sc, *, tq, tk, tkm, n_sub, q_off, kv_len, hi):
    q0 = pl.program_id(2) * tq
    first = q_off + q0

    def seen_by_all(f):
        return jnp.minimum(kv_len, (f // CHUNK + 1) * CHUNK)

    n_full = seen_by_all(first) // tk
    m_lo = n_full * (tk // tkm)

    def block(r0, rows, k0, width, masked):
        hds = range(2)
        rs = slice(r0, r0 + rows)
        k = [kv_ref[0, hd, pl.ds(k0, width), :] for hd in hds]
        s = [_dot_nt(q_ref[0, hd, rs, :], k[hd][:, :LANES], hi) for hd in hds]
        if masked:
            qchunk = jnp.right_shift(first + r0 + lax.broadcasted_iota(I32, (rows, 1), 0), CHUNK_SHIFT)
            kpos = k0 + lax.broadcasted_iota(I32, (1, width), 1)
            vis = jnp.logical_and(jnp.right_shift(kpos, CHUNK_SHIFT) <= qchunk, kpos < kv_len)
            s = [jnp.where(vis, s_, NEG_BIG) for s_ in s]
        m_prev = [m_sc[hd, rs, :] for hd in hds]
        m_new = [jnp.maximum(m_prev[hd], jnp.max(s[hd], axis=-1, keepdims=True)) for hd in hds]
        alpha = [jnp.exp2(m_prev[hd] - m_new[hd]) for hd in hds]
        p = [jnp.exp2(s[hd] - jnp.tile(m_new[hd], (1, width // LANES))) for hd in hds]
        pv = [_dot(p[hd], k[hd][:, LANES:], hi) for hd in hds]
        for hd in hds:
            l_sc[hd, rs, :] = alpha[hd] * l_sc[hd, rs, :] + jnp.sum(p[hd], axis=-1, keepdims=True)
            acc_sc[hd, rs, :] = alpha[hd] * acc_sc[hd, rs, :] + pv[hd]
            m_sc[hd, rs, :] = m_new[hd]

    m_sc[...] = jnp.full((2, tq, LANES), NEG_BIG, F32)
    l_sc[...] = jnp.zeros((2, tq, LANES), F32)
    acc_sc[...] = jnp.zeros((2, tq, LANES), F32)

    def full_body(j, carry):
        block(0, tq, pl.multiple_of(j * tk, tk), tk, False)
        return carry

    lax.fori_loop(0, n_full, full_body, 0)
    rows = tq // n_sub
    for r in range(n_sub):
        f = first + r * rows
        u_hi = jnp.maximum(m_lo, seen_by_all(f) // tkm)
        n_any = jnp.minimum(kv_len, ((f + rows - 1) // CHUNK + 1) * CHUNK)
        m_hi = (n_any + tkm - 1) // tkm

        def open_body(j, carry, r=r):
            block(r * rows, rows, pl.multiple_of(j * tkm, tkm), tkm, False)
            return carry

        def masked_body(j, carry, r=r):
            block(r * rows, rows, pl.multiple_of(j * tkm, tkm), tkm, True)
            return carry

        lax.fori_loop(m_lo, u_hi, open_body, 0)
        lax.fori_loop(u_hi, m_hi, masked_body, 0)
    outs = [acc_sc[hd] / l_sc[hd] for hd in range(2)]
    lane = lax.broadcasted_iota(I32, (tq, LANES), 1)
    o = jnp.where(lane < MLA_V, outs[0], pltpu.roll(outs[1], MLA_V, 1))
    o_ref[0] = o.astype(o_ref.dtype)


def _attention(q, kv, tq, tk, tkm, q_off, kv_len, hi):
    B, H, L, _ = q.shape
    Lk = kv.shape[2]
    n_sub = 2 if tq >= 2 * tkm else 1
    return pl.pallas_call(
        functools.partial(_attn_kernel, tq=tq, tk=tk, tkm=tkm, n_sub=n_sub, q_off=q_off, kv_len=kv_len, hi=hi),
        grid=(B, H // 2, L // tq),
        in_specs=[pl.BlockSpec((1, 2, tq, LANES), lambda b, h, i: (b, h, i, 0)),
                  pl.BlockSpec((1, 2, Lk, 2 * LANES), lambda b, h, i: (b, h, 0, 0))],
        out_specs=pl.BlockSpec((1, tq, LANES), lambda b, h, i: (b, i, h)),
        out_shape=jax.ShapeDtypeStruct((B, L, H * MLA_V), _act_dtype(hi)),
        scratch_shapes=[pltpu.VMEM((2, tq, LANES), F32)] * 3,
        compiler_params=_cparams(("parallel", "parallel", "parallel")),
        name="mla_attention",
    )(q, kv)


def _attn_latent_kernel(q_ref, ckv_ref, kpe_ref, wuk_ref, wuv_ref, o_ref, *, L, q_off, kv_len, hi):
    H = MLA_HEADS
    lk = ckv_ref.shape[1]
    ckv = ckv_ref[0]
    kpe = kpe_ref[0]
    qa = jnp.concatenate([_dot(q_ref[0, h], wuk_ref[h], hi) for h in range(H)], axis=0)
    qp = jnp.concatenate([q_ref[0, h] for h in range(H)], axis=0)
    s = _dot_nt(qa, ckv, hi) + _dot_nt(qp, kpe, hi)
    row = lax.broadcasted_iota(I32, (H * L, 1), 0)
    qchunk = jnp.right_shift(q_off + jnp.bitwise_and(row, L - 1), CHUNK_SHIFT)
    kpos = lax.broadcasted_iota(I32, (1, lk), 1)
    vis = jnp.logical_and(jnp.right_shift(kpos, CHUNK_SHIFT) <= qchunk, kpos < kv_len)
    s = jnp.where(vis, s, NEG_BIG)
    p = jnp.exp2(s - jnp.max(s, axis=-1, keepdims=True))
    lat = _dot(p, ckv, hi) / jnp.sum(p, axis=-1, keepdims=True)
    outs = [_dot(lat[h * L:(h + 1) * L], wuv_ref[h], hi)[:, :MLA_V] for h in range(H)]
    o_ref[0] = jnp.concatenate(outs, axis=-1).astype(o_ref.dtype)


def _attention_latent(q, ckv, kpe, wuk, wuv, q_off, kv_len, hi):
    B, H, L, _ = q.shape
    lk = ckv.shape[1]
    return pl.pallas_call(
        functools.partial(_attn_latent_kernel, L=L, q_off=q_off, kv_len=kv_len, hi=hi),
        grid=(B,),
        in_specs=[pl.BlockSpec((1, H, L, LANES), lambda b: (b, 0, 0, 0)),
                  pl.BlockSpec((1, lk, MLA_KV_RANK), lambda b: (b, 0, 0)),
                  pl.BlockSpec((1, lk, LANES), lambda b: (b, 0, 0)),
                  pl.BlockSpec((H, LANES, MLA_KV_RANK), lambda b: (0, 0, 0)),
                  pl.BlockSpec((H, MLA_KV_RANK, LANES), lambda b: (0, 0, 0))],
        out_specs=pl.BlockSpec((1, L, H * MLA_V), lambda b: (b, 0, 0)),
        out_shape=jax.ShapeDtypeStruct((B, L, H * MLA_V), _act_dtype(hi)),
        compiler_params=_cparams(("parallel",)),
        name="mla_attention_latent",
    )(q, ckv, kpe, wuk, wuv)


def _gdn_kernel(qkv_ref, gate_ref, ba_ref, cst_ref, s0_ref, cw_ref, dtb_ref, alog_ref, nrm_ref,
                o_ref, sout_ref, cout_ref, state_sc, cbuf, *, bb_n, **kw):
    for b in range(bb_n):
        _gdn_stream(qkv_ref.at[b], gate_ref.at[b], ba_ref.at[b], cst_ref.at[b], s0_ref.at[b], cw_ref, dtb_ref,
                    alog_ref, nrm_ref, o_ref.at[b], sout_ref.at[b], cout_ref.at[b], state_sc.at[b], cbuf.at[b], **kw)


def _gdn_stream(qkv_ref, gate_ref, ba_ref, cst_ref, s0_ref, cw_ref, dtb_ref, alog_ref, nrm_ref,
                o_ref, sout_ref, cout_ref, state_sc, cbuf, *, T, n_valid, hi):
    c = pl.program_id(1)
    nc = pl.num_programs(1)
    halo = GDN_CONV - 1
    base = SUBLANES - halo
    H = GDN_HEADS
    DK = GDN_DK

    @pl.when(c == 0)
    def _():
        state_sc[...] = s0_ref[...]
        cbuf[base:SUBLANES, :] = cst_ref[...]

    cbuf[SUBLANES:SUBLANES + T, :] = qkv_ref[...]
    conv = cw_ref[0:1, :] * cbuf[base:base + T, :]
    for k in range(1, GDN_CONV):
        conv = conv + cw_ref[k:k + 1, :] * cbuf[base + k:base + k + T, :]
    qkv = _silu(conv)

    c_last = (n_valid - 1) // T
    nv_last = n_valid - c_last * T

    @pl.when(c == c_last)
    def _():
        cout_ref[...] = cbuf[base + nv_last:base + nv_last + halo, :]

    cbuf[base:SUBLANES, :] = cbuf[base + T:SUBLANES + T, :]

    tok = lax.broadcasted_iota(I32, (T, 1), 0) + c * T
    valid = tok < n_valid
    ba = ba_ref[...]
    beta = jnp.where(valid, _sigmoid(ba), 0.0)
    g = jnp.where(valid, -jnp.exp(alog_ref[...]) * _softplus(ba + dtb_ref[...]), 0.0)
    ri = lax.broadcasted_iota(I32, (T, T), 0)
    ci = lax.broadcasted_iota(I32, (T, T), 1)
    tril = jnp.where(ci <= ri, 1.0, 0.0).astype(F32)
    gc = _dot_sel(tril, g)

    def l2n(x):
        return x * lax.rsqrt(jnp.sum(x * x, axis=-1, keepdims=True) + EPS)

    qn = [l2n(qkv[:, h * DK:(h + 1) * DK]) * (DK ** -0.5) for h in range(H)]
    kn = [l2n(qkv[:, (H + h) * DK:(H + h + 1) * DK]) for h in range(H)]
    vv = [qkv[:, (2 * H + h) * DK:(2 * H + h + 1) * DK] for h in range(H)]

    G = 2
    W = G * T
    bi = lax.broadcasted_iota(I32, (W, W), 0)
    bj = lax.broadcasted_iota(I32, (W, W), 1)
    t_shift = T.bit_length() - 1
    same = jnp.right_shift(bi, t_shift) == jnp.right_shift(bj, t_shift)
    strict = jnp.logical_and(same, bj < bi)
    incl = jnp.logical_and(same, bj <= bi)
    eye_w = jnp.where(bi == bj, 1.0, 0.0).astype(F32)
    rowhead = jnp.right_shift(lax.broadcasted_iota(I32, (W, DK), 0), t_shift)

    groups = [[grp * G + j for j in range(G)] for grp in range(H // G)]
    n_g = len(groups)
    kst = [jnp.concatenate([kn[h] for h in hs], axis=0) for hs in groups]
    qst = [jnp.concatenate([qn[h] for h in hs], axis=0) for hs in groups]
    vst = [jnp.concatenate([vv[h] for h in hs], axis=0) for hs in groups]
    cb = [jnp.concatenate([jnp.broadcast_to(gc[:, H + h:H + h + 1], (T, W)) for h in hs], axis=0)
          for hs in groups]
    bb = [jnp.concatenate([jnp.broadcast_to(beta[:, h:h + 1], (T, DK)) for h in hs], axis=0) for hs in groups]
    glast = [jnp.concatenate([jnp.broadcast_to(gc[T - 1:T, H + h:H + h + 1], (T, DK)) for h in hs], axis=0)
             for hs in groups]
    dec = [jnp.exp(jnp.minimum(c_ - jnp.transpose(c_), 0.0)) for c_ in cb]
    kk = [_dot_nt(k_, k_, hi) for k_ in kst]
    qk = [_dot_nt(q_, k_, hi) for q_, k_ in zip(qst, kst)]
    a_mat = [jnp.where(strict, jnp.tile(b_, (1, W // DK)) * kk_ * d_, 0.0) for b_, kk_, d_ in zip(bb, kk, dec)]
    qkm = [jnp.where(incl, qk_ * d_, 0.0) for qk_, d_ in zip(qk, dec)]
    p_mat = [eye_w - a_ for a_ in a_mat]
    a_pow = a_mat
    for _ in range(T.bit_length() - 2):
        a_pow = [_dot_split(a_, a_) for a_ in a_pow]
        p_mat = [p_ + _dot_split(p_, a_) for p_, a_ in zip(p_mat, a_pow)]
    cbk = [c_[:, :DK] for c_ in cb]
    eg = [jnp.exp(c_) for c_ in cbk]
    rhs = [jnp.concatenate([v_ * b_, k_ * b_ * e_], axis=-1) for v_, b_, k_, e_ in zip(vst, bb, kst, eg)]
    sol = [_dot_split(p_, r_) for p_, r_ in zip(p_mat, rhs)]
    qdec = [q_ * e_ for q_, e_ in zip(qst, eg)]
    kdec_t = [jnp.transpose(k_ * jnp.exp(g_ - c_)) for k_, g_, c_ in zip(kst, glast, cbk)]
    s_old = [state_sc[h * DK:(h + 1) * DK, :] for h in range(H)]
    ws = [[_dot(jnp.concatenate([sol[g][j * T:(j + 1) * T, DK:], qdec[g][j * T:(j + 1) * T]], axis=0),
                s_old[h], hi) for j, h in enumerate(groups[g])] for g in range(n_g)]
    vnew_st = [jnp.concatenate([sol[g][j * T:(j + 1) * T, :DK] - ws[g][j][:T] for j in range(G)], axis=0)
               for g in range(n_g)]
    o_st = [jnp.concatenate([ws[g][j][T:] for j in range(G)], axis=0) + _dot(qkm[g], vnew_st[g], hi)
            for g in range(n_g)]
    o_heads = [None] * H
    for g in range(n_g):
        for j, h in enumerate(groups[g]):
            vm = jnp.where(rowhead == j, vnew_st[g], 0.0)
            last = jnp.exp(jnp.broadcast_to(gc[T - 1:T, H + h:H + h + 1], (DK, GDN_DV)))
            state_sc[h * DK:(h + 1) * DK, :] = s_old[h] * last + _dot(kdec_t[g], vm, hi)
            o_heads[h] = o_st[g][j * T:(j + 1) * T]
    gate = gate_ref[...]
    outs = []
    for h in range(H):
        oh = o_heads[h]
        oh = oh * lax.rsqrt(jnp.mean(oh * oh, axis=-1, keepdims=True) + EPS) * nrm_ref[...]
        outs.append(oh * _silu(gate[:, h * GDN_DV:(h + 1) * GDN_DV]))
    o_ref[...] = jnp.concatenate(outs, axis=-1).astype(o_ref.dtype)

    @pl.when(c == nc - 1)
    def _():
        sout_ref[...] = state_sc[...]


def _gdn(qkv, gate, ba, cst, s0, cw, dtb, alog, nrm, T, n_valid, hi):
    B, Lp, _ = qkv.shape
    hk = GDN_HEADS * GDN_DK
    bb_n = 1
    row = lambda b, c: (0, 0)
    return pl.pallas_call(
        functools.partial(_gdn_kernel, bb_n=bb_n, T=T, n_valid=n_valid, hi=hi),
        grid=(B // bb_n, Lp // T),
        in_specs=[pl.BlockSpec((bb_n, T, GDN_QKV), lambda b, c: (b, c, 0)),
                  pl.BlockSpec((bb_n, T, GDN_HEADS * GDN_DV), lambda b, c: (b, c, 0)),
                  pl.BlockSpec((bb_n, T, LANES), lambda b, c: (b, c, 0)),
                  pl.BlockSpec((bb_n, GDN_CONV - 1, GDN_QKV), lambda b, c: (b, 0, 0)),
                  pl.BlockSpec((bb_n, hk, GDN_DV), lambda b, c: (b, 0, 0)),
                  pl.BlockSpec((GDN_CONV, GDN_QKV), row),
                  pl.BlockSpec((1, LANES), row),
                  pl.BlockSpec((1, LANES), row),
                  pl.BlockSpec((1, GDN_DV), row)],
        out_specs=[pl.BlockSpec((bb_n, T, GDN_HEADS * GDN_DV), lambda b, c: (b, c, 0)),
                   pl.BlockSpec((bb_n, hk, GDN_DV), lambda b, c: (b, 0, 0)),
                   pl.BlockSpec((bb_n, GDN_CONV - 1, GDN_QKV), lambda b, c: (b, 0, 0))],
        out_shape=[jax.ShapeDtypeStruct((B, Lp, GDN_HEADS * GDN_DV), _act_dtype(hi)),
                   jax.ShapeDtypeStruct((B, hk, GDN_DV), F32),
                   jax.ShapeDtypeStruct((B, GDN_CONV - 1, GDN_QKV), F32)],
        scratch_shapes=[pltpu.VMEM((bb_n, hk, GDN_DV), F32),
                        pltpu.VMEM((bb_n, T + SUBLANES, GDN_QKV), F32)],
        compiler_params=_cparams(("parallel", "arbitrary")),
        name="gdn_scan",
    )(qkv, gate, ba, cst, s0, cw, dtb, alog, nrm)


def _sconv_kernel(b_ref, c_ref, v_ref, cst_ref, w_ref, o_ref, cout_ref, cbuf, *, T):
    i = pl.program_id(1)
    halo = SC_CONV - 1
    base = SUBLANES - halo

    @pl.when(i == 0)
    def _():
        cbuf[base:SUBLANES, :] = cst_ref[0]

    cbuf[SUBLANES:SUBLANES + T, :] = c_ref[0] * v_ref[0]
    conv = w_ref[0:1, :] * cbuf[base:base + T, :]
    for k in range(1, SC_CONV):
        conv = conv + w_ref[k:k + 1, :] * cbuf[base + k:base + k + T, :]
    o_ref[0] = (b_ref[0] * conv).astype(o_ref.dtype)
    cout_ref[0] = cbuf[base + T:SUBLANES + T, :]
    cbuf[base:SUBLANES, :] = cbuf[base + T:SUBLANES + T, :]


def _sconv(scb, scc, scv, cst, w, T, hi):
    B, L, _ = scb.shape
    T = min(T, L)
    blk = pl.BlockSpec((1, T, SC_WIDTH), lambda b, i: (b, i, 0))
    st = pl.BlockSpec((1, SC_CONV - 1, SC_WIDTH), lambda b, i: (b, 0, 0))
    return pl.pallas_call(
        functools.partial(_sconv_kernel, T=T),
        grid=(B, L // T),
        in_specs=[blk, blk, blk, st, pl.BlockSpec((SC_CONV, SC_WIDTH), lambda b, i: (0, 0))],
        out_specs=[blk, st],
        out_shape=[jax.ShapeDtypeStruct((B, L, SC_WIDTH), _act_dtype(hi)),
                   jax.ShapeDtypeStruct((B, SC_CONV - 1, SC_WIDTH), F32)],
        scratch_shapes=[pltpu.VMEM((T + SUBLANES, SC_WIDTH), F32)],
        compiler_params=_cparams(("parallel", "arbitrary")),
        name="short_conv",
    )(scb, scc, scv, cst, w)


def _out_kernel(y_ref, o_ref, x_ref, mod_ref, wy_ref, wo_ref, g_ref, wr_ref, br_ref,
                xn_ref, h_ref, lg_ref, *, hi):
    mix = _dot(y_ref[0], wy_ref[...], hi) + _dot(o_ref[0], wo_ref[...], hi)
    xn = x_ref[0] + mod_ref[0, 2:3, :] * mix
    xn_ref[0] = xn
    h = xn * lax.rsqrt(jnp.mean(xn * xn, axis=-1, keepdims=True) + EPS) * g_ref[...]
    h = h * (1.0 + mod_ref[0, 4:5, :]) + mod_ref[0, 3:4, :]
    h_ref[0] = h
    lg_ref[0] = _dot_split(h, wr_ref[...]) + br_ref[...]


def _out_proj(y, o, x, mod, wy, wo, g, wr, br, tm, hi):
    B, L, _ = x.shape
    tm = min(tm, L)
    blk = lambda dt_w: pl.BlockSpec((1, tm, dt_w), lambda b, i: (b, i, 0))
    full = lambda s: pl.BlockSpec(s, lambda b, i: (0, 0))
    return pl.pallas_call(
        functools.partial(_out_kernel, hi=hi),
        grid=(B, L // tm),
        in_specs=[blk(D_MODEL), blk(D_MODEL), blk(D_MODEL),
                  pl.BlockSpec((1, 6, D_MODEL), lambda b, i: (b, 0, 0)),
                  full((D_MODEL, D_MODEL)), full((D_MODEL, D_MODEL)), full((1, D_MODEL)),
                  full((D_MODEL, LANES)), full((1, LANES))],
        out_specs=[blk(D_MODEL), blk(D_MODEL), blk(LANES)],
        out_shape=[jax.ShapeDtypeStruct((B, L, D_MODEL), F32),
                   jax.ShapeDtypeStruct((B, L, D_MODEL), F32),
                   jax.ShapeDtypeStruct((B, L, LANES), F32)],
        compiler_params=_cparams(("parallel", "parallel")),
        name="out_proj",
    )(y, o, x, mod, wy, wo, g, wr, br)


def _route_kernel(lg_ref, e_ref, rank_ref, gate_ref, cnt_ref, base_sc, *, tm):
    i = pl.program_id(0)

    @pl.when(i == 0)
    def _():
        base_sc[...] = jnp.zeros_like(base_sc)

    lg = lg_ref[...]
    lane_i = lax.broadcasted_iota(I32, (tm, LANES), 1)
    lane = lane_i.astype(F32)
    vals, idxs = [], []
    cur = lg
    for _ in range(TOP_K):
        m = jnp.max(cur, axis=-1, keepdims=True)
        idx = jnp.min(jnp.where(cur == m, lane, float(LANES)), axis=-1, keepdims=True)
        vals.append(m)
        idxs.append(idx)
        cur = jnp.where(lane == idx, -jnp.inf, cur)
    ex = [jnp.exp(v - vals[0]) for v in vals]
    den = ex[0] + ex[1] + ex[2] + ex[3]
    onehot = jnp.zeros((tm, LANES), F32)
    for idx in idxs:
        onehot = onehot + jnp.where(lane == idx, 1.0, 0.0)
    ri = lax.broadcasted_iota(I32, (tm, tm), 0)
    ci = lax.broadcasted_iota(I32, (tm, tm), 1)
    before = _dot(jnp.where(ci < ri, 1.0, 0.0), onehot) + base_sc[...]
    e_out = jnp.zeros((tm, LANES), I32)
    r_out = jnp.zeros((tm, LANES), I32)
    g_out = jnp.zeros((tm, LANES), F32)
    for k in range(TOP_K):
        rk = jnp.sum(jnp.where(lane == idxs[k], before, 0.0), axis=-1, keepdims=True)
        e_out = jnp.where(lane_i == k, idxs[k].astype(I32), e_out)
        r_out = jnp.where(lane_i == k, rk.astype(I32), r_out)
        g_out = jnp.where(lane_i == k, ex[k] / den, g_out)
    e_ref[...] = e_out
    rank_ref[...] = r_out
    gate_ref[...] = g_out
    base_sc[...] = base_sc[...] + jnp.sum(onehot, axis=0, keepdims=True)
    cnt_ref[...] = base_sc[...].astype(I32)


def _route(logits, tm):
    n_tok = logits.shape[0]
    tm = min(tm, n_tok)
    blk = pl.BlockSpec((tm, LANES), lambda i: (i, 0))
    return pl.pallas_call(
        functools.partial(_route_kernel, tm=tm),
        grid=(n_tok // tm,),
        in_specs=[blk],
        out_specs=[blk, blk, blk, pl.BlockSpec((1, LANES), lambda i: (0, 0))],
        out_shape=[jax.ShapeDtypeStruct((n_tok, LANES), I32),
                   jax.ShapeDtypeStruct((n_tok, LANES), I32),
                   jax.ShapeDtypeStruct((n_tok, LANES), F32),
                   jax.ShapeDtypeStruct((1, LANES), I32)],
        scratch_shapes=[pltpu.VMEM((1, LANES), F32)],
        compiler_params=_cparams(("arbitrary",)),
        name="moe_route",
    )(logits)


def _dispatch_kernel(zs_ref, nz_ref, dest_ref, h_ref, out_hbm, zbuf, sem, zsem, *, tm, n_blocks):
    bm = zbuf.shape[0]

    def zero_copy(row0):
        return pltpu.make_async_copy(zbuf, out_hbm.at[pl.ds(row0, bm), :], zsem)

    @pl.when(pl.program_id(0) == 0)
    def _():
        zbuf[...] = jnp.zeros_like(zbuf)
        n_used = nz_ref[0]
        for e in range(N_EXPERTS):
            @pl.when(zs_ref[e] >= 0)
            def _(e=e):
                zero_copy(pl.multiple_of(zs_ref[e], bm)).start()

        def tail(j, carry):
            zero_copy(pl.multiple_of(j * bm, bm)).start()
            return carry

        lax.fori_loop(n_used, n_blocks, tail, 0)

        def drain(j, carry):
            zero_copy(0).wait()
            return carry

        lax.fori_loop(0, nz_ref[1], drain, 0)

    def issue(r, carry):
        for k in range(TOP_K):
            d = dest_ref[r * TOP_K + k]
            pltpu.make_async_copy(h_ref.at[pl.ds(r, 1), :], out_hbm.at[pl.ds(d, 1), :], sem).start(priority=k % 2)
        return carry

    lax.fori_loop(0, tm, issue, 0)
    for _ in range(TOP_K):
        pltpu.make_async_copy(h_ref, out_hbm.at[pl.ds(0, tm), :], sem).wait()


def _dispatch(h, dest_flat, zero_start, zero_counts, n_rows, tm, bm):
    n_tok = h.shape[0]
    tm = min(tm, n_tok)
    grid_spec = pltpu.PrefetchScalarGridSpec(
        num_scalar_prefetch=2,
        grid=(n_tok // tm,),
        in_specs=[pl.BlockSpec((tm * TOP_K,), lambda i, zs, nz: (i,), memory_space=pltpu.SMEM),
                  pl.BlockSpec((tm, D_MODEL), lambda i, zs, nz: (i, 0))],
        out_specs=pl.BlockSpec(memory_space=pl.ANY),
        scratch_shapes=[pltpu.VMEM((bm, D_MODEL), F32), pltpu.SemaphoreType.DMA(()),
                        pltpu.SemaphoreType.DMA(())],
    )
    return pl.pallas_call(
        functools.partial(_dispatch_kernel, tm=tm, n_blocks=n_rows // bm),
        grid_spec=grid_spec,
        out_shape=jax.ShapeDtypeStruct((n_rows, D_MODEL), F32),
        compiler_params=_cparams(("arbitrary",)),
        name="moe_dispatch",
    )(zero_start, zero_counts, dest_flat, h)


def _ffn_kernel(blk_e_ref, nused_ref, x_ref, wgu_ref, bgu_ref, wd_ref, bd_ref, o_ref, wgu_sc, wd_sc):
    i = pl.program_id(0)
    prev = blk_e_ref[jnp.maximum(i - 1, 0)]
    fresh = jnp.logical_or(i == 0, blk_e_ref[i] != prev)
    active = i < nused_ref[0]

    @pl.when(jnp.logical_and(active, fresh))
    def _():
        wgu_sc[...] = wgu_ref[0, 0].astype(BF16)
        wd_sc[...] = wd_ref[0, 0].astype(BF16)

    @pl.when(active)
    def _():
        gu = jnp.dot(x_ref[...].astype(BF16), wgu_sc[...], preferred_element_type=F32) + bgu_ref[0]
        gate = jnp.minimum(gu[:, :D_FF], SWIGLU_LIMIT)
        up = jnp.clip(gu[:, D_FF:], -SWIGLU_LIMIT, SWIGLU_LIMIT)
        act = (up + 1.0) * gate * _sigmoid(SWIGLU_ALPHA * gate)
        o_ref[...] = jnp.dot(act.astype(BF16), wd_sc[...], preferred_element_type=F32) + bd_ref[0]

    @pl.when(jnp.logical_not(active))
    def _():
        o_ref[...] = jnp.zeros_like(o_ref)


def _expert_ffn(xin, blk_e, n_used, w_gu, b_gu, w_down, b_down, layer, bm):
    n_rows = xin.shape[0]
    n_blocks = n_rows // bm

    def row_map(i, be, nu):
        return (jnp.minimum(i, nu[0] - 1), 0)

    def e_map4(i, be, nu):
        return (layer, be[jnp.minimum(i, nu[0] - 1)], 0, 0)

    def e_map3(i, be, nu):
        return (layer * N_EXPERTS + be[jnp.minimum(i, nu[0] - 1)], 0, 0)

    grid_spec = pltpu.PrefetchScalarGridSpec(
        num_scalar_prefetch=2,
        grid=(n_blocks,),
        in_specs=[pl.BlockSpec((bm, D_MODEL), row_map),
                  pl.BlockSpec((1, 1, D_MODEL, 2 * D_FF), e_map4),
                  pl.BlockSpec((1, 1, 2 * D_FF), e_map3),
                  pl.BlockSpec((1, 1, D_FF, D_MODEL), e_map4),
                  pl.BlockSpec((1, 1, D_MODEL), e_map3)],
        out_specs=pl.BlockSpec((bm, D_MODEL), lambda i, be, nu: (i, 0)),
        scratch_shapes=[pltpu.VMEM((D_MODEL, 2 * D_FF), BF16),
                        pltpu.VMEM((D_FF, D_MODEL), BF16)],
    )
    return pl.pallas_call(
        _ffn_kernel,
        grid_spec=grid_spec,
        out_shape=jax.ShapeDtypeStruct((n_rows, D_MODEL), F32),
        compiler_params=_cparams(("arbitrary",)),
        name="moe_expert_ffn",
    )(blk_e, n_used, xin, w_gu, b_gu.reshape(DEPTH * N_EXPERTS, 1, 2 * D_FF), w_down,
      b_down.reshape(DEPTH * N_EXPERTS, 1, D_MODEL))


def _combine_kernel(dest_ref, dest_next_ref, f_hbm, gate_ref, x_ref, mod_ref, g_ref, o_ref, buf, sem, *, tm, final):
    i = pl.program_id(0)
    slot = lax.rem(i, 2)

    def fetch(idx_ref, s):
        def issue(r, carry):
            for k in range(TOP_K):
                d = idx_ref[r * TOP_K + k]
                pltpu.make_async_copy(f_hbm.at[pl.ds(d, 1), :], buf.at[s, k, pl.ds(r, 1), :],
                                      sem.at[s]).start(priority=k % 2)
            return carry

        lax.fori_loop(0, tm, issue, 0)

    @pl.when(i == 0)
    def _():
        fetch(dest_ref, 0)

    @pl.when(i + 1 < pl.num_programs(0))
    def _():
        fetch(dest_next_ref, 1 - slot)

    for k in range(TOP_K):
        pltpu.make_async_copy(f_hbm.at[pl.ds(0, tm), :], buf.at[slot, k], sem.at[slot]).wait()
    gates = gate_ref[...]
    moe = gates[:, 0:1] * buf[slot, 0]
    for k in range(1, TOP_K):
        moe = moe + gates[:, k:k + 1] * buf[slot, k]
    xo = x_ref[...] + mod_ref[0, 5:6, :] * moe
    if final:
        xo = xo * lax.rsqrt(jnp.mean(xo * xo, axis=-1, keepdims=True) + EPS) * g_ref[...]
    o_ref[...] = xo


def _combine(ffn_out, dest_flat, gates, x, mod, g_final, tiles_per_batch, tm, final):
    n_tok = x.shape[0]
    n_tiles = n_tok // tm
    return pl.pallas_call(
        functools.partial(_combine_kernel, tm=tm, final=final),
        grid=(n_tiles,),
        in_specs=[pl.BlockSpec((tm * TOP_K,), lambda i: (i,), memory_space=pltpu.SMEM),
                  pl.BlockSpec((tm * TOP_K,), lambda i: (jnp.minimum(i + 1, n_tiles - 1),),
                               memory_space=pltpu.SMEM),
                  pl.BlockSpec(memory_space=pl.ANY),
                  pl.BlockSpec((tm, LANES), lambda i: (i, 0)),
                  pl.BlockSpec((tm, D_MODEL), lambda i: (i, 0)),
                  pl.BlockSpec((1, 6, D_MODEL), lambda i: (i // tiles_per_batch, 0, 0)),
                  pl.BlockSpec((1, D_MODEL), lambda i: (0, 0))],
        out_specs=pl.BlockSpec((tm, D_MODEL), lambda i: (i, 0)),
        out_shape=jax.ShapeDtypeStruct((n_tok, D_MODEL), F32),
        scratch_shapes=[pltpu.VMEM((2, TOP_K, tm, D_MODEL), F32), pltpu.SemaphoreType.DMA((2,))],
        compiler_params=_cparams(("arbitrary",)),
        name="moe_combine",
    )(dest_flat, dest_flat, ffn_out, gates, x, mod, g_final)


def _pad_cols(w, width):
    return jnp.pad(w, ((0, 0), (0, width - w.shape[1])))


def _pad_lanes(v, offset=0):
    return jnp.pad(v.astype(F32), (offset, LANES - offset - v.shape[0])).reshape(1, LANES)


def _rope_tables(pos):
    half = MLA_ROPE // 2
    inv = ROPE_THETA ** (-jnp.arange(half, dtype=F32) / half)
    ang = pos.astype(F32)[:, None] * inv[None, :]
    cos, sin = jnp.cos(ang), jnp.sin(ang)
    z = jnp.zeros_like(cos)
    lead = jnp.zeros((pos.shape[0], ROPE_LANE), F32)
    pad = jnp.zeros((pos.shape[0], LANES - ROPE_LANE - MLA_ROPE), F32)
    cos_t = jnp.concatenate([lead, cos, cos, pad], axis=1)
    sin_a = jnp.concatenate([lead, -sin, z, pad], axis=1)
    sin_b = jnp.concatenate([lead, z, sin, pad], axis=1)
    return cos_t, sin_a, sin_b


def _pad_seq(t, lp):
    return jnp.pad(t, ((0, 0), (0, lp - t.shape[1]), (0, 0)))


def _even_weights(W, j, dt):
    w_in = W['ev_w_in'][j]
    o1 = SSD_D_INNER
    o2 = o1 + SSD_XBC
    o3 = o2 + SSD_HEADS
    o4 = o3 + MLA_Q_RANK
    w_ssd = jnp.concatenate([w_in[:, :o2], _pad_cols(w_in[:, o2:o3], LANES)], axis=1).astype(dt)
    o5 = o4 + MLA_KV_RANK
    zc = lambda n: jnp.zeros((D_MODEL, n), F32)
    w_mla = jnp.concatenate([w_in[:, o3:o4], w_in[:, o4:o5], zc(ROPE_LANE), w_in[:, o5:],
                             zc(LANES - ROPE_LANE - MLA_ROPE)], axis=1).astype(dt)
    wq = W['mla_w_q_up'][j].reshape(MLA_Q_RANK, MLA_HEADS, MLA_NOPE + MLA_ROPE)
    wq = jnp.pad(wq, ((0, 0), (0, 0), (0, LANES - MLA_NOPE - MLA_ROPE)))
    wq = jnp.transpose(wq, (1, 0, 2)).astype(dt)
    wkv = jnp.transpose(W['mla_w_kv_up'][j].reshape(MLA_KV_RANK, MLA_HEADS, MLA_NOPE + MLA_V),
                        (1, 0, 2)).astype(dt)
    expand = (jnp.arange(LANES)[:, None] == (jnp.arange(SSD_D_INNER) // SSD_HEADDIM)[None, :]).astype(F32)
    wuk = jnp.pad(jnp.transpose(wkv[:, :, :MLA_NOPE], (0, 2, 1)), ((0, 0), (0, LANES - MLA_NOPE), (0, 0)))
    wuv = jnp.pad(wkv[:, :, MLA_NOPE:], ((0, 0), (0, 0), (0, LANES - MLA_V)))
    return dict(
        w_ssd=w_ssd, w_mla=w_mla, wq=wq, wkv=wkv, wuk=wuk, wuv=wuv, expand=expand,
        wy=W['ev_w_out'][j][:SSD_D_INNER].astype(dt), wo=W['ev_w_out'][j][SSD_D_INNER:].astype(dt),
        cw=W['ssd_conv_w'][j], cb=W['ssd_conv_b'][j].reshape(1, SSD_XBC),
        dtb=_pad_lanes(W['ssd_dt_bias'][j]), alog=_pad_lanes(W['ssd_a_log'][j]),
        dsk=jnp.repeat(W['ssd_d'][j].astype(F32), SSD_HEADDIM).reshape(1, SSD_D_INNER),
        nrm=W['ssd_norm'][j].reshape(1, SSD_D_INNER),
        qn=W['mla_q_norm'][j].reshape(1, MLA_Q_RANK), kvn=W['mla_kv_norm'][j].reshape(1, MLA_KV_RANK))


def _odd_weights(W, j, dt):
    w_in = W['od_w_in'][j]
    o1 = GDN_QKV
    o2 = o1 + GDN_HEADS * GDN_DV
    o3 = o2 + 2 * GDN_HEADS
    w_gdn = jnp.concatenate([w_in[:, :o2], _pad_cols(w_in[:, o2:o3], LANES)], axis=1).astype(dt)
    w_sc = w_in[:, o3:].astype(dt)
    return dict(
        w_gdn=w_gdn, w_sc=w_sc,
        wy=W['od_w_out'][j][:GDN_HEADS * GDN_DV].astype(dt), wo=W['od_w_out'][j][GDN_HEADS * GDN_DV:].astype(dt),
        cw=W['gdn_conv_w'][j], dtb=_pad_lanes(W['gdn_dt_bias'][j], GDN_HEADS),
        alog=_pad_lanes(W['gdn_a_log'][j], GDN_HEADS), nrm=W['gdn_norm'][j].reshape(1, GDN_DV),
        scw=W['sconv_w'][j])


def _moe(h, logits, x_new, mod, W, i, g_final, L, final):
    n_tok = h.shape[0]
    bm = MOE_BM if n_tok * TOP_K >= N_EXPERTS * MOE_BM else MOE_BM_SMALL
    e_pad, rank_pad, gates, cnt = _route(logits, 512)
    counts = cnt[0, :N_EXPERTS]
    padded = (counts + bm - 1) // bm * bm
    pend = jnp.cumsum(padded)
    pstart = pend - padded
    e_sel = e_pad[:, :TOP_K]
    dest = (pstart[e_sel] + rank_pad[:, :TOP_K]).astype(I32).reshape(-1)
    n_blocks = n_tok * TOP_K // bm + N_EXPERTS
    n_rows = n_blocks * bm
    blk_start = jnp.arange(n_blocks, dtype=pend.dtype) * bm
    blk_e = jnp.minimum(jnp.sum((blk_start[:, None] >= pend[None, :]).astype(I32), axis=1), N_EXPERTS - 1)
    n_used = (pend[-1:] // bm).astype(I32)
    zero_start = jnp.where(counts > 0, pend - bm, -1).astype(I32)
    n_zero = jnp.sum((counts > 0).astype(I32)) + n_blocks - n_used[0]
    zero_counts = jnp.stack([n_used[0], n_zero]).astype(I32)
    xin = _dispatch(h, dest, zero_start, zero_counts, n_rows, DISPATCH_TM, bm)
    f_out = _expert_ffn(xin, blk_e, n_used, W['w_gu'], W['b_gu'], W['w_down'], W['b_down'], i, bm)
    tm = min(256, L)
    return _combine(f_out, dest, gates, x_new, mod, g_final, L // tm, tm, final)


def _trunk(x, c_mod, pos0, caches, W, PW, seq_t, hi):
    B, L, _ = x.shape
    new = {}
    pos = pos0 + jnp.arange(L, dtype=I32)
    cos_t, sin_a, sin_b = _rope_tables(pos)
    wr_all = W['w_router']
    for i in range(DEPTH):
        mod = c_mod[i]
        g_mix = W['norm_mix'][i].reshape(1, D_MODEL)
        j = i // 2
        if i % 2 == 0:
            P = PW[i]
            z, xbc, dtr = _in_proj(x, mod, g_mix, P['w_ssd'],
                                   ((0, SSD_D_INNER), (SSD_D_INNER, SSD_XBC), (SSD_D_INNER + SSD_XBC, LANES)),
                                   (F32, F32, F32), 0, 1, 512, hi)
            latq, latkv = _in_proj(x, mod, g_mix, P['w_mla'], ((0, MLA_Q_RANK), (MLA_Q_RANK, MLA_Q_RANK)),
                                   (F32, F32), 0, 1, 1024, hi)
            T = seq_t['ssd']
            lp = -(-L // T) * T
            y, s_new, cst_new = _ssd(_pad_seq(z, lp), _pad_seq(xbc, lp), _pad_seq(dtr, lp),
                                     caches['ssd_conv'][j], caches['ssd'][j].reshape(B, -1, SSD_STATE),
                                     P['cw'], P['cb'], P['dtb'], P['alog'], P['dsk'], P['nrm'], P['expand'], T, L, hi)
            y = y[:, :L]
            new['ssd'] = s_new.reshape(1, B, SSD_HEADS, SSD_HEADDIM, SSD_STATE)
            new['ssd_conv'] = cst_new[None]
            ckv_new, kpe_new = _latkv_post(latkv, P['kvn'], cos_t, sin_a, sin_b, 1024)
            new['mla_ckv'] = ckv_new[None]
            new['mla_krope'] = kpe_new[None, :, :, ROPE_LANE:ROPE_LANE + MLA_ROPE]
            ckv_past, kpe_past = caches['mla_ckv'][j], caches['mla_krope'][j]
            past = ckv_past.shape[1]
            kv_len = past + L
            tk = seq_t['tk']
            lk = -(-kv_len // tk) * tk
            ckv_all = _pad_seq(jnp.concatenate([ckv_past, ckv_new], axis=1), lk)
            kpe_all = _pad_seq(jnp.concatenate(
                [jnp.pad(kpe_past, ((0, 0), (0, 0), (ROPE_LANE, LANES - ROPE_LANE - MLA_ROPE))), kpe_new],
                axis=1), lk)
            q = _q_proj(latq, P['qn'], P['wq'], cos_t, sin_a, sin_b, 1024, hi)
            if seq_t['latent']:
                o = _attention_latent(q, ckv_all, kpe_all, P['wuk'], P['wuv'], pos0, kv_len, hi)
            else:
                kv = _kv_up(ckv_all, kpe_all, P['wkv'], min(tk, 512), hi)
                o = _attention(q, kv, min(seq_t['tq'], L), tk, seq_t['tkm'], pos0, kv_len, hi)
        else:
            P = PW[i]
            hv = GDN_HEADS * GDN_DV
            qkv, gate, ba = _in_proj(x, mod, g_mix, P['w_gdn'],
                                     ((0, GDN_QKV), (GDN_QKV, hv), (GDN_QKV + hv, LANES)),
                                     (F32, F32, F32), 0, 1, 512, hi)
            scb, scc, scv = _in_proj(x, mod, g_mix, P['w_sc'],
                                     ((0, SC_WIDTH), (SC_WIDTH, SC_WIDTH), (2 * SC_WIDTH, SC_WIDTH)),
                                     (F32, F32, F32), 0, 1, 512, hi)
            T = CHUNK
            lp = -(-L // T) * T
            y, s_new, cst_new = _gdn(_pad_seq(qkv, lp), _pad_seq(gate, lp), _pad_seq(ba, lp),
                                     caches['gdn_conv'][j], caches['gdn'][j].reshape(B, -1, GDN_DV),
                                     P['cw'], P['dtb'], P['alog'], P['nrm'], T, L, hi)
            y = y[:, :L]
            new['gdn'] = s_new.reshape(1, B, GDN_HEADS, GDN_DK, GDN_DV)
            new['gdn_conv'] = cst_new[None]
            o, sc_new = _sconv(scb, scc, scv, caches['sconv'][j], P['scw'], 512, hi)
            new['sconv'] = sc_new[None]
        wr = _pad_cols(wr_all[i], LANES)
        br = jnp.concatenate([W['b_router'][i].astype(F32), jnp.full((LANES - N_EXPERTS,), NEG_BIG, F32)]).reshape(1, LANES)
        x_new, h, logits = _out_proj(y, o, x, mod, P['wy'], P['wo'], W['norm_ffn'][i].reshape(1, D_MODEL),
                                     wr, br, 512, hi)
        final = i == DEPTH - 1
        xo = _moe(h.reshape(B * L, D_MODEL), logits.reshape(B * L, LANES), x_new.reshape(B * L, D_MODEL),
                  mod, W, i, W['norm_final'].reshape(1, D_MODEL), L, final)
        x = xo.reshape(B, L, D_MODEL)
    return x, new


def _prep_weights(W, dt):
    PW = {}
    for i in range(DEPTH):
        PW[i] = _even_weights(W, i // 2, dt) if i % 2 == 0 else _odd_weights(W, i // 2, dt)
    return PW


def kernel(x_prompt, x_sample, c_prompt, c_sample, cache_mla_ckv, cache_mla_krope, state_ssd, state_ssd_conv, state_gdn, state_gdn_conv, state_sconv, norm_mix, norm_ffn, w_ada, b_ada, w_router, b_router, w_gu, b_gu, w_down, b_down, norm_final, ev_w_in, ev_w_out, ssd_conv_w, ssd_conv_b, ssd_dt_bias, ssd_a_log, ssd_d, ssd_norm, mla_q_norm, mla_w_q_up, mla_kv_norm, mla_w_kv_up, od_w_in, od_w_out, gdn_conv_w, gdn_dt_bias, gdn_a_log, gdn_norm, sconv_w):
    W = dict(norm_mix=norm_mix, norm_ffn=norm_ffn, w_ada=w_ada, b_ada=b_ada, w_router=w_router,
             b_router=b_router, w_gu=w_gu, b_gu=b_gu, w_down=w_down, b_down=b_down, norm_final=norm_final,
             ev_w_in=ev_w_in, ev_w_out=ev_w_out, ssd_conv_w=ssd_conv_w, ssd_conv_b=ssd_conv_b,
             ssd_dt_bias=ssd_dt_bias, ssd_a_log=ssd_a_log, ssd_d=ssd_d, ssd_norm=ssd_norm,
             mla_q_norm=mla_q_norm, mla_w_q_up=mla_w_q_up, mla_kv_norm=mla_kv_norm, mla_w_kv_up=mla_w_kv_up,
             od_w_in=od_w_in, od_w_out=od_w_out, gdn_conv_w=gdn_conv_w, gdn_dt_bias=gdn_dt_bias,
             gdn_a_log=gdn_a_log, gdn_norm=gdn_norm, sconv_w=sconv_w)
    bp, bs = x_prompt.shape[0], x_sample.shape[0]
    nb = 16
    c_all = jnp.concatenate([c_prompt, c_sample, jnp.zeros((nb - bp - bs, D_MODEL), F32)], axis=0)
    mod_all = _ada_mod(c_all, w_ada, b_ada).reshape(DEPTH, nb, 6, D_MODEL)
    n_even, n_odd = (DEPTH + 1) // 2, DEPTH // 2
    zero_caches = dict(
        mla_ckv=jnp.zeros((n_even, bp, 0, MLA_KV_RANK), F32), mla_krope=jnp.zeros((n_even, bp, 0, MLA_ROPE), F32),
        ssd=jnp.zeros((n_even, bp, SSD_HEADS, SSD_HEADDIM, SSD_STATE), F32),
        ssd_conv=jnp.zeros((n_even, bp, SSD_CONV - 1, SSD_XBC), F32),
        gdn=jnp.zeros((n_odd, bp, GDN_HEADS, GDN_DK, GDN_DV), F32),
        gdn_conv=jnp.zeros((n_odd, bp, GDN_CONV - 1, GDN_QKV), F32),
        sconv=jnp.zeros((n_odd, bp, SC_CONV - 1, SC_WIDTH), F32))
    y_p, sp = _trunk(x_prompt, mod_all[:, :bp], 0, zero_caches, W, _prep_weights(W, BF16),
                     dict(ssd=256, tq=1024, tk=1024, tkm=512, latent=False), False)
    past = cache_mla_ckv.shape[2]
    caches = dict(mla_ckv=cache_mla_ckv, mla_krope=cache_mla_krope, ssd=state_ssd, ssd_conv=state_ssd_conv,
                  gdn=state_gdn, gdn_conv=state_gdn_conv, sconv=state_sconv)
    y_s, ss = _trunk(x_sample, mod_all[:, bp:bp + bs], past, caches, W, _prep_weights(W, F32),
                     dict(ssd=128, tq=32, tk=256, tkm=256, latent=True), True)
    return (y_p, y_s,
            sp['mla_ckv'], ss['mla_ckv'], sp['mla_krope'], ss['mla_krope'],
            sp['ssd'], ss['ssd'], sp['ssd_conv'], ss['ssd_conv'],
            sp['gdn'], ss['gdn'], sp['gdn_conv'], ss['gdn_conv'],
            sp['sconv'], ss['sconv'])
```
